```python
import math
import jax, jax.numpy as jnp
from jax import lax
import numpy as np

D_MODEL = 2048
BATCH = 2
SEQ = 8192
DEPTH = 1

N_META = 16
D_CONV = D_MODEL
CONV_W = 3
CONV_GROUPS = 8
MLSTM_HEADS = 8
D_MV = D_MODEL
D_MQK = D_MODEL // 2
HEAD_V = D_MV // MLSTM_HEADS
HEAD_QK = D_MQK // MLSTM_HEADS
CHUNK = 64
PEER_HEADS = 8
N_KEYS = 128
N_EXPERTS = N_KEYS * N_KEYS
D_KEY = 256
HALF_KEY = D_KEY // 2
TOPK = 16
PEER_BLOCK = 128
N_IN = 3 * D_CONV + 2 * D_MQK + 2 * D_MV + 2 * MLSTM_HEADS + 2 * D_MODEL
ALPHA = (2 * DEPTH) ** 0.25
BETA = (8 * DEPTH) ** -0.25
LN_EPS = 1e-5
I_GATE_PAD = -1e30

kernel_name = "hybrid_conv_mlstm_peer_block"


def layer_norm(x, g, b):
    xf = x.astype(jnp.float32)
    mu = xf.mean(-1, keepdims=True)
    var = jnp.square(xf - mu).mean(-1, keepdims=True)
    y = (xf - mu) * lax.rsqrt(var + LN_EPS)
    return (y * g + b).astype(x.dtype)


def short_conv_mixer(b_gate, c_gate, xc, conv_w):
    z = c_gate * xc
    L = z.shape[1]
    zp = jnp.pad(z, ((0, 0), (CONV_W - 1, 0), (0, 0)))
    y = sum(conv_w[k] * zp[:, k:k + L] for k in range(CONV_W))
    return b_gate * y


def mlstm_chunkwise(q, k, v, i_pre, f_pre):
    Bsz, L = q.shape[:2]
    pad = (-L) % CHUNK

    def lead_pad(a, value=0.0):
        return jnp.pad(a, ((0, 0), (pad, 0)) + ((0, 0),) * (a.ndim - 2), constant_values=value)

    q = lead_pad(q.astype(jnp.float32) * (HEAD_QK ** -0.5))
    k = lead_pad(k.astype(jnp.float32))
    v = lead_pad(v.astype(jnp.float32))
    log_i = lead_pad(i_pre.astype(jnp.float32), I_GATE_PAD)
    log_f = lead_pad(jax.nn.log_sigmoid(f_pre.astype(jnp.float32)))
    Lp = L + pad
    nc = Lp // CHUNK

    def to_chunks(a):
        a = a.reshape((Bsz, nc, CHUNK) + a.shape[2:])
        return jnp.moveaxis(a, (1, 3), (0, 2))

    xs = (to_chunks(q), to_chunks(k), to_chunks(v), to_chunks(log_i), to_chunks(log_f))
    causal = jnp.tril(jnp.ones((CHUNK, CHUNK), dtype=bool))

    def step(carry, xs_c):
        C, n, m = carry
        qc, kc, vc, li, lf = xs_c
        b = jnp.cumsum(lf, axis=-1)
        d = b[..., :, None] - b[..., None, :] + li[..., None, :]
        d = jnp.where(causal, d, -jnp.inf)
        inter = b + m[..., None]
        m_t = jnp.maximum(inter, d.max(-1))
        w = jnp.exp(d - m_t[..., None])
        s = jnp.einsum('bhtd,bhsd->bhts', qc, kc) * w
        sc_inter = jnp.exp(inter - m_t)
        num = (jnp.einsum('bhts,bhsv->bhtv', s, vc)
               + sc_inter[..., None] * jnp.einsum('bhtd,bhdv->bhtv', qc, C))
        den = s.sum(-1) + sc_inter * jnp.einsum('bhtd,bhd->bht', qc, n)
        h = num / jnp.maximum(jnp.abs(den), jnp.exp(-m_t))[..., None]
        b_last = b[..., -1]
        dl = b_last[..., None] - b + li
        m_new = jnp.maximum(b_last + m, dl.max(-1))
        wl = jnp.exp(dl - m_new[..., None])
        decay = jnp.exp(b_last + m - m_new)
        C_new = decay[..., None, None] * C + jnp.einsum('bhs,bhsd,bhsv->bhdv', wl, kc, vc)
        n_new = decay[..., None] * n + jnp.einsum('bhs,bhsd->bhd', wl, kc)
        return (C_new, n_new, m_new), h

    H = q.shape[2]
    init = (jnp.zeros((Bsz, H, HEAD_QK, HEAD_V), jnp.float32),
            jnp.zeros((Bsz, H, HEAD_QK), jnp.float32),
            jnp.zeros((Bsz, H), jnp.float32))
    _, hs = lax.scan(step, init, xs)
    hs = jnp.transpose(hs, (1, 0, 3, 2, 4)).reshape(Bsz, Lp, H, HEAD_V)
    return hs[:, pad:]


def token_mixer(h, w_in, b_if, conv_w, mh_norm_w, p_a, p_b, w_o):
    Bsz, L, _ = h.shape
    proj = jnp.einsum('bld,dn->bln', h, w_in)
    sizes = [D_CONV, D_CONV, D_CONV, D_MQK, D_MQK, D_MV, D_MV, 2 * MLSTM_HEADS, D_MODEL, D_MODEL]
    offsets = np.cumsum(sizes)[:-1].tolist()
    cb, cc, cx, q, k, v, o, gif, ga, gb = jnp.split(proj, offsets, axis=-1)
    y_a = short_conv_mixer(cb, cc, cx, conv_w)
    i_pre = gif[..., :MLSTM_HEADS] + b_if[:MLSTM_HEADS]
    f_pre = gif[..., MLSTM_HEADS:] + b_if[MLSTM_HEADS:]
    hm = mlstm_chunkwise(q.reshape(Bsz, L, MLSTM_HEADS, HEAD_QK),
                         k.reshape(Bsz, L, MLSTM_HEADS, HEAD_QK),
                         v.reshape(Bsz, L, MLSTM_HEADS, HEAD_V), i_pre, f_pre)
    hm = hm * lax.rsqrt(jnp.mean(jnp.square(hm), -1, keepdims=True) + LN_EPS)
    hm = hm.reshape(Bsz, L, D_MV) * mh_norm_w
    y_b = jax.nn.sigmoid(o) * hm.astype(h.dtype)
    merged = (jax.nn.sigmoid(ga) * jnp.einsum('blc,cd->bld', y_a, p_a)
              + jax.nn.sigmoid(gb) * jnp.einsum('blc,cd->bld', y_b, p_b))
    return jnp.einsum('bld,de->ble', merged, w_o)


def peer_ffn(h, peer_wq, peer_keys, peer_u, peer_v):
    Bsz, L, D = h.shape
    T = Bsz * L
    t = h.reshape(T, D)
    q = (t @ peer_wq).reshape(T, PEER_HEADS, 2, HALF_KEY)
    sc = jnp.einsum('thpk,hpnk->thpn', q, peer_keys).astype(jnp.float32)
    s_top, i_top = lax.top_k(sc, TOPK)
    cand = s_top[..., 0, :, None] + s_top[..., 1, None, :]
    cand_idx = i_top[..., 0, :, None] * N_KEYS + i_top[..., 1, None, :]
    best_s, best_pos = lax.top_k(cand.reshape(T, PEER_HEADS, TOPK * TOPK), TOPK)
    expert_idx = jnp.take_along_axis(cand_idx.reshape(T, PEER_HEADS, TOPK * TOPK), best_pos, axis=-1)
    g = jax.nn.softmax(best_s, axis=-1)
    idx = expert_idx.reshape(T, PEER_HEADS * TOPK)
    g = g.reshape(T, PEER_HEADS * TOPK).astype(h.dtype)
    npad = (-T) % PEER_BLOCK
    tb = jnp.pad(t, ((0, npad), (0, 0))).reshape(-1, PEER_BLOCK, D)
    ib = jnp.pad(idx, ((0, npad), (0, 0))).reshape(-1, PEER_BLOCK, PEER_HEADS * TOPK)
    gbk = jnp.pad(g, ((0, npad), (0, 0))).reshape(-1, PEER_BLOCK, PEER_HEADS * TOPK)

    def eval_block(args):
        xb, eb, wb = args
        u = peer_u[eb]
        a = jax.nn.gelu(jnp.einsum('td,tkd->tk', xb, u), approximate=False)
        return jnp.einsum('tk,tkd->td', wb * a, peer_v[eb])

    out = lax.map(eval_block, (tb, ib, gbk))
    return out.reshape(-1, D)[:T].reshape(Bsz, L, D)


def setup_inputs(seed: int = 0) -> dict:
    key = jax.random.key(seed)
    ks = jax.random.split(key, 24)
    nrm = jax.random.normal
    D = D_MODEL
    H = MLSTM_HEADS
    b_i = 0.1 * nrm(ks[3], (DEPTH, H))
    b_f = jnp.linspace(3.0, 6.0, H)[None, :] + 0.1 * nrm(ks[4], (DEPTH, H))
    return {
        "x": nrm(ks[0], (BATCH, SEQ, D), jnp.float32),
        "meta_tokens": nrm(ks[1], (N_META, D), jnp.float32),
        "ln0_g": 1.0 + 0.1 * nrm(ks[20], (D,)),
        "ln0_b": 0.02 * nrm(ks[21], (D,)),
        "w_in": nrm(ks[2], (DEPTH, D, N_IN)) * D ** -0.5,
        "b_if": jnp.concatenate([b_i, b_f], axis=-1),
        "conv_w": nrm(ks[5], (DEPTH, CONV_W, D_CONV)) * CONV_W ** -0.5,
        "mh_norm_w": 1.0 + 0.1 * nrm(ks[6], (DEPTH, D_MV)),
        "p_a": nrm(ks[7], (DEPTH, D_CONV, D)) * D_CONV ** -0.5,
        "p_b": nrm(ks[8], (DEPTH, D_MV, D)) * D_MV ** -0.5,
        "w_o": nrm(ks[9], (DEPTH, D, D)) * (D ** -0.5) * BETA,
        "ln1_g": 1.0 + 0.1 * nrm(ks[10], (DEPTH, D)),
        "ln1_b": 0.02 * nrm(ks[11], (DEPTH, D)),
        "peer_wq": nrm(ks[12], (DEPTH, D, PEER_HEADS * D_KEY)) * D ** -0.5,
        "peer_keys": nrm(ks[13], (DEPTH, PEER_HEADS, 2, N_KEYS, HALF_KEY)) * HALF_KEY ** -0.5,
        "peer_u": nrm(ks[14], (DEPTH, N_EXPERTS, D)) * D ** -0.5,
        "peer_v": nrm(ks[15], (DEPTH, N_EXPERTS, D)) * BETA,
        "ln2_g": 1.0 + 0.1 * nrm(ks[16], (DEPTH, D)),
        "ln2_b": 0.02 * nrm(ks[17], (DEPTH, D)),
    }


def reference(x, meta_tokens, ln0_g, ln0_b, w_in, b_if, conv_w, mh_norm_w, p_a, p_b, w_o,
              ln1_g, ln1_b, peer_wq, peer_keys, peer_u, peer_v, ln2_g, ln2_b):
    Bsz = x.shape[0]
    meta = jnp.broadcast_to(meta_tokens[None].astype(x.dtype), (Bsz, N_META, D_MODEL))
    h = layer_norm(jnp.concatenate([meta, x], axis=1), ln0_g, ln0_b)
    for l in range(DEPTH):
        mix = token_mixer(h, w_in[l], b_if[l], conv_w[l], mh_norm_w[l], p_a[l], p_b[l], w_o[l])
        h = layer_norm(ALPHA * h + mix, ln1_g[l], ln1_b[l])
        ffn = peer_ffn(h, peer_wq[l], peer_keys[l], peer_u[l], peer_v[l])
        h = layer_norm(ALPHA * h + ffn, ln2_g[l], ln2_b[l])
    return h[:, N_META:]
```

```python
import functools
import math

import jax
import jax.numpy as jnp
from jax import lax
from jax.experimental import pallas as pl
from jax.experimental.pallas import tpu as pltpu

TOPK = 16
LN_EPS = 1e-5
I_GATE_PAD = -1e30
MLSTM_CHUNK = 64
LANES = 128
VMEM_LIMIT_BYTES = 56 * 1024 * 1024


def _layer_norm(x, g, b):
    mu = x.mean(-1, keepdims=True)
    var = jnp.square(x - mu).mean(-1, keepdims=True)
    return (x - mu) * lax.rsqrt(var + LN_EPS) * g + b


def _gelu_exact(x):
    return 0.5 * x * (1.0 + lax.erf(x * (2.0 ** -0.5)))


def _mm_kernel(a_ref, b_ref, o_ref):
    o_ref[...] = jnp.dot(a_ref[...], b_ref[...],
                         preferred_element_type=jnp.float32).astype(o_ref.dtype)


def _matmul(a, b, out_dtype, tm, tn):
    m, k = a.shape
    _, n = b.shape
    tm = min(tm, m)
    tn = min(tn, n)
    assert m % tm == 0 and n % tn == 0
    return pl.pallas_call(
        _mm_kernel,
        grid=(m // tm, n // tn),
        in_specs=[pl.BlockSpec((tm, k), lambda i, j: (i, 0)),
                  pl.BlockSpec((k, tn), lambda i, j: (0, j))],
        out_specs=pl.BlockSpec((tm, tn), lambda i, j: (i, j)),
        out_shape=jax.ShapeDtypeStruct((m, n), out_dtype),
        compiler_params=pltpu.CompilerParams(
            dimension_semantics=("parallel", "parallel"),
            vmem_limit_bytes=VMEM_LIMIT_BYTES),
        name="matmul",
    )(a, b)


def _peer_dense_kernel(ht_ref, s0_ref, s1_ref, e0_ref, e1_ref, tau_ref, u_ref, vt_ref,
                       o_ref, p_ref, *, n_heads, rows_per_step):
    k = pl.program_id(1)

    @pl.when(k == 0)
    def _():
        o_ref[...] = jnp.zeros_like(o_ref)

    a = jnp.dot(u_ref[...], ht_ref[...], preferred_element_type=jnp.float32)
    act = _gelu_exact(a)
    for ii in range(rows_per_step):
        i = k * rows_per_step + ii
        w = jnp.zeros((LANES, ht_ref.shape[1]), jnp.float32)
        for h in range(n_heads):
            s0row = s0_ref[h, pl.ds(i, 1), :]
            e0row = e0_ref[h, pl.ds(i, 1), :]
            sel = (s1_ref[h] + s0row) >= tau_ref[h:h + 1, :]
            w = w + jnp.where(sel, e1_ref[h] * e0row, 0.0)
        p_ref[ii * LANES:(ii + 1) * LANES, :] = (
            w * act[ii * LANES:(ii + 1) * LANES, :]).astype(p_ref.dtype)
    o_ref[...] += jnp.dot(vt_ref[...], p_ref[...], preferred_element_type=jnp.float32)


def _peer_dense(ht, s0, s1, e0, e1, tau, u, vt, tm, rows_per_step):
    d, t = ht.shape
    n_heads, n_keys, _ = s0.shape
    assert n_keys == LANES
    n_experts = u.shape[0]
    eb = rows_per_step * LANES
    sel_spec = pl.BlockSpec((n_heads, n_keys, tm), lambda i, k: (0, 0, i))
    return pl.pallas_call(
        functools.partial(_peer_dense_kernel, n_heads=n_heads, rows_per_step=rows_per_step),
        grid=(t // tm, n_experts // eb),
        in_specs=[pl.BlockSpec((d, tm), lambda i, k: (0, i)),
                  sel_spec, sel_spec, sel_spec, sel_spec,
                  pl.BlockSpec((n_heads, tm), lambda i, k: (0, i)),
                  pl.BlockSpec((eb, d), lambda i, k: (k, 0)),
                  pl.BlockSpec((d, eb), lambda i, k: (0, k))],
        out_specs=pl.BlockSpec((d, tm), lambda i, k: (0, i)),
        out_shape=jax.ShapeDtypeStruct((d, t), jnp.float32),
        scratch_shapes=[pltpu.VMEM((eb, tm), jnp.bfloat16)],
        compiler_params=pltpu.CompilerParams(
            dimension_semantics=("parallel", "arbitrary"),
            vmem_limit_bytes=VMEM_LIMIT_BYTES),
        name="peer_dense",
    )(ht, s0, s1, e0, e1, tau, u, vt)


def _short_conv(b_gate, c_gate, xc, conv_w):
    z = c_gate * xc
    length = z.shape[1]
    kw = conv_w.shape[0]
    zp = jnp.pad(z, ((0, 0), (kw - 1, 0), (0, 0)))
    y = sum(conv_w[k] * zp[:, k:k + length] for k in range(kw))
    return b_gate * y


def _mlstm(q, k, v, i_pre, f_pre):
    bsz, length = q.shape[:2]
    head_qk = q.shape[-1]
    head_v = v.shape[-1]
    chunk = MLSTM_CHUNK
    pad = (-length) % chunk

    def lead_pad(a, value=0.0):
        return jnp.pad(a, ((0, 0), (pad, 0)) + ((0, 0),) * (a.ndim - 2), constant_values=value)

    q = lead_pad(q * (head_qk ** -0.5))
    k = lead_pad(k)
    v = lead_pad(v)
    log_i = lead_pad(i_pre, I_GATE_PAD)
    log_f = lead_pad(jax.nn.log_sigmoid(f_pre))
    lp = length + pad
    nc = lp // chunk

    def to_chunks(a):
        a = a.reshape((bsz, nc, chunk) + a.shape[2:])
        return jnp.moveaxis(a, (1, 3), (0, 2))

    xs = (to_chunks(q), to_chunks(k), to_chunks(v), to_chunks(log_i), to_chunks(log_f))
    causal = jnp.tril(jnp.ones((chunk, chunk), dtype=bool))

    def step(carry, xs_c):
        c_st, n_st, m_st = carry
        qc, kc, vc, li, lf = xs_c
        b = jnp.cumsum(lf, axis=-1)
        d = b[..., :, None] - b[..., None, :] + li[..., None, :]
        d = jnp.where(causal, d, -jnp.inf)
        inter = b + m_st[..., None]
        m_t = jnp.maximum(inter, d.max(-1))
        w = jnp.exp(d - m_t[..., None])
        s = jnp.einsum('bhtd,bhsd->bhts', qc, kc) * w
        sc_inter = jnp.exp(inter - m_t)
        num = (jnp.einsum('bhts,bhsv->bhtv', s, vc)
               + sc_inter[..., None] * jnp.einsum('bhtd,bhdv->bhtv', qc, c_st))
        den = s.sum(-1) + sc_inter * jnp.einsum('bhtd,bhd->bht', qc, n_st)
        h = num / jnp.maximum(jnp.abs(den), jnp.exp(-m_t))[..., None]
        b_last = b[..., -1]
        dl = b_last[..., None] - b + li
        m_new = jnp.maximum(b_last + m_st, dl.max(-1))
        wl = jnp.exp(dl - m_new[..., None])
        decay = jnp.exp(b_last + m_st - m_new)
        c_new = decay[..., None, None] * c_st + jnp.einsum('bhs,bhsd,bhsv->bhdv', wl, kc, vc)
        n_new = decay[..., None] * n_st + jnp.einsum('bhs,bhsd->bhd', wl, kc)
        return (c_new, n_new, m_new), h

    n_heads = q.shape[2]
    init = (jnp.zeros((bsz, n_heads, head_qk, head_v), jnp.float32),
            jnp.zeros((bsz, n_heads, head_qk), jnp.float32),
            jnp.zeros((bsz, n_heads), jnp.float32))
    _, hs = lax.scan(step, init, xs)
    hs = jnp.transpose(hs, (1, 0, 3, 2, 4)).reshape(bsz, lp, n_heads, head_v)
    return hs[:, pad:]


def kernel(x, meta_tokens, ln0_g, ln0_b, w_in, b_if, conv_w, mh_norm_w, p_a, p_b, w_o,
           ln1_g, ln1_b, peer_wq, peer_keys, peer_u, peer_v, ln2_g, ln2_b):
    depth = w_in.shape[0]
    assert depth == 1
    bsz, seq, d = x.shape
    n_meta = meta_tokens.shape[0]
    n_in = w_in.shape[-1]
    d_conv = conv_w.shape[-1]
    d_mv = mh_norm_w.shape[-1]
    n_mh = b_if.shape[-1] // 2
    d_mqk = (n_in - 3 * d_conv - 2 * d_mv - 2 * n_mh - 2 * d) // 2
    head_qk = d_mqk // n_mh
    head_v = d_mv // n_mh
    alpha = (2 * depth) ** 0.25
    bf = jnp.bfloat16
    t_x = bsz * seq

    h0_x = _layer_norm(x.reshape(t_x, d), ln0_g, ln0_b)
    h0_m = _layer_norm(meta_tokens, ln0_g, ln0_b)
    gate_off = 3 * d_conv + 2 * d_mqk + 2 * d_mv
    w = w_in[0]
    w_main = jnp.concatenate([w[:, :gate_off], w[:, gate_off + 2 * n_mh:]], axis=1).astype(bf)
    w_gif = jnp.pad(w[:, gate_off:gate_off + 2 * n_mh], ((0, 0), (0, LANES - 2 * n_mh))).astype(bf)
    proj_x = _matmul(h0_x.astype(bf), w_main, bf, 1024, 512)
    proj_m = _matmul(h0_m.astype(bf), w_main, bf, 1024, 512)
    gif_x = _matmul(h0_x.astype(bf), w_gif, jnp.float32, 1024, LANES)[:, :2 * n_mh]
    gif_m = _matmul(h0_m.astype(bf), w_gif, jnp.float32, 1024, LANES)[:, :2 * n_mh]

    proj = jnp.concatenate(
        [jnp.broadcast_to(proj_m[None], (bsz, n_meta, proj_m.shape[-1])),
         proj_x.reshape(bsz, seq, -1)], axis=1).astype(jnp.float32)
    gif = jnp.concatenate(
        [jnp.broadcast_to(gif_m[None], (bsz, n_meta, 2 * n_mh)),
         gif_x.reshape(bsz, seq, -1)], axis=1) + b_if[0]
    sizes = [d_conv, d_conv, d_conv, d_mqk, d_mqk, d_mv, d_mv, d, d]
    offs = [0]
    for s in sizes:
        offs.append(offs[-1] + s)
    cb, cc, cx, q, k, v, o, ga, gb = [proj[..., offs[i]:offs[i + 1]] for i in range(9)]
    length = seq + n_meta

    y_a = _short_conv(cb, cc, cx, conv_w[0])
    hm = _mlstm(q.reshape(bsz, length, n_mh, head_qk), k.reshape(bsz, length, n_mh, head_qk),
                v.reshape(bsz, length, n_mh, head_v), gif[..., :n_mh], gif[..., n_mh:])
    hm = hm * lax.rsqrt(jnp.mean(jnp.square(hm), -1, keepdims=True) + LN_EPS)
    hm = hm.reshape(bsz, length, d_mv) * mh_norm_w[0]
    y_b = jax.nn.sigmoid(o) * hm

    y_a = y_a[:, n_meta:].reshape(t_x, d_conv).astype(bf)
    y_b = y_b[:, n_meta:].reshape(t_x, d_mv).astype(bf)
    ga = ga[:, n_meta:].reshape(t_x, d)
    gb = gb[:, n_meta:].reshape(t_x, d)
    za = _matmul(y_a, p_a[0].astype(bf), jnp.float32, 1024, 512)
    zb = _matmul(y_b, p_b[0].astype(bf), jnp.float32, 1024, 512)
    merged = jax.nn.sigmoid(ga) * za + jax.nn.sigmoid(gb) * zb
    mix = _matmul(merged.astype(bf), w_o[0].astype(bf), jnp.float32, 1024, 512)
    h1 = _layer_norm(alpha * h0_x + mix, ln1_g[0], ln1_b[0])

    n_ph, _, n_keys, half_key = peer_keys.shape[1:]
    qp = _matmul(h1.astype(bf), peer_wq[0].astype(bf), jnp.float32, 1024, 512)
    qp = qp.reshape(t_x, n_ph, 2, half_key)
    sc = jnp.einsum('thpk,hpnk->thpn', qp, peer_keys[0]).astype(jnp.float32)
    s_top, _ = lax.top_k(sc, TOPK)
    cand = s_top[..., 0, :, None] + s_top[..., 1, None, :]
    best_s, _ = lax.top_k(cand.reshape(t_x, n_ph, TOPK * TOPK), TOPK)
    tau = best_s[..., TOPK - 1]
    zsum = jnp.sum(jnp.exp(best_s - best_s[..., :1]), axis=-1)
    e0 = jnp.exp(sc[:, :, 0, :] - s_top[..., 0, :1]) / zsum[..., None]
    e1 = jnp.exp(sc[:, :, 1, :] - s_top[..., 1, :1])
    to_t = lambda a: jnp.transpose(a, (1, 2, 0))
    ffn_t = _peer_dense(h1.T.astype(bf), to_t(sc[:, :, 0, :]), to_t(sc[:, :, 1, :]),
                        to_t(e0), to_t(e1), tau.T,
                        peer_u[0].astype(bf), peer_v[0].T.astype(bf), 512, 4)
    out = _layer_norm(alpha * h1 + ffn_t.T, ln2_g[0], ln2_b[0])
    return out.reshape(bsz, seq, d)
```

```python
import functools
import math

import jax
import jax.numpy as jnp
from jax import lax
from jax.experimental import pallas as pl
from jax.experimental.pallas import tpu as pltpu

TOPK = 16
LN_EPS = 1e-5
I_GATE_PAD = -1e30
MLSTM_CHUNK = 64
LANES = 128
BF16_SUBLANES = 16
VMEM_LIMIT_BYTES = 56 * 1024 * 1024


def _layer_norm(x, g, b):
    mu = x.mean(-1, keepdims=True)
    var = jnp.square(x - mu).mean(-1, keepdims=True)
    return (x - mu) * lax.rsqrt(var + LN_EPS) * g + b


def _gelu_exact(x):
    return 0.5 * x * (1.0 + lax.erf(x * (2.0 ** -0.5)))


def _mm_kernel(a_ref, b_ref, o_ref):
    o_ref[...] = jnp.dot(a_ref[...], b_ref[...],
                         preferred_element_type=jnp.float32).astype(o_ref.dtype)


def _matmul(a, b, out_dtype, tm, tn):
    m, k = a.shape
    _, n = b.shape
    tm = min(tm, m)
    tn = min(tn, n)
    assert m % tm == 0 and n % tn == 0
    return pl.pallas_call(
        _mm_kernel,
        grid=(m // tm, n // tn),
        in_specs=[pl.BlockSpec((tm, k), lambda i, j: (i, 0)),
                  pl.BlockSpec((k, tn), lambda i, j: (0, j))],
        out_specs=pl.BlockSpec((tm, tn), lambda i, j: (i, j)),
        out_shape=jax.ShapeDtypeStruct((m, n), out_dtype),
        compiler_params=pltpu.CompilerParams(
            dimension_semantics=("parallel", "parallel"),
            vmem_limit_bytes=VMEM_LIMIT_BYTES),
        name="matmul",
    )(a, b)


def _peer_dense_kernel(ht_ref, e0_ref, thr_ref, e1_ref, rank_ref, u_ref, vt_ref,
                       o_ref, p_ref, *, n_heads, rows_per_step):
    k = pl.program_id(1)
    bf = jnp.bfloat16

    @pl.when(k == 0)
    def _():
        o_ref[...] = jnp.zeros_like(o_ref)

    a = jnp.dot(u_ref[...], ht_ref[...], preferred_element_type=jnp.float32)
    act = _gelu_exact(a).astype(bf)
    zero = jnp.zeros((), bf)
    tm = ht_ref.shape[1]

    def row_tile(ref, h, i):
        row = jnp.broadcast_to(ref[h, pl.ds(i, 1), :], (BF16_SUBLANES, tm)).astype(bf)
        return pltpu.repeat(row, LANES // BF16_SUBLANES, axis=0)

    for ii in range(rows_per_step):
        i = k * rows_per_step + ii
        w = jnp.zeros((LANES, tm), bf)
        for h in range(n_heads):
            thr_row = row_tile(thr_ref, h, i)
            e0_row = row_tile(e0_ref, h, i)
            w = w + jnp.where(rank_ref[h] < thr_row, e1_ref[h], zero) * e0_row
        p_ref[ii * LANES:(ii + 1) * LANES, :] = w * act[ii * LANES:(ii + 1) * LANES, :]
    o_ref[...] += jnp.dot(vt_ref[...], p_ref[...], preferred_element_type=jnp.float32)


def _peer_dense(ht, e0, thr, e1, rank, u, vt, tm, rows_per_step):
    d, t = ht.shape
    n_heads, n_keys, _ = e0.shape
    assert n_keys == LANES
    n_experts = u.shape[0]
    eb = rows_per_step * LANES
    sel_spec = pl.BlockSpec((n_heads, n_keys, tm), lambda i, k: (0, 0, i))
    return pl.pallas_call(
        functools.partial(_peer_dense_kernel, n_heads=n_heads, rows_per_step=rows_per_step),
        grid=(t // tm, n_experts // eb),
        in_specs=[pl.BlockSpec((d, tm), lambda i, k: (0, i)),
                  sel_spec, sel_spec, sel_spec, sel_spec,
                  pl.BlockSpec((eb, d), lambda i, k: (k, 0)),
                  pl.BlockSpec((d, eb), lambda i, k: (0, k))],
        out_specs=pl.BlockSpec((d, tm), lambda i, k: (0, i)),
        out_shape=jax.ShapeDtypeStruct((d, t), jnp.float32),
        scratch_shapes=[pltpu.VMEM((eb, tm), jnp.bfloat16)],
        compiler_params=pltpu.CompilerParams(
            dimension_semantics=("parallel", "arbitrary"),
            vmem_limit_bytes=VMEM_LIMIT_BYTES),
        name="peer_dense",
    )(ht, e0, thr, e1, rank, u, vt)


def _fold_keys_kernel(keys_ref, wq_ref, o_ref):
    o_ref[...] = lax.dot_general(
        keys_ref[0], wq_ref[...], (((1,), (1,)), ((), ())),
        precision=lax.Precision.HIGHEST, preferred_element_type=jnp.float32)


def _fold_keys(keys, wq):
    n_blocks, n_keys, half_key = keys.shape
    d = wq.shape[0]
    return pl.pallas_call(
        _fold_keys_kernel,
        grid=(n_blocks,),
        in_specs=[pl.BlockSpec((1, n_keys, half_key), lambda b: (b, 0, 0)),
                  pl.BlockSpec((d, half_key), lambda b: (0, b))],
        out_specs=pl.BlockSpec((n_keys, d), lambda b: (b, 0)),
        out_shape=jax.ShapeDtypeStruct((n_blocks * n_keys, d), jnp.float32),
        compiler_params=pltpu.CompilerParams(dimension_semantics=("parallel",)),
        name="fold_keys",
    )(keys, wq)


def _extract_top(work, n, row_ref, rank=None):
    for r in range(n):
        m = jnp.max(work, axis=0, keepdims=True)
        row_ref[r:r + 1, :] = m
        hit = work == m
        if rank is not None:
            rank = jnp.where(hit, float(r), rank)
        work = jnp.where(hit, -jnp.inf, work)
    return rank


def _peer_select_kernel(sc_ref, e0_ref, thr_ref, e1_ref, rank_ref, a_ref, b_ref, c_ref,
                        *, n_heads, n_keys):
    tl = sc_ref.shape[1]
    half = TOPK // 2

    def head_body(h, carry):
        r0 = pl.multiple_of(h * 2 * n_keys, 2 * n_keys)
        s0 = sc_ref[pl.ds(r0, n_keys), :]
        s1 = sc_ref[pl.ds(r0 + n_keys, n_keys), :]
        _extract_top(s0, TOPK, a_ref)
        rank1 = _extract_top(s1, TOPK, b_ref, jnp.full((n_keys, tl), 100.0, jnp.float32))
        pieces = [a_ref[0:1, :] + b_ref[...]]
        pieces += [a_ref[p:p + 1, :] + b_ref[0:half, :] for p in range(1, half)]
        pieces += [a_ref[half:TOPK, :] + b_ref[0:1, :]]
        _extract_top(jnp.concatenate(pieces, axis=0), TOPK, c_ref)
        best = c_ref[...]
        tau = best[TOPK - 1:TOPK, :]
        zsum = jnp.sum(jnp.exp(best - best[0:1, :]), axis=0, keepdims=True)
        b_all = b_ref[...]
        thr = jnp.zeros((n_keys, tl), jnp.float32)
        for p in range(TOPK):
            a_p = a_ref[p:p + 1, :]
            n_sel = jnp.sum(jnp.where(a_p + b_all >= tau, 1.0, 0.0), axis=0, keepdims=True)
            thr = jnp.where(s0 == a_p, n_sel, thr)
        e0_ref[h] = jnp.exp(s0 - a_ref[0:1, :]) / zsum
        thr_ref[h] = thr
        e1_ref[h] = jnp.exp(s1 - b_ref[0:1, :]).astype(e1_ref.dtype)
        rank_ref[h] = rank1.astype(rank_ref.dtype)
        return carry

    lax.fori_loop(0, n_heads, head_body, 0)


def _peer_select(scores_t, n_heads, n_keys, tl):
    rows, t = scores_t.shape
    assert rows == n_heads * 2 * n_keys
    out_spec = pl.BlockSpec((n_heads, n_keys, tl), lambda i: (0, 0, i))
    f32 = jax.ShapeDtypeStruct((n_heads, n_keys, t), jnp.float32)
    b16 = jax.ShapeDtypeStruct((n_heads, n_keys, t), jnp.bfloat16)
    return pl.pallas_call(
        functools.partial(_peer_select_kernel, n_heads=n_heads, n_keys=n_keys),
        grid=(t // tl,),
        in_specs=[pl.BlockSpec((rows, tl), lambda i: (0, i))],
        out_specs=[out_spec, out_spec, out_spec, out_spec],
        out_shape=[f32, f32, b16, b16],
        scratch_shapes=[pltpu.VMEM((TOPK, tl), jnp.float32)] * 3,
        compiler_params=pltpu.CompilerParams(
            dimension_semantics=("parallel",), vmem_limit_bytes=VMEM_LIMIT_BYTES),
        name="peer_select",
    )(scores_t)


def _short_conv(b_gate, c_gate, xc, conv_w):
    z = c_gate * xc
    length = z.shape[1]
    kw = conv_w.shape[0]
    zp = jnp.pad(z, ((0, 0), (kw - 1, 0), (0, 0)))
    y = sum(conv_w[k] * zp[:, k:k + length] for k in range(kw))
    return b_gate * y


def _mlstm(q, k, v, i_pre, f_pre):
    bsz, length = q.shape[:2]
    head_qk = q.shape[-1]
    head_v = v.shape[-1]
    chunk = MLSTM_CHUNK
    pad = (-length) % chunk

    def lead_pad(a, value=0.0):
        return jnp.pad(a, ((0, 0), (pad, 0)) + ((0, 0),) * (a.ndim - 2), constant_values=value)

    q = lead_pad(q * (head_qk ** -0.5))
    k = lead_pad(k)
    v = lead_pad(v)
    log_i = lead_pad(i_pre, I_GATE_PAD)
    log_f = lead_pad(jax.nn.log_sigmoid(f_pre))
    lp = length + pad
    nc = lp // chunk

    def to_chunks(a):
        a = a.reshape((bsz, nc, chunk) + a.shape[2:])
        return jnp.moveaxis(a, (1, 3), (0, 2))

    xs = (to_chunks(q), to_chunks(k), to_chunks(v), to_chunks(log_i), to_chunks(log_f))
    causal = jnp.tril(jnp.ones((chunk, chunk), dtype=bool))

    def step(carry, xs_c):
        c_st, n_st, m_st = carry
        qc, kc, vc, li, lf = xs_c
        b = jnp.cumsum(lf, axis=-1)
        d = b[..., :, None] - b[..., None, :] + li[..., None, :]
        d = jnp.where(causal, d, -jnp.inf)
        inter = b + m_st[..., None]
        m_t = jnp.maximum(inter, d.max(-1))
        w = jnp.exp(d - m_t[..., None])
        s = jnp.einsum('bhtd,bhsd->bhts', qc, kc) * w
        sc_inter = jnp.exp(inter - m_t)
        num = (jnp.einsum('bhts,bhsv->bhtv', s, vc)
               + sc_inter[..., None] * jnp.einsum('bhtd,bhdv->bhtv', qc, c_st))
        den = s.sum(-1) + sc_inter * jnp.einsum('bhtd,bhd->bht', qc, n_st)
        h = num / jnp.maximum(jnp.abs(den), jnp.exp(-m_t))[..., None]
        b_last = b[..., -1]
        dl = b_last[..., None] - b + li
        m_new = jnp.maximum(b_last + m_st, dl.max(-1))
        wl = jnp.exp(dl - m_new[..., None])
        decay = jnp.exp(b_last + m_st - m_new)
        c_new = decay[..., None, None] * c_st + jnp.einsum('bhs,bhsd,bhsv->bhdv', wl, kc, vc)
        n_new = decay[..., None] * n_st + jnp.einsum('bhs,bhsd->bhd', wl, kc)
        return (c_new, n_new, m_new), h

    n_heads = q.shape[2]
    init = (jnp.zeros((bsz, n_heads, head_qk, head_v), jnp.float32),
            jnp.zeros((bsz, n_heads, head_qk), jnp.float32),
            jnp.zeros((bsz, n_heads), jnp.float32))
    _, hs = lax.scan(step, init, xs)
    hs = jnp.transpose(hs, (1, 0, 3, 2, 4)).reshape(bsz, lp, n_heads, head_v)
    return hs[:, pad:]


def kernel(x, meta_tokens, ln0_g, ln0_b, w_in, b_if, conv_w, mh_norm_w, p_a, p_b, w_o,
           ln1_g, ln1_b, peer_wq, peer_keys, peer_u, peer_v, ln2_g, ln2_b):
    depth = w_in.shape[0]
    assert depth == 1
    bsz, seq, d = x.shape
    n_meta = meta_tokens.shape[0]
    n_in = w_in.shape[-1]
    d_conv = conv_w.shape[-1]
    d_mv = mh_norm_w.shape[-1]
    n_mh = b_if.shape[-1] // 2
    d_mqk = (n_in - 3 * d_conv - 2 * d_mv - 2 * n_mh - 2 * d) // 2
    head_qk = d_mqk // n_mh
    head_v = d_mv // n_mh
    alpha = (2 * depth) ** 0.25
    bf = jnp.bfloat16
    t_x = bsz * seq

    h0_x = _layer_norm(x.reshape(t_x, d), ln0_g, ln0_b)
    h0_m = _layer_norm(meta_tokens, ln0_g, ln0_b)
    gate_off = 3 * d_conv + 2 * d_mqk + 2 * d_mv
    w = w_in[0]
    w_main = jnp.concatenate([w[:, :gate_off], w[:, gate_off + 2 * n_mh:]], axis=1).astype(bf)
    w_gif = jnp.pad(w[:, gate_off:gate_off + 2 * n_mh], ((0, 0), (0, LANES - 2 * n_mh))).astype(bf)
    proj_x = _matmul(h0_x.astype(bf), w_main, bf, 1024, 512)
    proj_m = _matmul(h0_m.astype(bf), w_main, bf, 1024, 512)
    gif_x = _matmul(h0_x.astype(bf), w_gif, jnp.float32, 1024, LANES)[:, :2 * n_mh]
    gif_m = _matmul(h0_m.astype(bf), w_gif, jnp.float32, 1024, LANES)[:, :2 * n_mh]

    proj = jnp.concatenate(
        [jnp.broadcast_to(proj_m[None], (bsz, n_meta, proj_m.shape[-1])),
         proj_x.reshape(bsz, seq, -1)], axis=1).astype(jnp.float32)
    gif = jnp.concatenate(
        [jnp.broadcast_to(gif_m[None], (bsz, n_meta, 2 * n_mh)),
         gif_x.reshape(bsz, seq, -1)], axis=1) + b_if[0]
    sizes = [d_conv, d_conv, d_conv, d_mqk, d_mqk, d_mv, d_mv, d, d]
    offs = [0]
    for s in sizes:
        offs.append(offs[-1] + s)
    cb, cc, cx, q, k, v, o, ga, gb = [proj[..., offs[i]:offs[i + 1]] for i in range(9)]
    length = seq + n_meta

    y_a = _short_conv(cb, cc, cx, conv_w[0])
    hm = _mlstm(q.reshape(bsz, length, n_mh, head_qk), k.reshape(bsz, length, n_mh, head_qk),
                v.reshape(bsz, length, n_mh, head_v), gif[..., :n_mh], gif[..., n_mh:])
    hm = hm * lax.rsqrt(jnp.mean(jnp.square(hm), -1, keepdims=True) + LN_EPS)
    hm = hm.reshape(bsz, length, d_mv) * mh_norm_w[0]
    y_b = jax.nn.sigmoid(o) * hm

    y_a = y_a[:, n_meta:].reshape(t_x, d_conv).astype(bf)
    y_b = y_b[:, n_meta:].reshape(t_x, d_mv).astype(bf)
    ga = ga[:, n_meta:].reshape(t_x, d)
    gb = gb[:, n_meta:].reshape(t_x, d)
    za = _matmul(y_a, p_a[0].astype(bf), jnp.float32, 1024, 512)
    zb = _matmul(y_b, p_b[0].astype(bf), jnp.float32, 1024, 512)
    merged = jax.nn.sigmoid(ga) * za + jax.nn.sigmoid(gb) * zb
    mix = _matmul(merged.astype(bf), w_o[0].astype(bf), jnp.float32, 1024, 512)
    h1 = _layer_norm(alpha * h0_x + mix, ln1_g[0], ln1_b[0])

    n_ph, _, n_keys, half_key = peer_keys.shape[1:]
    h1_t = h1.T.astype(bf)
    w_fold = _fold_keys(peer_keys[0].reshape(n_ph * 2, n_keys, half_key), peer_wq[0])
    scores_t = _matmul(w_fold.astype(bf), h1_t, jnp.float32, 1024, 512)
    e0, thr, e1, rank = _peer_select(scores_t, n_ph, n_keys, LANES)
    ffn_t = _peer_dense(h1_t, e0, thr, e1, rank,
                        peer_u[0].astype(bf), peer_v[0].T.astype(bf), 512, 8)
    out = _layer_norm(alpha * h1 + ffn_t.T, ln2_g[0], ln2_b[0])
    return out.reshape(bsz, seq, d)
```

```python
import functools
import math

import jax
import jax.numpy as jnp
from jax import lax
from jax.experimental import pallas as pl
from jax.experimental.pallas import tpu as pltpu

TOPK = 16
LN_EPS = 1e-5
I_GATE_PAD = -1e30
MLSTM_CHUNK = 256
MIX_TILE = 256
LANES = 128
BF16_SUBLANES = 16
VMEM_LIMIT_BYTES = 56 * 1024 * 1024


def _layer_norm(x, g, b):
    mu = x.mean(-1, keepdims=True)
    var = jnp.square(x - mu).mean(-1, keepdims=True)
    return (x - mu) * lax.rsqrt(var + LN_EPS) * g + b


def _gelu_exact(x):
    return 0.5 * x * (1.0 + lax.erf(x * (2.0 ** -0.5)))


def _mm_kernel(a_ref, b_ref, o_ref):
    o_ref[...] = jnp.dot(a_ref[...], b_ref[...],
                         preferred_element_type=jnp.float32).astype(o_ref.dtype)


def _matmul(a, b, out_dtype, tm, tn):
    m, k = a.shape
    _, n = b.shape
    tm = min(tm, m)
    tn = min(tn, n)
    assert m % tm == 0 and n % tn == 0
    return pl.pallas_call(
        _mm_kernel,
        grid=(m // tm, n // tn),
        in_specs=[pl.BlockSpec((tm, k), lambda i, j: (i, 0)),
                  pl.BlockSpec((k, tn), lambda i, j: (0, j))],
        out_specs=pl.BlockSpec((tm, tn), lambda i, j: (i, j)),
        out_shape=jax.ShapeDtypeStruct((m, n), out_dtype),
        compiler_params=pltpu.CompilerParams(
            dimension_semantics=("parallel", "parallel"),
            vmem_limit_bytes=VMEM_LIMIT_BYTES),
        name="matmul",
    )(a, b)


def _peer_dense_kernel(ht_ref, e0_ref, thr_ref, e1_ref, rank_ref, u_ref, vt_ref,
                       o_ref, p_ref, *, n_heads, rows_per_step):
    k = pl.program_id(1)
    bf = jnp.bfloat16

    @pl.when(k == 0)
    def _():
        o_ref[...] = jnp.zeros_like(o_ref)

    a = jnp.dot(u_ref[...], ht_ref[...], preferred_element_type=jnp.float32)
    act = _gelu_exact(a).astype(bf)
    zero = jnp.zeros((), bf)
    tm = ht_ref.shape[1]

    def row_tile(ref, h, i):
        row = jnp.broadcast_to(ref[h, pl.ds(i, 1), :], (BF16_SUBLANES, tm)).astype(bf)
        return pltpu.repeat(row, LANES // BF16_SUBLANES, axis=0)

    for ii in range(rows_per_step):
        i = k * rows_per_step + ii
        w = jnp.zeros((LANES, tm), bf)
        for h in range(n_heads):
            thr_row = row_tile(thr_ref, h, i)
            e0_row = row_tile(e0_ref, h, i)
            w = w + jnp.where(rank_ref[h] < thr_row, e1_ref[h], zero) * e0_row
        p_ref[ii * LANES:(ii + 1) * LANES, :] = w * act[ii * LANES:(ii + 1) * LANES, :]
    o_ref[...] += jnp.dot(vt_ref[...], p_ref[...], preferred_element_type=jnp.float32)


def _peer_dense(ht, e0, thr, e1, rank, u, vt, tm, rows_per_step):
    d, t = ht.shape
    n_heads, n_keys, _ = e0.shape
    assert n_keys == LANES
    n_experts = u.shape[0]
    eb = rows_per_step * LANES
    sel_spec = pl.BlockSpec((n_heads, n_keys, tm), lambda i, k: (0, 0, i))
    return pl.pallas_call(
        functools.partial(_peer_dense_kernel, n_heads=n_heads, rows_per_step=rows_per_step),
        grid=(t // tm, n_experts // eb),
        in_specs=[pl.BlockSpec((d, tm), lambda i, k: (0, i)),
                  sel_spec, sel_spec, sel_spec, sel_spec,
                  pl.BlockSpec((eb, d), lambda i, k: (k, 0)),
                  pl.BlockSpec((d, eb), lambda i, k: (0, k))],
        out_specs=pl.BlockSpec((d, tm), lambda i, k: (0, i)),
        out_shape=jax.ShapeDtypeStruct((d, t), jnp.float32),
        scratch_shapes=[pltpu.VMEM((eb, tm), jnp.bfloat16)],
        compiler_params=pltpu.CompilerParams(
            dimension_semantics=("parallel", "arbitrary"),
            vmem_limit_bytes=VMEM_LIMIT_BYTES),
        name="peer_dense",
    )(ht, e0, thr, e1, rank, u, vt)


def _fold_keys_kernel(keys_ref, wq_ref, o_ref):
    o_ref[...] = lax.dot_general(
        keys_ref[0], wq_ref[...], (((1,), (1,)), ((), ())),
        precision=lax.Precision.HIGHEST, preferred_element_type=jnp.float32)


def _fold_keys(keys, wq):
    n_blocks, n_keys, half_key = keys.shape
    d = wq.shape[0]
    return pl.pallas_call(
        _fold_keys_kernel,
        grid=(n_blocks,),
        in_specs=[pl.BlockSpec((1, n_keys, half_key), lambda b: (b, 0, 0)),
                  pl.BlockSpec((d, half_key), lambda b: (0, b))],
        out_specs=pl.BlockSpec((n_keys, d), lambda b: (b, 0)),
        out_shape=jax.ShapeDtypeStruct((n_blocks * n_keys, d), jnp.float32),
        compiler_params=pltpu.CompilerParams(dimension_semantics=("parallel",)),
        name="fold_keys",
    )(keys, wq)


def _extract_top(work, n, row_ref, rank=None):
    for r in range(n):
        m = jnp.max(work, axis=0, keepdims=True)
        row_ref[r:r + 1, :] = m
        hit = work == m
        if rank is not None:
            rank = jnp.where(hit, float(r), rank)
        work = jnp.where(hit, -jnp.inf, work)
    return rank


def _peer_select_kernel(sc_ref, e0_ref, thr_ref, e1_ref, rank_ref, a_ref, b_ref, c_ref,
                        *, n_heads, n_keys):
    tl = sc_ref.shape[1]
    half = TOPK // 2

    def head_body(h, carry):
        r0 = pl.multiple_of(h * 2 * n_keys, 2 * n_keys)
        s0 = sc_ref[pl.ds(r0, n_keys), :]
        s1 = sc_ref[pl.ds(r0 + n_keys, n_keys), :]
        _extract_top(s0, TOPK, a_ref)
        rank1 = _extract_top(s1, TOPK, b_ref, jnp.full((n_keys, tl), 100.0, jnp.float32))
        pieces = [a_ref[0:1, :] + b_ref[...]]
        pieces += [a_ref[p:p + 1, :] + b_ref[0:half, :] for p in range(1, half)]
        pieces += [a_ref[half:TOPK, :] + b_ref[0:1, :]]
        _extract_top(jnp.concatenate(pieces, axis=0), TOPK, c_ref)
        best = c_ref[...]
        tau = best[TOPK - 1:TOPK, :]
        zsum = jnp.sum(jnp.exp(best - best[0:1, :]), axis=0, keepdims=True)
        b_all = b_ref[...]
        thr = jnp.zeros((n_keys, tl), jnp.float32)
        for p in range(TOPK):
            a_p = a_ref[p:p + 1, :]
            n_sel = jnp.sum(jnp.where(a_p + b_all >= tau, 1.0, 0.0), axis=0, keepdims=True)
            thr = jnp.where(s0 == a_p, n_sel, thr)
        e0_ref[h] = jnp.exp(s0 - a_ref[0:1, :]) / zsum
        thr_ref[h] = thr
        e1_ref[h] = jnp.exp(s1 - b_ref[0:1, :]).astype(e1_ref.dtype)
        rank_ref[h] = rank1.astype(rank_ref.dtype)
        return carry

    lax.fori_loop(0, n_heads, head_body, 0)


def _peer_select(scores_t, n_heads, n_keys, tl):
    rows, t = scores_t.shape
    assert rows == n_heads * 2 * n_keys
    out_spec = pl.BlockSpec((n_heads, n_keys, tl), lambda i: (0, 0, i))
    f32 = jax.ShapeDtypeStruct((n_heads, n_keys, t), jnp.float32)
    b16 = jax.ShapeDtypeStruct((n_heads, n_keys, t), jnp.bfloat16)
    return pl.pallas_call(
        functools.partial(_peer_select_kernel, n_heads=n_heads, n_keys=n_keys),
        grid=(t // tl,),
        in_specs=[pl.BlockSpec((rows, tl), lambda i: (0, i))],
        out_specs=[out_spec, out_spec, out_spec, out_spec],
        out_shape=[f32, f32, b16, b16],
        scratch_shapes=[pltpu.VMEM((TOPK, tl), jnp.float32)] * 3,
        compiler_params=pltpu.CompilerParams(
            dimension_semantics=("parallel",), vmem_limit_bytes=VMEM_LIMIT_BYTES),
        name="peer_select",
    )(scores_t)


def _log_sigmoid(x):
    return jnp.minimum(x, 0.0) - jnp.log1p(jnp.exp(-jnp.abs(x)))


def _mlstm_kernel(q_ref, k_ref, v_ref, o_ref, g_ref, gt_ref,
                  qm_ref, km_ref, vm_ref, om_ref, gm_ref, gmt_ref,
                  bias_ref, biast_ref, nw_ref, y_ref, c_ref, m_ref,
                  *, n_heads, head_qk, head_v):
    c = pl.program_id(1)
    tc = q_ref.shape[0]
    f32, bf = jnp.float32, jnp.bfloat16
    first = c == 0

    @pl.when(first)
    def _():
        c_ref[...] = jnp.zeros_like(c_ref)
        m_ref[...] = jnp.zeros_like(m_ref)

    q = jnp.where(first, qm_ref[...], q_ref[...])
    k = jnp.where(first, km_ref[...], k_ref[...])
    v = jnp.where(first, vm_ref[...], v_ref[...])
    o = jnp.where(first, om_ref[...], o_ref[...])
    g = jnp.where(first, gm_ref[...], g_ref[...]) + bias_ref[...]
    gt = jnp.where(first, gmt_ref[...], gt_ref[...]) + biast_ref[...]

    row = lax.broadcasted_iota(jnp.int32, (tc, tc), 0)
    col = lax.broadcasted_iota(jnp.int32, (tc, tc), 1)
    causal = col <= row
    tri = jnp.where(causal, 1.0, 0.0).astype(f32)
    b_cols = jnp.dot(tri, _log_sigmoid(g), precision=lax.Precision.HIGHEST,
                     preferred_element_type=f32)
    b_rows = lax.dot_general(_log_sigmoid(gt), tri, (((1,), (1,)), ((), ())),
                             precision=lax.Precision.HIGHEST, preferred_element_type=f32)
    ln_scale = -0.5 * math.log(head_qk)
    ones_blk = jnp.where(lax.broadcasted_iota(jnp.int32, (tc, LANES), 1) == 0, 1.0, 0.0).astype(bf)

    for h in range(n_heads):
        qh = q[:, h * head_qk:(h + 1) * head_qk]
        kh = k[:, h * head_qk:(h + 1) * head_qk]
        v_ext = jnp.concatenate([v[:, h * head_v:(h + 1) * head_v], ones_blk], axis=1)
        li_col = g[:, h:h + 1]
        b_col = b_cols[:, n_heads + h:n_heads + h + 1]
        li_row = gt[h:h + 1, :]
        b_row = b_rows[n_heads + h:n_heads + h + 1, :]
        m_prev = m_ref[h:h + 1, 0:1]
        d = jnp.where(causal, b_col + (li_row - b_row), -jnp.inf)
        inter = b_col + m_prev
        m_t = jnp.maximum(inter, jnp.max(d, axis=1, keepdims=True))
        w = jnp.exp(d - (m_t - ln_scale))
        s = lax.dot_general(qh, kh, (((1,), (1,)), ((), ())), preferred_element_type=f32) * w
        sc_inter = jnp.exp(inter - (m_t - ln_scale))
        c_ext = c_ref[h]
        num_ext = (jnp.dot(s.astype(bf), v_ext, preferred_element_type=f32)
                   + sc_inter * jnp.dot(qh, c_ext.astype(bf), preferred_element_type=f32))
        den = num_ext[:, head_v:head_v + 1]
        hm = num_ext[:, :head_v] / jnp.maximum(jnp.abs(den), jnp.exp(-m_t))
        hm = hm * lax.rsqrt(jnp.mean(hm * hm, axis=1, keepdims=True) + LN_EPS)
        gate_o = jax.nn.sigmoid(o[:, h * head_v:(h + 1) * head_v].astype(f32))
        y_ref[:, h * head_v:(h + 1) * head_v] = (
            gate_o * (hm * nw_ref[:, h * head_v:(h + 1) * head_v])).astype(y_ref.dtype)
        b_last = b_col[tc - 1:tc, :]
        dl = b_last - b_col + li_col
        m_new = jnp.maximum(b_last + m_prev, jnp.max(dl, axis=0, keepdims=True))
        wv = (jnp.exp(dl - m_new) * v_ext.astype(f32)).astype(bf)
        c_ref[h] = (jnp.exp(b_last + m_prev - m_new) * c_ext
                    + lax.dot_general(kh, wv, (((0,), (0,)), ((), ())),
                                      preferred_element_type=f32))
        m_ref[h:h + 1, :] = jnp.broadcast_to(m_new, (1, LANES))


def _mlstm(proj_x, proj_m, gif_x, gif_m, b_if, norm_w, *, bsz, seq, n_heads, head_qk, head_v,
           off_q, off_v):
    tc = min(MLSTM_CHUNK, seq)
    assert seq % tc == 0
    n_meta = proj_m.shape[0]
    d_mqk, d_mv = n_heads * head_qk, n_heads * head_v
    assert off_q % d_mqk == 0 and off_v % d_mv == 0 and n_meta <= tc
    nc_x = seq // tc
    bf = jnp.bfloat16
    lead = tc - n_meta
    pad_rows = lambda a: jnp.pad(a, ((lead, 0), (0, 0)))
    qm = pad_rows(proj_m[:, off_q:off_q + d_mqk])
    km = pad_rows(proj_m[:, off_q + d_mqk:off_q + 2 * d_mqk])
    vm = pad_rows(proj_m[:, off_v:off_v + d_mv])
    om = pad_rows(proj_m[:, off_v + d_mv:off_v + 2 * d_mv])
    pad_gate = jnp.concatenate([jnp.full((lead, n_heads), I_GATE_PAD, jnp.float32),
                                jnp.full((lead, n_heads), -I_GATE_PAD, jnp.float32),
                                jnp.zeros((lead, LANES - 2 * n_heads), jnp.float32)], axis=1)
    gm = jnp.concatenate([pad_gate, gif_m], axis=0)
    bias = jnp.pad(b_if, (0, LANES - 2 * n_heads))[None, :]
    biast = jnp.broadcast_to(b_if[:, None], (2 * n_heads, tc))
    gt_x = gif_x[:, :2 * n_heads].T
    gt_m = gm[:, :2 * n_heads].T

    xrow = lambda b, c: b * nc_x + jnp.maximum(c - 1, 0)
    const = lambda b, c: (0, 0)
    return pl.pallas_call(
        functools.partial(_mlstm_kernel, n_heads=n_heads, head_qk=head_qk, head_v=head_v),
        grid=(bsz, nc_x + 1),
        in_specs=[pl.BlockSpec((tc, d_mqk), lambda b, c: (xrow(b, c), off_q // d_mqk)),
                  pl.BlockSpec((tc, d_mqk), lambda b, c: (xrow(b, c), off_q // d_mqk + 1)),
                  pl.BlockSpec((tc, d_mv), lambda b, c: (xrow(b, c), off_v // d_mv)),
                  pl.BlockSpec((tc, d_mv), lambda b, c: (xrow(b, c), off_v // d_mv + 1)),
                  pl.BlockSpec((tc, LANES), lambda b, c: (xrow(b, c), 0)),
                  pl.BlockSpec((2 * n_heads, tc), lambda b, c: (0, xrow(b, c))),
                  pl.BlockSpec((tc, d_mqk), const), pl.BlockSpec((tc, d_mqk), const),
                  pl.BlockSpec((tc, d_mv), const), pl.BlockSpec((tc, d_mv), const),
                  pl.BlockSpec((tc, LANES), const), pl.BlockSpec((2 * n_heads, tc), const),
                  pl.BlockSpec((1, LANES), const), pl.BlockSpec((2 * n_heads, tc), const),
                  pl.BlockSpec((1, d_mv), const)],
        out_specs=pl.BlockSpec((tc, d_mv), lambda b, c: (xrow(b, c), 0)),
        out_shape=jax.ShapeDtypeStruct((bsz * seq, d_mv), bf),
        scratch_shapes=[pltpu.VMEM((n_heads, head_qk, head_v + LANES), jnp.float32),
                        pltpu.VMEM((n_heads, LANES), jnp.float32)],
        compiler_params=pltpu.CompilerParams(
            dimension_semantics=("parallel", "arbitrary"),
            vmem_limit_bytes=VMEM_LIMIT_BYTES),
        name="mlstm",
    )(proj_x, proj_x, proj_x, proj_x, gif_x, gt_x, qm, km, vm, om, gm, gt_m,
      bias, biast, norm_w[None, :])


def _ln_rows(x, g, b):
    mu = jnp.mean(x, axis=1, keepdims=True)
    xc = x - mu
    var = jnp.mean(xc * xc, axis=1, keepdims=True)
    return xc * lax.rsqrt(var + LN_EPS) * g + b


def _mix_kernel(x_ref, cb_ref, cc_ref, cx_ref, cch_ref, cxh_ref, ccm_ref, cxm_ref,
                yb_ref, ga_ref, gb_ref, pa_ref, pb_ref, wo_ref, cw_ref,
                g0_ref, b0_ref, g1_ref, b1_ref, h1_ref, h1t_ref,
                *, tiles_per_batch, alpha):
    i = pl.program_id(0)
    f32, bf = jnp.float32, jnp.bfloat16
    tm = x_ref.shape[0]
    z = cc_ref[...].astype(f32) * cx_ref[...].astype(f32)
    batch_start = (i % tiles_per_batch) == 0
    halo = jnp.where(batch_start,
                     ccm_ref[...].astype(f32) * cxm_ref[...].astype(f32),
                     cch_ref[...].astype(f32) * cxh_ref[...].astype(f32))
    sub = lax.broadcasted_iota(jnp.int32, halo.shape, 0)
    n_halo = halo.shape[0]
    y = cw_ref[2:3, :] * z
    for shift in (1, 2):
        rolled = pltpu.roll(z, shift, 0)
        head = jnp.where(sub < shift, pltpu.roll(halo, shift, 0), rolled[:n_halo])
        y = y + cw_ref[2 - shift:3 - shift, :] * jnp.concatenate([head, rolled[n_halo:]], axis=0)
    y_a = (cb_ref[...].astype(f32) * y).astype(bf)
    za = jnp.dot(y_a, pa_ref[...], preferred_element_type=f32)
    zb = jnp.dot(yb_ref[...], pb_ref[...], preferred_element_type=f32)
    merged = (jax.nn.sigmoid(ga_ref[...].astype(f32)) * za
              + jax.nn.sigmoid(gb_ref[...].astype(f32)) * zb).astype(bf)
    mix = jnp.dot(merged, wo_ref[...], preferred_element_type=f32)
    h0 = _ln_rows(x_ref[...], g0_ref[...], b0_ref[...])
    h1 = _ln_rows(alpha * h0 + mix, g1_ref[...], b1_ref[...])
    h1_ref[...] = h1
    h1t_ref[...] = h1.T.astype(bf)


def _mix(x2, proj_x, proj_m, y_b, p_a, p_b, w_o, conv_w, ln0_g, ln0_b, ln1_g, ln1_b,
         *, seq, off_g, alpha, tm):
    t_x, d = x2.shape
    d_conv = conv_w.shape[-1]
    d_mv = y_b.shape[-1]
    n_meta = proj_m.shape[0]
    halo = 8
    assert seq % tm == 0 and tm % halo == 0 and n_meta % halo == 0 and off_g % d == 0
    assert conv_w.shape[0] == 3
    tile = lambda j: pl.BlockSpec((tm, d_conv), lambda i: (i, j))
    prev = lambda j: pl.BlockSpec((halo, d_conv),
                                  lambda i: (jnp.maximum(i * (tm // halo) - 1, 0), j))
    meta = lambda j: pl.BlockSpec((halo, d_conv), lambda i: (n_meta // halo - 1, j))
    const = lambda shape: pl.BlockSpec(shape, lambda i: (0, 0), pipeline_mode=pl.Buffered(1))
    vec = lambda a: a[None, :]
    return pl.pallas_call(
        functools.partial(_mix_kernel, tiles_per_batch=seq // tm, alpha=alpha),
        grid=(t_x // tm,),
        in_specs=[pl.BlockSpec((tm, d), lambda i: (i, 0)),
                  tile(0), tile(1), tile(2), prev(1), prev(2), meta(1), meta(2),
                  pl.BlockSpec((tm, d_mv), lambda i: (i, 0)),
                  pl.BlockSpec((tm, d), lambda i: (i, off_g // d)),
                  pl.BlockSpec((tm, d), lambda i: (i, off_g // d + 1)),
                  const(p_a.shape), const(p_b.shape), const(w_o.shape), const(conv_w.shape),
                  const((1, d)), const((1, d)), const((1, d)), const((1, d))],
        out_specs=[pl.BlockSpec((tm, d), lambda i: (i, 0)),
                   pl.BlockSpec((d, tm), lambda i: (0, i))],
        out_shape=[jax.ShapeDtypeStruct((t_x, d), jnp.float32),
                   jax.ShapeDtypeStruct((d, t_x), jnp.bfloat16)],
        compiler_params=pltpu.CompilerParams(
            dimension_semantics=("parallel",), vmem_limit_bytes=VMEM_LIMIT_BYTES),
        name="mix",
    )(x2, proj_x, proj_x, proj_x, proj_x, proj_x, proj_m, proj_m, y_b, proj_x, proj_x,
      p_a, p_b, w_o, conv_w, vec(ln0_g), vec(ln0_b), vec(ln1_g), vec(ln1_b))


def kernel(x, meta_tokens, ln0_g, ln0_b, w_in, b_if, conv_w, mh_norm_w, p_a, p_b, w_o,
           ln1_g, ln1_b, peer_wq, peer_keys, peer_u, peer_v, ln2_g, ln2_b):
    depth = w_in.shape[0]
    assert depth == 1
    bsz, seq, d = x.shape
    n_meta = meta_tokens.shape[0]
    n_in = w_in.shape[-1]
    d_conv = conv_w.shape[-1]
    d_mv = mh_norm_w.shape[-1]
    n_mh = b_if.shape[-1] // 2
    d_mqk = (n_in - 3 * d_conv - 2 * d_mv - 2 * n_mh - 2 * d) // 2
    head_qk = d_mqk // n_mh
    head_v = d_mv // n_mh
    alpha = (2 * depth) ** 0.25
    bf = jnp.bfloat16
    t_x = bsz * seq

    h0_x = _layer_norm(x.reshape(t_x, d), ln0_g, ln0_b)
    h0_m = _layer_norm(meta_tokens, ln0_g, ln0_b)
    gate_off = 3 * d_conv + 2 * d_mqk + 2 * d_mv
    w = w_in[0]
    w_main = jnp.concatenate([w[:, :gate_off], w[:, gate_off + 2 * n_mh:]], axis=1).astype(bf)
    w_gif = jnp.pad(w[:, gate_off:gate_off + 2 * n_mh], ((0, 0), (0, LANES - 2 * n_mh))).astype(bf)
    proj_x = _matmul(h0_x.astype(bf), w_main, bf, 1024, 512)
    proj_m = _matmul(h0_m.astype(bf), w_main, bf, 1024, 512)
    gif_x = _matmul(h0_x.astype(bf), w_gif, jnp.float32, 1024, LANES)
    gif_m = _matmul(h0_m.astype(bf), w_gif, jnp.float32, 1024, LANES)

    off_q = 3 * d_conv
    off_v = off_q + 2 * d_mqk
    off_g = off_v + 2 * d_mv
    y_b = _mlstm(proj_x, proj_m, gif_x, gif_m, b_if[0], mh_norm_w[0], bsz=bsz, seq=seq,
                 n_heads=n_mh, head_qk=head_qk, head_v=head_v, off_q=off_q, off_v=off_v)
    h1, h1_t = _mix(x.reshape(t_x, d), proj_x, proj_m, y_b,
                    p_a[0].astype(bf), p_b[0].astype(bf), w_o[0].astype(bf), conv_w[0],
                    ln0_g, ln0_b, ln1_g[0], ln1_b[0], seq=seq, off_g=off_g, alpha=alpha,
                    tm=MIX_TILE)

    n_ph, _, n_keys, half_key = peer_keys.shape[1:]
    w_fold = _fold_keys(peer_keys[0].reshape(n_ph * 2, n_keys, half_key), peer_wq[0])
    scores_t = _matmul(w_fold.astype(bf), h1_t, jnp.float32, 1024, 512)
    e0, thr, e1, rank = _peer_select(scores_t, n_ph, n_keys, LANES)
    ffn_t = _peer_dense(h1_t, e0, thr, e1, rank,
                        peer_u[0].astype(bf), peer_v[0].T.astype(bf), 512, 8)
    out = _layer_norm(alpha * h1 + ffn_t.T, ln2_g[0], ln2_b[0])
    return out.reshape(bsz, seq, d)
```

```python
import functools
import math

import jax
import jax.numpy as jnp
from jax import lax
from jax.experimental import pallas as pl
from jax.experimental.pallas import tpu as pltpu

TOPK = 16
LN_EPS = 1e-5
I_GATE_PAD = -1e30
MLSTM_CHUNK = 256
MIX_TILE = 256
LANES = 128
BF16_SUBLANES = 16
VMEM_LIMIT_BYTES = 56 * 1024 * 1024
PEER_VMEM_LIMIT_BYTES = 58 * 1024 * 1024


def _gelu_exact(x):
    return 0.5 * x * (1.0 + lax.erf(x * (2.0 ** -0.5)))


def _ln_rows(x, g, b):
    mu = jnp.mean(x, axis=1, keepdims=True)
    xc = x - mu
    var = jnp.mean(xc * xc, axis=1, keepdims=True)
    return xc * lax.rsqrt(var + LN_EPS) * g + b


def _inproj_kernel(x_ref, g_ref, b_ref, w_ref, wg_ref, o_ref, og_ref, h_ref):
    @pl.when(pl.program_id(1) == 0)
    def _():
        h = _ln_rows(x_ref[...], g_ref[...], b_ref[...]).astype(h_ref.dtype)
        h_ref[...] = h
        og_ref[...] = jnp.dot(h, wg_ref[...], preferred_element_type=jnp.float32)

    o_ref[...] = jnp.dot(h_ref[...], w_ref[...],
                         preferred_element_type=jnp.float32).astype(o_ref.dtype)


def _inproj(x2, ln_g, ln_b, w_main, w_gate, tm, tn):
    m, d = x2.shape
    n = w_main.shape[1]
    tm = min(tm, m)
    assert m % tm == 0 and n % tn == 0 and w_gate.shape[1] == LANES
    const = lambda shape: pl.BlockSpec(shape, lambda i, j: (0, 0))
    return pl.pallas_call(
        _inproj_kernel,
        grid=(m // tm, n // tn),
        in_specs=[pl.BlockSpec((tm, d), lambda i, j: (i, 0)),
                  const((1, d)), const((1, d)),
                  pl.BlockSpec((d, tn), lambda i, j: (0, j)),
                  const((d, LANES))],
        out_specs=[pl.BlockSpec((tm, tn), lambda i, j: (i, j)),
                   pl.BlockSpec((tm, LANES), lambda i, j: (i, 0))],
        out_shape=[jax.ShapeDtypeStruct((m, n), jnp.bfloat16),
                   jax.ShapeDtypeStruct((m, LANES), jnp.float32)],
        scratch_shapes=[pltpu.VMEM((tm, d), jnp.bfloat16)],
        compiler_params=pltpu.CompilerParams(
            dimension_semantics=("parallel", "arbitrary"),
            vmem_limit_bytes=VMEM_LIMIT_BYTES),
        name="inproj",
    )(x2, ln_g[None, :], ln_b[None, :], w_main, w_gate)


def _mm_kernel(a_ref, b_ref, o_ref):
    bf = jnp.bfloat16
    o_ref[...] = jnp.dot(a_ref[...].astype(bf), b_ref[...].astype(bf),
                         preferred_element_type=jnp.float32).astype(o_ref.dtype)


def _matmul(a, b, out_dtype, tm, tn):
    m, k = a.shape
    _, n = b.shape
    tm = min(tm, m)
    tn = min(tn, n)
    assert m % tm == 0 and n % tn == 0
    return pl.pallas_call(
        _mm_kernel,
        grid=(m // tm, n // tn),
        in_specs=[pl.BlockSpec((tm, k), lambda i, j: (i, 0)),
                  pl.BlockSpec((k, tn), lambda i, j: (0, j))],
        out_specs=pl.BlockSpec((tm, tn), lambda i, j: (i, j)),
        out_shape=jax.ShapeDtypeStruct((m, n), out_dtype),
        compiler_params=pltpu.CompilerParams(
            dimension_semantics=("parallel", "parallel"),
            vmem_limit_bytes=VMEM_LIMIT_BYTES),
        name="matmul",
    )(a, b)


def _peer_dense_kernel(ht_ref, e0_ref, thr_ref, e1_ref, rank_ref, u_ref, vt_ref, g_ref, b_ref,
                       o_ref, hb_ref, acc_ref, p_ref, *, n_heads, rows_per_step, alpha):
    k = pl.program_id(1)
    bf = jnp.bfloat16

    @pl.when(k == 0)
    def _():
        acc_ref[...] = jnp.zeros_like(acc_ref)
        hb_ref[...] = ht_ref[...].astype(bf)

    a = jnp.dot(u_ref[...], hb_ref[...], preferred_element_type=jnp.float32)
    act = _gelu_exact(a).astype(bf)
    zero = jnp.zeros((), bf)
    tm = ht_ref.shape[1]

    def row_tile(ref, h, i):
        row = jnp.broadcast_to(ref[h, pl.ds(i, 1), :], (BF16_SUBLANES, tm)).astype(bf)
        return pltpu.repeat(row, LANES // BF16_SUBLANES, axis=0)

    for ii in range(rows_per_step):
        i = k * rows_per_step + ii
        w = jnp.zeros((LANES, tm), bf)
        for h in range(n_heads):
            thr_row = row_tile(thr_ref, h, i)
            e0_row = row_tile(e0_ref, h, i)
            w = w + jnp.where(rank_ref[h] < thr_row, e1_ref[h], zero) * e0_row
        p_ref[ii * LANES:(ii + 1) * LANES, :] = w * act[ii * LANES:(ii + 1) * LANES, :]
    acc_ref[...] += jnp.dot(vt_ref[...], p_ref[...], preferred_element_type=jnp.float32)

    @pl.when(k == pl.num_programs(1) - 1)
    def _():
        y = alpha * ht_ref[...] + acc_ref[...]
        mu = jnp.mean(y, axis=0, keepdims=True)
        yc = y - mu
        var = jnp.mean(yc * yc, axis=0, keepdims=True)
        o_ref[...] = (yc * lax.rsqrt(var + LN_EPS)).T * g_ref[...] + b_ref[...]


def _peer_dense(ht, e0, thr, e1, rank, u, vt, ln_g, ln_b, *, alpha, tm, rows_per_step):
    d, t = ht.shape
    n_heads, n_keys, _ = e0.shape
    assert n_keys == LANES
    n_experts = u.shape[0]
    eb = rows_per_step * LANES
    sel_spec = pl.BlockSpec((n_heads, n_keys, tm), lambda i, k: (0, 0, i))
    row_spec = pl.BlockSpec((n_heads, n_keys, tm), lambda i, k: (0, 0, i),
                            pipeline_mode=pl.Buffered(1))
    vec_spec = pl.BlockSpec((1, d), lambda i, k: (0, 0))
    return pl.pallas_call(
        functools.partial(_peer_dense_kernel, n_heads=n_heads, rows_per_step=rows_per_step,
                          alpha=alpha),
        grid=(t // tm, n_experts // eb),
        in_specs=[pl.BlockSpec((d, tm), lambda i, k: (0, i)),
                  row_spec, row_spec, sel_spec, sel_spec,
                  pl.BlockSpec((eb, d), lambda i, k: (k, 0)),
                  pl.BlockSpec((d, eb), lambda i, k: (0, k)),
                  vec_spec, vec_spec],
        out_specs=pl.BlockSpec((tm, d), lambda i, k: (i, 0)),
        out_shape=jax.ShapeDtypeStruct((t, d), jnp.float32),
        scratch_shapes=[pltpu.VMEM((d, tm), jnp.bfloat16),
                        pltpu.VMEM((d, tm), jnp.float32),
                        pltpu.VMEM((eb, tm), jnp.bfloat16)],
        compiler_params=pltpu.CompilerParams(
            dimension_semantics=("parallel", "arbitrary"),
            vmem_limit_bytes=PEER_VMEM_LIMIT_BYTES),
        name="peer_dense",
    )(ht, e0, thr, e1, rank, u, vt, ln_g[None, :], ln_b[None, :])


def _fold_keys_kernel(keys_ref, wq_ref, o_ref):
    o_ref[...] = lax.dot_general(
        keys_ref[0], wq_ref[...], (((1,), (1,)), ((), ())),
        precision=lax.Precision.HIGHEST, preferred_element_type=jnp.float32)


def _fold_keys(keys, wq):
    n_blocks, n_keys, half_key = keys.shape
    d = wq.shape[0]
    return pl.pallas_call(
        _fold_keys_kernel,
        grid=(n_blocks,),
        in_specs=[pl.BlockSpec((1, n_keys, half_key), lambda b: (b, 0, 0)),
                  pl.BlockSpec((d, half_key), lambda b: (0, b))],
        out_specs=pl.BlockSpec((n_keys, d), lambda b: (b, 0)),
        out_shape=jax.ShapeDtypeStruct((n_blocks * n_keys, d), jnp.float32),
        compiler_params=pltpu.CompilerParams(dimension_semantics=("parallel",)),
        name="fold_keys",
    )(keys, wq)


def _extract_top(work, n, row_ref, rank=None):
    for r in range(n):
        m = jnp.max(work, axis=0, keepdims=True)
        row_ref[r:r + 1, :] = m
        hit = work == m
        if rank is not None:
            rank = jnp.where(hit, float(r), rank)
        work = jnp.where(hit, -jnp.inf, work)
    return rank


def _peer_select_kernel(sc_ref, e0_ref, thr_ref, e1_ref, rank_ref, a_ref, b_ref, c_ref,
                        *, n_heads, n_keys):
    tl = sc_ref.shape[1]
    half = TOPK // 2

    def head_body(h, carry):
        r0 = pl.multiple_of(h * 2 * n_keys, 2 * n_keys)
        s0 = sc_ref[pl.ds(r0, n_keys), :]
        s1 = sc_ref[pl.ds(r0 + n_keys, n_keys), :]
        _extract_top(s0, TOPK, a_ref)
        rank1 = _extract_top(s1, TOPK, b_ref, jnp.full((n_keys, tl), 100.0, jnp.float32))
        pieces = [a_ref[0:1, :] + b_ref[...]]
        pieces += [a_ref[p:p + 1, :] + b_ref[0:half, :] for p in range(1, half)]
        pieces += [a_ref[half:TOPK, :] + b_ref[0:1, :]]
        _extract_top(jnp.concatenate(pieces, axis=0), TOPK, c_ref)
        best = c_ref[...]
        tau = best[TOPK - 1:TOPK, :]
        zsum = jnp.sum(jnp.exp(best - best[0:1, :]), axis=0, keepdims=True)
        b_all = b_ref[...]
        thr = jnp.zeros((n_keys, tl), jnp.float32)
        for p in range(TOPK):
            a_p = a_ref[p:p + 1, :]
            n_sel = jnp.sum(jnp.where(a_p + b_all >= tau, 1.0, 0.0), axis=0, keepdims=True)
            thr = jnp.where(s0 == a_p, n_sel, thr)
        e0_ref[h] = jnp.exp(s0 - a_ref[0:1, :]) / zsum
        thr_ref[h] = thr
        e1_ref[h] = jnp.exp(s1 - b_ref[0:1, :]).astype(e1_ref.dtype)
        rank_ref[h] = rank1.astype(rank_ref.dtype)
        return carry

    lax.fori_loop(0, n_heads, head_body, 0)


def _peer_select(scores_t, n_heads, n_keys, tl):
    rows, t = scores_t.shape
    assert rows == n_heads * 2 * n_keys
    out_spec = pl.BlockSpec((n_heads, n_keys, tl), lambda i: (0, 0, i))
    f32 = jax.ShapeDtypeStruct((n_heads, n_keys, t), jnp.float32)
    b16 = jax.ShapeDtypeStruct((n_heads, n_keys, t), jnp.bfloat16)
    return pl.pallas_call(
        functools.partial(_peer_select_kernel, n_heads=n_heads, n_keys=n_keys),
        grid=(t // tl,),
        in_specs=[pl.BlockSpec((rows, tl), lambda i: (0, i))],
        out_specs=[out_spec, out_spec, out_spec, out_spec],
        out_shape=[f32, f32, b16, b16],
        scratch_shapes=[pltpu.VMEM((TOPK, tl), jnp.float32)] * 3,
        compiler_params=pltpu.CompilerParams(
            dimension_semantics=("parallel",), vmem_limit_bytes=VMEM_LIMIT_BYTES),
        name="peer_select",
    )(scores_t)


def _log_sigmoid(x):
    return jnp.minimum(x, 0.0) - jnp.log1p(jnp.exp(-jnp.abs(x)))


def _mlstm_kernel(q_ref, k_ref, v_ref, o_ref, g_ref, gt_ref,
                  qm_ref, km_ref, vm_ref, om_ref, gm_ref, gmt_ref,
                  bias_ref, biast_ref, nw_ref, y_ref, c_ref, m_ref,
                  *, n_heads, head_qk, head_v):
    c = pl.program_id(1)
    tc = q_ref.shape[0]
    f32, bf = jnp.float32, jnp.bfloat16
    first = c == 0

    @pl.when(first)
    def _():
        c_ref[...] = jnp.zeros_like(c_ref)
        m_ref[...] = jnp.zeros_like(m_ref)

    q = jnp.where(first, qm_ref[...], q_ref[...])
    k = jnp.where(first, km_ref[...], k_ref[...])
    v = jnp.where(first, vm_ref[...], v_ref[...])
    o = jnp.where(first, om_ref[...], o_ref[...])
    g = jnp.where(first, gm_ref[...], g_ref[...]) + bias_ref[...]
    gt = jnp.where(first, gmt_ref[...], gt_ref[...]) + biast_ref[...]

    row = lax.broadcasted_iota(jnp.int32, (tc, tc), 0)
    col = lax.broadcasted_iota(jnp.int32, (tc, tc), 1)
    causal = col <= row
    tri = jnp.where(causal, 1.0, 0.0).astype(f32)
    b_cols = jnp.dot(tri, _log_sigmoid(g), precision=lax.Precision.HIGHEST,
                     preferred_element_type=f32)
    b_rows = lax.dot_general(_log_sigmoid(gt), tri, (((1,), (1,)), ((), ())),
                             precision=lax.Precision.HIGHEST, preferred_element_type=f32)
    ln_scale = -0.5 * math.log(head_qk)
    ones_blk = jnp.where(lax.broadcasted_iota(jnp.int32, (tc, LANES), 1) == 0, 1.0, 0.0).astype(bf)

    for h in range(n_heads):
        qh = q[:, h * head_qk:(h + 1) * head_qk]
        kh = k[:, h * head_qk:(h + 1) * head_qk]
        v_ext = jnp.concatenate([v[:, h * head_v:(h + 1) * head_v], ones_blk], axis=1)
        li_col = g[:, h:h + 1]
        b_col = b_cols[:, n_heads + h:n_heads + h + 1]
        li_row = gt[h:h + 1, :]
        b_row = b_rows[n_heads + h:n_heads + h + 1, :]
        m_prev = m_ref[h:h + 1, 0:1]
        d = jnp.where(causal, b_col + (li_row - b_row), -jnp.inf)
        inter = b_col + m_prev
        m_t = jnp.maximum(inter, jnp.max(d, axis=1, keepdims=True))
        w = jnp.exp(d - (m_t - ln_scale))
        s = lax.dot_general(qh, kh, (((1,), (1,)), ((), ())), preferred_element_type=f32) * w
        sc_inter = jnp.exp(inter - (m_t - ln_scale))
        c_ext = c_ref[h]
        num_ext = (jnp.dot(s.astype(bf), v_ext, preferred_element_type=f32)
                   + sc_inter * jnp.dot(qh, c_ext.astype(bf), preferred_element_type=f32))
        den = num_ext[:, head_v:head_v + 1]
        hm = num_ext[:, :head_v] / jnp.maximum(jnp.abs(den), jnp.exp(-m_t))
        hm = hm * lax.rsqrt(jnp.mean(hm * hm, axis=1, keepdims=True) + LN_EPS)
        gate_o = jax.nn.sigmoid(o[:, h * head_v:(h + 1) * head_v].astype(f32))
        y_ref[:, h * head_v:(h + 1) * head_v] = (
            gate_o * (hm * nw_ref[:, h * head_v:(h + 1) * head_v])).astype(y_ref.dtype)
        b_last = b_col[tc - 1:tc, :]
        dl = b_last - b_col + li_col
        m_new = jnp.maximum(b_last + m_prev, jnp.max(dl, axis=0, keepdims=True))
        wv = (jnp.exp(dl - m_new) * v_ext.astype(f32)).astype(bf)
        c_ref[h] = (jnp.exp(b_last + m_prev - m_new) * c_ext
                    + lax.dot_general(kh, wv, (((0,), (0,)), ((), ())),
                                      preferred_element_type=f32))
        m_ref[h:h + 1, :] = jnp.broadcast_to(m_new, (1, LANES))


def _mlstm(proj_x, proj_m, gif_x, gif_m, b_if, norm_w, *, bsz, seq, n_heads, head_qk, head_v,
           off_q, off_v):
    tc = min(MLSTM_CHUNK, seq)
    assert seq % tc == 0
    n_meta = proj_m.shape[0]
    d_mqk, d_mv = n_heads * head_qk, n_heads * head_v
    assert off_q % d_mqk == 0 and off_v % d_mv == 0 and n_meta <= tc
    nc_x = seq // tc
    bf = jnp.bfloat16
    lead = tc - n_meta
    pad_rows = lambda a: jnp.pad(a, ((lead, 0), (0, 0)))
    qm = pad_rows(proj_m[:, off_q:off_q + d_mqk])
    km = pad_rows(proj_m[:, off_q + d_mqk:off_q + 2 * d_mqk])
    vm = pad_rows(proj_m[:, off_v:off_v + d_mv])
    om = pad_rows(proj_m[:, off_v + d_mv:off_v + 2 * d_mv])
    pad_gate = jnp.concatenate([jnp.full((lead, n_heads), I_GATE_PAD, jnp.float32),
                                jnp.full((lead, n_heads), -I_GATE_PAD, jnp.float32),
                                jnp.zeros((lead, LANES - 2 * n_heads), jnp.float32)], axis=1)
    gm = jnp.concatenate([pad_gate, gif_m], axis=0)
    bias = jnp.pad(b_if, (0, LANES - 2 * n_heads))[None, :]
    biast = jnp.broadcast_to(b_if[:, None], (2 * n_heads, tc))
    gt_x = gif_x[:, :2 * n_heads].T
    gt_m = gm[:, :2 * n_heads].T

    xrow = lambda b, c: b * nc_x + jnp.maximum(c - 1, 0)
    const = lambda b, c: (0, 0)
    return pl.pallas_call(
        functools.partial(_mlstm_kernel, n_heads=n_heads, head_qk=head_qk, head_v=head_v),
        grid=(bsz, nc_x + 1),
        in_specs=[pl.BlockSpec((tc, d_mqk), lambda b, c: (xrow(b, c), off_q // d_mqk)),
                  pl.BlockSpec((tc, d_mqk), lambda b, c: (xrow(b, c), off_q // d_mqk + 1)),
                  pl.BlockSpec((tc, d_mv), lambda b, c: (xrow(b, c), off_v // d_mv)),
                  pl.BlockSpec((tc, d_mv), lambda b, c: (xrow(b, c), off_v // d_mv + 1)),
                  pl.BlockSpec((tc, LANES), lambda b, c: (xrow(b, c), 0)),
                  pl.BlockSpec((2 * n_heads, tc), lambda b, c: (0, xrow(b, c))),
                  pl.BlockSpec((tc, d_mqk), const), pl.BlockSpec((tc, d_mqk), const),
                  pl.BlockSpec((tc, d_mv), const), pl.BlockSpec((tc, d_mv), const),
                  pl.BlockSpec((tc, LANES), const), pl.BlockSpec((2 * n_heads, tc), const),
                  pl.BlockSpec((1, LANES), const), pl.BlockSpec((2 * n_heads, tc), const),
                  pl.BlockSpec((1, d_mv), const)],
        out_specs=pl.BlockSpec((tc, d_mv), lambda b, c: (xrow(b, c), 0)),
        out_shape=jax.ShapeDtypeStruct((bsz * seq, d_mv), bf),
        scratch_shapes=[pltpu.VMEM((n_heads, head_qk, head_v + LANES), jnp.float32),
                        pltpu.VMEM((n_heads, LANES), jnp.float32)],
        compiler_params=pltpu.CompilerParams(
            dimension_semantics=("parallel", "arbitrary"),
            vmem_limit_bytes=VMEM_LIMIT_BYTES),
        name="mlstm",
    )(proj_x, proj_x, proj_x, proj_x, gif_x, gt_x, qm, km, vm, om, gm, gt_m,
      bias, biast, norm_w[None, :])


def _mix_kernel(x_ref, cb_ref, cc_ref, cx_ref, cch_ref, cxh_ref, ccm_ref, cxm_ref,
                yb_ref, ga_ref, gb_ref, pa_ref, pb_ref, wo_ref, cw_ref,
                g0_ref, b0_ref, g1_ref, b1_ref, h1t_ref,
                *, tiles_per_batch, alpha):
    i = pl.program_id(0)
    f32, bf = jnp.float32, jnp.bfloat16
    tm = x_ref.shape[0]
    z = cc_ref[...].astype(f32) * cx_ref[...].astype(f32)
    batch_start = (i % tiles_per_batch) == 0
    halo = jnp.where(batch_start,
                     ccm_ref[...].astype(f32) * cxm_ref[...].astype(f32),
                     cch_ref[...].astype(f32) * cxh_ref[...].astype(f32))
    sub = lax.broadcasted_iota(jnp.int32, halo.shape, 0)
    n_halo = halo.shape[0]
    y = cw_ref[2:3, :] * z
    for shift in (1, 2):
        rolled = pltpu.roll(z, shift, 0)
        head = jnp.where(sub < shift, pltpu.roll(halo, shift, 0), rolled[:n_halo])
        y = y + cw_ref[2 - shift:3 - shift, :] * jnp.concatenate([head, rolled[n_halo:]], axis=0)
    y_a = (cb_ref[...].astype(f32) * y).astype(bf)
    za = jnp.dot(y_a, pa_ref[...], preferred_element_type=f32)
    zb = jnp.dot(yb_ref[...], pb_ref[...], preferred_element_type=f32)
    merged = (jax.nn.sigmoid(ga_ref[...].astype(f32)) * za
              + jax.nn.sigmoid(gb_ref[...].astype(f32)) * zb).astype(bf)
    mix = jnp.dot(merged, wo_ref[...], preferred_element_type=f32)
    h0 = _ln_rows(x_ref[...], g0_ref[...], b0_ref[...])
    h1t_ref[...] = _ln_rows(alpha * h0 + mix, g1_ref[...], b1_ref[...]).T


def _mix(x2, proj_x, proj_m, y_b, p_a, p_b, w_o, conv_w, ln0_g, ln0_b, ln1_g, ln1_b,
         *, seq, off_g, alpha, tm):
    t_x, d = x2.shape
    d_conv = conv_w.shape[-1]
    d_mv = y_b.shape[-1]
    n_meta = proj_m.shape[0]
    halo = 8
    assert seq % tm == 0 and tm % halo == 0 and n_meta % halo == 0 and off_g % d == 0
    assert conv_w.shape[0] == 3
    tile = lambda j: pl.BlockSpec((tm, d_conv), lambda i: (i, j))
    prev = lambda j: pl.BlockSpec((halo, d_conv),
                                  lambda i: (jnp.maximum(i * (tm // halo) - 1, 0), j))
    meta = lambda j: pl.BlockSpec((halo, d_conv), lambda i: (n_meta // halo - 1, j))
    const = lambda shape: pl.BlockSpec(shape, lambda i: (0, 0), pipeline_mode=pl.Buffered(1))
    vec = lambda a: a[None, :]
    return pl.pallas_call(
        functools.partial(_mix_kernel, tiles_per_batch=seq // tm, alpha=alpha),
        grid=(t_x // tm,),
        in_specs=[pl.BlockSpec((tm, d), lambda i: (i, 0)),
                  tile(0), tile(1), tile(2), prev(1), prev(2), meta(1), meta(2),
                  pl.BlockSpec((tm, d_mv), lambda i: (i, 0)),
                  pl.BlockSpec((tm, d), lambda i: (i, off_g // d)),
                  pl.BlockSpec((tm, d), lambda i: (i, off_g // d + 1)),
                  const(p_a.shape), const(p_b.shape), const(w_o.shape), const(conv_w.shape),
                  const((1, d)), const((1, d)), const((1, d)), const((1, d))],
        out_specs=pl.BlockSpec((d, tm), lambda i: (0, i)),
        out_shape=jax.ShapeDtypeStruct((d, t_x), jnp.float32),
        compiler_params=pltpu.CompilerParams(
            dimension_semantics=("parallel",), vmem_limit_bytes=VMEM_LIMIT_BYTES),
        name="mix",
    )(x2, proj_x, proj_x, proj_x, proj_x, proj_x, proj_m, proj_m, y_b, proj_x, proj_x,
      p_a, p_b, w_o, conv_w, vec(ln0_g), vec(ln0_b), vec(ln1_g), vec(ln1_b))


def kernel(x, meta_tokens, ln0_g, ln0_b, w_in, b_if, conv_w, mh_norm_w, p_a, p_b, w_o,
           ln1_g, ln1_b, peer_wq, peer_keys, peer_u, peer_v, ln2_g, ln2_b):
    depth = w_in.shape[0]
    assert depth == 1
    bsz, seq, d = x.shape
    n_meta = meta_tokens.shape[0]
    n_in = w_in.shape[-1]
    d_conv = conv_w.shape[-1]
    d_mv = mh_norm_w.shape[-1]
    n_mh = b_if.shape[-1] // 2
    d_mqk = (n_in - 3 * d_conv - 2 * d_mv - 2 * n_mh - 2 * d) // 2
    head_qk = d_mqk // n_mh
    head_v = d_mv // n_mh
    alpha = (2 * depth) ** 0.25
    bf = jnp.bfloat16
    t_x = bsz * seq

    x2 = x.reshape(t_x, d)
    gate_off = 3 * d_conv + 2 * d_mqk + 2 * d_mv
    w = w_in[0]
    w_main = jnp.concatenate([w[:, :gate_off], w[:, gate_off + 2 * n_mh:]], axis=1).astype(bf)
    w_gif = jnp.pad(w[:, gate_off:gate_off + 2 * n_mh], ((0, 0), (0, LANES - 2 * n_mh))).astype(bf)
    proj_x, gif_x = _inproj(x2, ln0_g, ln0_b, w_main, w_gif, 1024, 512)
    proj_m, gif_m = _inproj(meta_tokens, ln0_g, ln0_b, w_main, w_gif, 1024, 512)

    off_q = 3 * d_conv
    off_v = off_q + 2 * d_mqk
    off_g = off_v + 2 * d_mv
    y_b = _mlstm(proj_x, proj_m, gif_x, gif_m, b_if[0], mh_norm_w[0], bsz=bsz, seq=seq,
                 n_heads=n_mh, head_qk=head_qk, head_v=head_v, off_q=off_q, off_v=off_v)
    h1_t = _mix(x2, proj_x, proj_m, y_b,
                p_a[0].astype(bf), p_b[0].astype(bf), w_o[0].astype(bf), conv_w[0],
                ln0_g, ln0_b, ln1_g[0], ln1_b[0], seq=seq, off_g=off_g, alpha=alpha,
                tm=MIX_TILE)

    n_ph, _, n_keys, half_key = peer_keys.shape[1:]
    w_fold = _fold_keys(peer_keys[0].reshape(n_ph * 2, n_keys, half_key), peer_wq[0])
    scores_t = _matmul(w_fold, h1_t, jnp.float32, 1024, 512)
    e0, thr, e1, rank = _peer_select(scores_t, n_ph, n_keys, LANES)
    out = _peer_dense(h1_t, e0, thr, e1, rank, peer_u[0].astype(bf), peer_v[0].T.astype(bf),
                      ln2_g[0], ln2_b[0], alpha=alpha, tm=512, rows_per_step=8)
    return out.reshape(bsz, seq, d)
```

```python
import functools
import math

import jax
import jax.numpy as jnp
from jax import lax
from jax.experimental import pallas as pl
from jax.experimental.pallas import tpu as pltpu

TOPK = 16
LN_EPS = 1e-5
I_GATE_PAD = -1e30
MLSTM_CHUNK = 256
MIX_TILE = 256
INPROJ_TILE_M = 1024
INPROJ_TILE_N = 1024
LANES = 128
BF16_SUBLANES = 16
VMEM_LIMIT_BYTES = 56 * 1024 * 1024
PEER_VMEM_LIMIT_BYTES = 58 * 1024 * 1024


def _gelu_exact(x):
    return 0.5 * x * (1.0 + lax.erf(x * (2.0 ** -0.5)))


def _ln_rows(x, g, b):
    mu = jnp.mean(x, axis=1, keepdims=True)
    xc = x - mu
    var = jnp.mean(xc * xc, axis=1, keepdims=True)
    return xc * lax.rsqrt(var + LN_EPS) * g + b


def _inproj_kernel(x_ref, g_ref, b_ref, w_ref, wg_ref, o_ref, og_ref, h_ref):
    @pl.when(pl.program_id(1) == 0)
    def _():
        h = _ln_rows(x_ref[...], g_ref[...], b_ref[...]).astype(h_ref.dtype)
        h_ref[...] = h
        og_ref[...] = jnp.dot(h, wg_ref[...], preferred_element_type=jnp.float32)

    o_ref[...] = jnp.dot(h_ref[...], w_ref[...],
                         preferred_element_type=jnp.float32).astype(o_ref.dtype)


def _inproj(x2, ln_g, ln_b, w_main, w_gate, tm, tn):
    m, d = x2.shape
    n = w_main.shape[1]
    tm = min(tm, m)
    assert m % tm == 0 and n % tn == 0 and w_gate.shape[1] == LANES
    const = lambda shape: pl.BlockSpec(shape, lambda i, j: (0, 0))
    return pl.pallas_call(
        _inproj_kernel,
        grid=(m // tm, n // tn),
        in_specs=[pl.BlockSpec((tm, d), lambda i, j: (i, 0)),
                  const((1, d)), const((1, d)),
                  pl.BlockSpec((d, tn), lambda i, j: (0, j)),
                  const((d, LANES))],
        out_specs=[pl.BlockSpec((tm, tn), lambda i, j: (i, j)),
                   pl.BlockSpec((tm, LANES), lambda i, j: (i, 0))],
        out_shape=[jax.ShapeDtypeStruct((m, n), jnp.bfloat16),
                   jax.ShapeDtypeStruct((m, LANES), jnp.float32)],
        scratch_shapes=[pltpu.VMEM((tm, d), jnp.bfloat16)],
        compiler_params=pltpu.CompilerParams(
            dimension_semantics=("parallel", "arbitrary"),
            vmem_limit_bytes=VMEM_LIMIT_BYTES),
        name="inproj",
    )(x2, ln_g[None, :], ln_b[None, :], w_main, w_gate)


def _mm_kernel(a_ref, b_ref, o_ref):
    bf = jnp.bfloat16
    o_ref[...] = jnp.dot(a_ref[...].astype(bf), b_ref[...].astype(bf),
                         preferred_element_type=jnp.float32).astype(o_ref.dtype)


def _matmul(a, b, out_dtype, tm, tn):
    m, k = a.shape
    _, n = b.shape
    tm = min(tm, m)
    tn = min(tn, n)
    assert m % tm == 0 and n % tn == 0
    return pl.pallas_call(
        _mm_kernel,
        grid=(m // tm, n // tn),
        in_specs=[pl.BlockSpec((tm, k), lambda i, j: (i, 0)),
                  pl.BlockSpec((k, tn), lambda i, j: (0, j))],
        out_specs=pl.BlockSpec((tm, tn), lambda i, j: (i, j)),
        out_shape=jax.ShapeDtypeStruct((m, n), out_dtype),
        compiler_params=pltpu.CompilerParams(
            dimension_semantics=("parallel", "parallel"),
            vmem_limit_bytes=VMEM_LIMIT_BYTES),
        name="matmul",
    )(a, b)


def _peer_dense_kernel(ht_ref, e0_ref, thr_ref, e1_ref, rank_ref, u_ref, vt_ref, g_ref, b_ref,
                       o_ref, hb_ref, acc_ref, w_ref, p_ref, *, n_heads, rows_per_step, alpha):
    k = pl.program_id(1)
    bf = jnp.bfloat16
    tm = ht_ref.shape[1]
    zero = jnp.zeros((), bf)

    def row_tile(ref, h, ii):
        row = jnp.broadcast_to(ref[h, ii:ii + 1, :], (BF16_SUBLANES, tm)).astype(bf)
        return pltpu.repeat(row, LANES // BF16_SUBLANES, axis=0)

    @pl.when(k == 0)
    def _():
        acc_ref[...] = jnp.zeros_like(acc_ref)
        hb_ref[...] = ht_ref[...].astype(bf)

    for ii in range(rows_per_step):
        w = jnp.zeros((LANES, tm), bf)
        for h in range(n_heads):
            sel = rank_ref[h] < row_tile(thr_ref, h, ii)
            w = w + jnp.where(sel, e1_ref[h], zero) * row_tile(e0_ref, h, ii)
        w_ref[ii * LANES:(ii + 1) * LANES, :] = w
    a = jnp.dot(u_ref[...], hb_ref[...], preferred_element_type=jnp.float32)
    p_ref[...] = w_ref[...] * _gelu_exact(a).astype(bf)
    acc_ref[...] += jnp.dot(vt_ref[...], p_ref[...], preferred_element_type=jnp.float32)

    @pl.when(k == pl.num_programs(1) - 1)
    def _():
        y = alpha * ht_ref[...] + acc_ref[...]
        mu = jnp.mean(y, axis=0, keepdims=True)
        yc = y - mu
        var = jnp.mean(yc * yc, axis=0, keepdims=True)
        o_ref[...] = (yc * lax.rsqrt(var + LN_EPS)).T * g_ref[...] + b_ref[...]


def _peer_dense(ht, e0, thr, e1, rank, u, vt, ln_g, ln_b, *, alpha, tm, rows_per_step):
    d, t = ht.shape
    n_heads, n_keys, _ = e0.shape
    assert n_keys == LANES
    n_experts = u.shape[0]
    eb = rows_per_step * LANES
    sel_spec = pl.BlockSpec((n_heads, n_keys, tm), lambda i, k: (0, 0, i))
    row_spec = pl.BlockSpec((n_heads, rows_per_step, tm), lambda i, k: (0, k, i))
    vec_spec = pl.BlockSpec((1, d), lambda i, k: (0, 0))
    return pl.pallas_call(
        functools.partial(_peer_dense_kernel, n_heads=n_heads, rows_per_step=rows_per_step,
                          alpha=alpha),
        grid=(t // tm, n_experts // eb),
        in_specs=[pl.BlockSpec((d, tm), lambda i, k: (0, i)),
                  row_spec, row_spec, sel_spec, sel_spec,
                  pl.BlockSpec((eb, d), lambda i, k: (k, 0)),
                  pl.BlockSpec((d, eb), lambda i, k: (0, k)),
                  vec_spec, vec_spec],
        out_specs=pl.BlockSpec((tm, d), lambda i, k: (i, 0)),
        out_shape=jax.ShapeDtypeStruct((t, d), jnp.float32),
        scratch_shapes=[pltpu.VMEM((d, tm), jnp.bfloat16),
                        pltpu.VMEM((d, tm), jnp.float32),
                        pltpu.VMEM((eb, tm), jnp.bfloat16),
                        pltpu.VMEM((eb, tm), jnp.bfloat16)],
        compiler_params=pltpu.CompilerParams(
            dimension_semantics=("parallel", "arbitrary"),
            vmem_limit_bytes=PEER_VMEM_LIMIT_BYTES),
        name="peer_dense",
    )(ht, e0, thr, e1, rank, u, vt, ln_g[None, :], ln_b[None, :])


def _fold_keys_kernel(keys_ref, wq_ref, o_ref):
    o_ref[...] = lax.dot_general(
        keys_ref[0], wq_ref[...], (((1,), (1,)), ((), ())),
        precision=lax.Precision.HIGHEST, preferred_element_type=jnp.float32)


def _fold_keys(keys, wq):
    n_blocks, n_keys, half_key = keys.shape
    d = wq.shape[0]
    return pl.pallas_call(
        _fold_keys_kernel,
        grid=(n_blocks,),
        in_specs=[pl.BlockSpec((1, n_keys, half_key), lambda b: (b, 0, 0)),
                  pl.BlockSpec((d, half_key), lambda b: (0, b))],
        out_specs=pl.BlockSpec((n_keys, d), lambda b: (b, 0)),
        out_shape=jax.ShapeDtypeStruct((n_blocks * n_keys, d), jnp.float32),
        compiler_params=pltpu.CompilerParams(dimension_semantics=("parallel",)),
        name="fold_keys",
    )(keys, wq)


def _extract_top(work, n, row_ref, rank=None):
    for r in range(n):
        m = jnp.max(work, axis=0, keepdims=True)
        row_ref[r:r + 1, :] = m
        hit = work == m
        if rank is not None:
            rank = jnp.where(hit, float(r), rank)
        work = jnp.where(hit, -jnp.inf, work)
    return rank


def _peer_select_kernel(sc_ref, e0_ref, thr_ref, e1_ref, rank_ref, a_ref, b_ref, c_ref,
                        *, n_heads, n_keys):
    tl = sc_ref.shape[1]
    half = TOPK // 2

    def head_body(h, carry):
        r0 = pl.multiple_of(h * 2 * n_keys, 2 * n_keys)
        s0 = sc_ref[pl.ds(r0, n_keys), :]
        s1 = sc_ref[pl.ds(r0 + n_keys, n_keys), :]
        _extract_top(s0, TOPK, a_ref)
        rank1 = _extract_top(s1, TOPK, b_ref, jnp.full((n_keys, tl), 100.0, jnp.float32))
        pieces = [a_ref[0:1, :] + b_ref[...]]
        pieces += [a_ref[p:p + 1, :] + b_ref[0:half, :] for p in range(1, half)]
        pieces += [a_ref[half:TOPK, :] + b_ref[0:1, :]]
        _extract_top(jnp.concatenate(pieces, axis=0), TOPK, c_ref)
        best = c_ref[...]
        tau = best[TOPK - 1:TOPK, :]
        zsum = jnp.sum(jnp.exp(best - best[0:1, :]), axis=0, keepdims=True)
        b_all = b_ref[...]
        thr = jnp.zeros((n_keys, tl), jnp.float32)
        for p in range(TOPK):
            a_p = a_ref[p:p + 1, :]
            n_sel = jnp.sum(jnp.where(a_p + b_all >= tau, 1.0, 0.0), axis=0, keepdims=True)
            thr = jnp.where(s0 == a_p, n_sel, thr)
        e0_ref[h] = jnp.exp(s0 - a_ref[0:1, :]) / zsum
        thr_ref[h] = thr
        e1_ref[h] = jnp.exp(s1 - b_ref[0:1, :]).astype(e1_ref.dtype)
        rank_ref[h] = rank1.astype(rank_ref.dtype)
        return carry

    lax.fori_loop(0, n_heads, head_body, 0)


def _peer_select(scores_t, n_heads, n_keys, tl):
    rows, t = scores_t.shape
    assert rows == n_heads * 2 * n_keys
    out_spec = pl.BlockSpec((n_heads, n_keys, tl), lambda i: (0, 0, i))
    f32 = jax.ShapeDtypeStruct((n_heads, n_keys, t), jnp.float32)
    b16 = jax.ShapeDtypeStruct((n_heads, n_keys, t), jnp.bfloat16)
    return pl.pallas_call(
        functools.partial(_peer_select_kernel, n_heads=n_heads, n_keys=n_keys),
        grid=(t // tl,),
        in_specs=[pl.BlockSpec((rows, tl), lambda i: (0, i))],
        out_specs=[out_spec, out_spec, out_spec, out_spec],
        out_shape=[f32, f32, b16, b16],
        scratch_shapes=[pltpu.VMEM((TOPK, tl), jnp.float32)] * 3,
        compiler_params=pltpu.CompilerParams(
            dimension_semantics=("parallel",), vmem_limit_bytes=VMEM_LIMIT_BYTES),
        name="peer_select",
    )(scores_t)


def _log_sigmoid(x):
    return jnp.minimum(x, 0.0) - jnp.log1p(jnp.exp(-jnp.abs(x)))


def _mlstm_kernel(q_ref, k_ref, v_ref, o_ref, g_ref, gt_ref,
                  qm_ref, km_ref, vm_ref, om_ref, gm_ref, gmt_ref,
                  bias_ref, biast_ref, nw_ref, y_ref, c_ref, m_ref,
                  *, n_heads, head_qk, head_v):
    c = pl.program_id(1)
    tc = q_ref.shape[0]
    f32, bf = jnp.float32, jnp.bfloat16
    first = c == 0

    @pl.when(first)
    def _():
        c_ref[...] = jnp.zeros_like(c_ref)
        m_ref[...] = jnp.zeros_like(m_ref)

    q = jnp.where(first, qm_ref[...], q_ref[...])
    k = jnp.where(first, km_ref[...], k_ref[...])
    v = jnp.where(first, vm_ref[...], v_ref[...])
    o = jnp.where(first, om_ref[...], o_ref[...])
    g = jnp.where(first, gm_ref[...], g_ref[...]) + bias_ref[...]
    gt = jnp.where(first, gmt_ref[...], gt_ref[...]) + biast_ref[...]

    row = lax.broadcasted_iota(jnp.int32, (tc, tc), 0)
    col = lax.broadcasted_iota(jnp.int32, (tc, tc), 1)
    causal = col <= row
    tri = jnp.where(causal, 1.0, 0.0).astype(f32)
    b_cols = jnp.dot(tri, _log_sigmoid(g), precision=lax.Precision.HIGHEST,
                     preferred_element_type=f32)
    b_rows = lax.dot_general(_log_sigmoid(gt), tri, (((1,), (1,)), ((), ())),
                             precision=lax.Precision.HIGHEST, preferred_element_type=f32)
    ln_scale = -0.5 * math.log(head_qk)
    ones_blk = jnp.where(lax.broadcasted_iota(jnp.int32, (tc, LANES), 1) == 0, 1.0, 0.0).astype(bf)

    for h in range(n_heads):
        qh = q[:, h * head_qk:(h + 1) * head_qk]
        kh = k[:, h * head_qk:(h + 1) * head_qk]
        v_ext = jnp.concatenate([v[:, h * head_v:(h + 1) * head_v], ones_blk], axis=1)
        li_col = g[:, h:h + 1]
        b_col = b_cols[:, n_heads + h:n_heads + h + 1]
        li_row = gt[h:h + 1, :]
        b_row = b_rows[n_heads + h:n_heads + h + 1, :]
        m_prev = m_ref[h:h + 1, 0:1]
        d = jnp.where(causal, b_col + (li_row - b_row), -jnp.inf)
        inter = b_col + m_prev
        m_t = jnp.maximum(inter, jnp.max(d, axis=1, keepdims=True))
        w = jnp.exp(d - (m_t - ln_scale))
        s = lax.dot_general(qh, kh, (((1,), (1,)), ((), ())), preferred_element_type=f32) * w
        sc_inter = jnp.exp(inter - (m_t - ln_scale))
        c_ext = c_ref[h]
        num_ext = (jnp.dot(s.astype(bf), v_ext, preferred_element_type=f32)
                   + sc_inter * jnp.dot(qh, c_ext.astype(bf), preferred_element_type=f32))
        den = num_ext[:, head_v:head_v + 1]
        hm = num_ext[:, :head_v] / jnp.maximum(jnp.abs(den), jnp.exp(-m_t))
        hm = hm * lax.rsqrt(jnp.mean(hm * hm, axis=1, keepdims=True) + LN_EPS)
        gate_o = jax.nn.sigmoid(o[:, h * head_v:(h + 1) * head_v].astype(f32))
        y_ref[:, h * head_v:(h + 1) * head_v] = (
            gate_o * (hm * nw_ref[:, h * head_v:(h + 1) * head_v])).astype(y_ref.dtype)
        b_last = b_col[tc - 1:tc, :]
        dl = b_last - b_col + li_col
        m_new = jnp.maximum(b_last + m_prev, jnp.max(dl, axis=0, keepdims=True))
        wv = (jnp.exp(dl - m_new) * v_ext.astype(f32)).astype(bf)
        c_ref[h] = (jnp.exp(b_last + m_prev - m_new) * c_ext
                    + lax.dot_general(kh, wv, (((0,), (0,)), ((), ())),
                                      preferred_element_type=f32))
        m_ref[h:h + 1, :] = jnp.broadcast_to(m_new, (1, LANES))


def _mlstm(proj_x, proj_m, gif_x, gif_m, b_if, norm_w, *, bsz, seq, n_heads, head_qk, head_v,
           off_q, off_v):
    tc = min(MLSTM_CHUNK, seq)
    assert seq % tc == 0
    n_meta = proj_m.shape[0]
    d_mqk, d_mv = n_heads * head_qk, n_heads * head_v
    assert off_q % d_mqk == 0 and off_v % d_mv == 0 and n_meta <= tc
    nc_x = seq // tc
    bf = jnp.bfloat16
    lead = tc - n_meta
    pad_rows = lambda a: jnp.pad(a, ((lead, 0), (0, 0)))
    qm = pad_rows(proj_m[:, off_q:off_q + d_mqk])
    km = pad_rows(proj_m[:, off_q + d_mqk:off_q + 2 * d_mqk])
    vm = pad_rows(proj_m[:, off_v:off_v + d_mv])
    om = pad_rows(proj_m[:, off_v + d_mv:off_v + 2 * d_mv])
    pad_gate = jnp.concatenate([jnp.full((lead, n_heads), I_GATE_PAD, jnp.float32),
                                jnp.full((lead, n_heads), -I_GATE_PAD, jnp.float32),
                                jnp.zeros((lead, LANES - 2 * n_heads), jnp.float32)], axis=1)
    gm = jnp.concatenate([pad_gate, gif_m], axis=0)
    bias = jnp.pad(b_if, (0, LANES - 2 * n_heads))[None, :]
    biast = jnp.broadcast_to(b_if[:, None], (2 * n_heads, tc))
    gt_x = gif_x[:, :2 * n_heads].T
    gt_m = gm[:, :2 * n_heads].T

    xrow = lambda b, c: b * nc_x + jnp.maximum(c - 1, 0)
    const = lambda b, c: (0, 0)
    return pl.pallas_call(
        functools.partial(_mlstm_kernel, n_heads=n_heads, head_qk=head_qk, head_v=head_v),
        grid=(bsz, nc_x + 1),
        in_specs=[pl.BlockSpec((tc, d_mqk), lambda b, c: (xrow(b, c), off_q // d_mqk)),
                  pl.BlockSpec((tc, d_mqk), lambda b, c: (xrow(b, c), off_q // d_mqk + 1)),
                  pl.BlockSpec((tc, d_mv), lambda b, c: (xrow(b, c), off_v // d_mv)),
                  pl.BlockSpec((tc, d_mv), lambda b, c: (xrow(b, c), off_v // d_mv + 1)),
                  pl.BlockSpec((tc, LANES), lambda b, c: (xrow(b, c), 0)),
                  pl.BlockSpec((2 * n_heads, tc), lambda b, c: (0, xrow(b, c))),
                  pl.BlockSpec((tc, d_mqk), const), pl.BlockSpec((tc, d_mqk), const),
                  pl.BlockSpec((tc, d_mv), const), pl.BlockSpec((tc, d_mv), const),
                  pl.BlockSpec((tc, LANES), const), pl.BlockSpec((2 * n_heads, tc), const),
                  pl.BlockSpec((1, LANES), const), pl.BlockSpec((2 * n_heads, tc), const),
                  pl.BlockSpec((1, d_mv), const)],
        out_specs=pl.BlockSpec((tc, d_mv), lambda b, c: (xrow(b, c), 0)),
        out_shape=jax.ShapeDtypeStruct((bsz * seq, d_mv), bf),
        scratch_shapes=[pltpu.VMEM((n_heads, head_qk, head_v + LANES), jnp.float32),
                        pltpu.VMEM((n_heads, LANES), jnp.float32)],
        compiler_params=pltpu.CompilerParams(
            dimension_semantics=("parallel", "arbitrary"),
            vmem_limit_bytes=VMEM_LIMIT_BYTES),
        name="mlstm",
    )(proj_x, proj_x, proj_x, proj_x, gif_x, gt_x, qm, km, vm, om, gm, gt_m,
      bias, biast, norm_w[None, :])


def _mix_kernel(x_ref, cb_ref, cc_ref, cx_ref, cch_ref, cxh_ref, ccm_ref, cxm_ref,
                yb_ref, ga_ref, gb_ref, pa_ref, pb_ref, wo_ref, cw_ref,
                g0_ref, b0_ref, g1_ref, b1_ref, h1t_ref,
                *, tiles_per_batch, alpha):
    i = pl.program_id(0)
    f32, bf = jnp.float32, jnp.bfloat16
    tm = x_ref.shape[0]
    z = cc_ref[...].astype(f32) * cx_ref[...].astype(f32)
    batch_start = (i % tiles_per_batch) == 0
    halo = jnp.where(batch_start,
                     ccm_ref[...].astype(f32) * cxm_ref[...].astype(f32),
                     cch_ref[...].astype(f32) * cxh_ref[...].astype(f32))
    sub = lax.broadcasted_iota(jnp.int32, halo.shape, 0)
    n_halo = halo.shape[0]
    y = cw_ref[2:3, :] * z
    for shift in (1, 2):
        rolled = pltpu.roll(z, shift, 0)
        head = jnp.where(sub < shift, pltpu.roll(halo, shift, 0), rolled[:n_halo])
        y = y + cw_ref[2 - shift:3 - shift, :] * jnp.concatenate([head, rolled[n_halo:]], axis=0)
    y_a = (cb_ref[...].astype(f32) * y).astype(bf)
    za = jnp.dot(y_a, pa_ref[...], preferred_element_type=f32)
    zb = jnp.dot(yb_ref[...], pb_ref[...], preferred_element_type=f32)
    merged = (jax.nn.sigmoid(ga_ref[...].astype(f32)) * za
              + jax.nn.sigmoid(gb_ref[...].astype(f32)) * zb).astype(bf)
    mix = jnp.dot(merged, wo_ref[...], preferred_element_type=f32)
    h0 = _ln_rows(x_ref[...], g0_ref[...], b0_ref[...])
    h1t_ref[...] = _ln_rows(alpha * h0 + mix, g1_ref[...], b1_ref[...]).T


def _mix(x2, proj_x, proj_m, y_b, p_a, p_b, w_o, conv_w, ln0_g, ln0_b, ln1_g, ln1_b,
         *, seq, off_g, alpha, tm):
    t_x, d = x2.shape
    d_conv = conv_w.shape[-1]
    d_mv = y_b.shape[-1]
    n_meta = proj_m.shape[0]
    halo = 8
    assert seq % tm == 0 and tm % halo == 0 and n_meta % halo == 0 and off_g % d == 0
    assert conv_w.shape[0] == 3
    tile = lambda j: pl.BlockSpec((tm, d_conv), lambda i: (i, j))
    prev = lambda j: pl.BlockSpec((halo, d_conv),
                                  lambda i: (jnp.maximum(i * (tm // halo) - 1, 0), j))
    meta = lambda j: pl.BlockSpec((halo, d_conv), lambda i: (n_meta // halo - 1, j))
    const = lambda shape: pl.BlockSpec(shape, lambda i: (0, 0), pipeline_mode=pl.Buffered(1))
    vec = lambda a: a[None, :]
    return pl.pallas_call(
        functools.partial(_mix_kernel, tiles_per_batch=seq // tm, alpha=alpha),
        grid=(t_x // tm,),
        in_specs=[pl.BlockSpec((tm, d), lambda i: (i, 0)),
                  tile(0), tile(1), tile(2), prev(1), prev(2), meta(1), meta(2),
                  pl.BlockSpec((tm, d_mv), lambda i: (i, 0)),
                  pl.BlockSpec((tm, d), lambda i: (i, off_g // d)),
                  pl.BlockSpec((tm, d), lambda i: (i, off_g // d + 1)),
                  const(p_a.shape), const(p_b.shape), const(w_o.shape), const(conv_w.shape),
                  const((1, d)), const((1, d)), const((1, d)), const((1, d))],
        out_specs=pl.BlockSpec((d, tm), lambda i: (0, i)),
        out_shape=jax.ShapeDtypeStruct((d, t_x), jnp.float32),
        compiler_params=pltpu.CompilerParams(
            dimension_semantics=("parallel",), vmem_limit_bytes=VMEM_LIMIT_BYTES),
        name="mix",
    )(x2, proj_x, proj_x, proj_x, proj_x, proj_x, proj_m, proj_m, y_b, proj_x, proj_x,
      p_a, p_b, w_o, conv_w, vec(ln0_g), vec(ln0_b), vec(ln1_g), vec(ln1_b))


def kernel(x, meta_tokens, ln0_g, ln0_b, w_in, b_if, conv_w, mh_norm_w, p_a, p_b, w_o,
           ln1_g, ln1_b, peer_wq, peer_keys, peer_u, peer_v, ln2_g, ln2_b):
    depth = w_in.shape[0]
    assert depth == 1
    bsz, seq, d = x.shape
    n_meta = meta_tokens.shape[0]
    n_in = w_in.shape[-1]
    d_conv = conv_w.shape[-1]
    d_mv = mh_norm_w.shape[-1]
    n_mh = b_if.shape[-1] // 2
    d_mqk = (n_in - 3 * d_conv - 2 * d_mv - 2 * n_mh - 2 * d) // 2
    head_qk = d_mqk // n_mh
    head_v = d_mv // n_mh
    alpha = (2 * depth) ** 0.25
    bf = jnp.bfloat16
    t_x = bsz * seq

    x2 = x.reshape(t_x, d)
    gate_off = 3 * d_conv + 2 * d_mqk + 2 * d_mv
    w = w_in[0]
    w_main = jnp.concatenate([w[:, :gate_off], w[:, gate_off + 2 * n_mh:]], axis=1).astype(bf)
    w_gif = jnp.pad(w[:, gate_off:gate_off + 2 * n_mh], ((0, 0), (0, LANES - 2 * n_mh))).astype(bf)
    inproj_tn = min(INPROJ_TILE_N, w_main.shape[1])
    proj_x, gif_x = _inproj(x2, ln0_g, ln0_b, w_main, w_gif, INPROJ_TILE_M, inproj_tn)
    proj_m, gif_m = _inproj(meta_tokens, ln0_g, ln0_b, w_main, w_gif, INPROJ_TILE_M, inproj_tn)

    off_q = 3 * d_conv
    off_v = off_q + 2 * d_mqk
    off_g = off_v + 2 * d_mv
    y_b = _mlstm(proj_x, proj_m, gif_x, gif_m, b_if[0], mh_norm_w[0], bsz=bsz, seq=seq,
                 n_heads=n_mh, head_qk=head_qk, head_v=head_v, off_q=off_q, off_v=off_v)
    h1_t = _mix(x2, proj_x, proj_m, y_b,
                p_a[0].astype(bf), p_b[0].astype(bf), w_o[0].astype(bf), conv_w[0],
                ln0_g, ln0_b, ln1_g[0], ln1_b[0], seq=seq, off_g=off_g, alpha=alpha,
                tm=MIX_TILE)

    n_ph, _, n_keys, half_key = peer_keys.shape[1:]
    w_fold = _fold_keys(peer_keys[0].reshape(n_ph * 2, n_keys, half_key), peer_wq[0])
    scores_t = _matmul(w_fold, h1_t, jnp.float32, 1024, 512)
    e0, thr, e1, rank = _peer_select(scores_t, n_ph, n_keys, LANES)
    out = _peer_dense(h1_t, e0, thr, e1, rank, peer_u[0].astype(bf), peer_v[0].T.astype(bf),
                      ln2_g[0], ln2_b[0], alpha=alpha, tm=512, rows_per_step=8)
    return out.reshape(bsz, seq, d)
```

```python
import functools
import math

import jax
import jax.numpy as jnp
from jax import lax
from jax.experimental import pallas as pl
from jax.experimental.pallas import tpu as pltpu

TOPK = 16
LN_EPS = 1e-5
I_GATE_PAD = -1e30
MLSTM_CHUNK = 256
MIX_TILE = 256
INPROJ_TILE_M = 1024
INPROJ_TILE_N = 1024
LANES = 128
BF16_SUBLANES = 16
VMEM_LIMIT_BYTES = 56 * 1024 * 1024
PEER_VMEM_LIMIT_BYTES = 58 * 1024 * 1024
FP8 = jnp.float8_e4m3fn
FP8_CLIP = 448.0
FP8_TARGET = 256.0
PEER_P_SCALE = 64.0


def _gelu_exact(x):
    return 0.5 * x * (1.0 + lax.erf(x * (2.0 ** -0.5)))


def _ln_rows(x, g, b):
    mu = jnp.mean(x, axis=1, keepdims=True)
    xc = x - mu
    var = jnp.mean(xc * xc, axis=1, keepdims=True)
    return xc * lax.rsqrt(var + LN_EPS) * g + b


def _inproj_kernel(x_ref, g_ref, b_ref, w_ref, wg_ref, o_ref, og_ref, h_ref):
    @pl.when(pl.program_id(1) == 0)
    def _():
        h = _ln_rows(x_ref[...], g_ref[...], b_ref[...]).astype(h_ref.dtype)
        h_ref[...] = h
        og_ref[...] = jnp.dot(h, wg_ref[...], preferred_element_type=jnp.float32)

    o_ref[...] = jnp.dot(h_ref[...], w_ref[...],
                         preferred_element_type=jnp.float32).astype(o_ref.dtype)


def _inproj(x2, ln_g, ln_b, w_main, w_gate, tm, tn):
    m, d = x2.shape
    n = w_main.shape[1]
    tm = min(tm, m)
    assert m % tm == 0 and n % tn == 0 and w_gate.shape[1] == LANES
    const = lambda shape: pl.BlockSpec(shape, lambda i, j: (0, 0))
    return pl.pallas_call(
        _inproj_kernel,
        grid=(m // tm, n // tn),
        in_specs=[pl.BlockSpec((tm, d), lambda i, j: (i, 0)),
                  const((1, d)), const((1, d)),
                  pl.BlockSpec((d, tn), lambda i, j: (0, j)),
                  const((d, LANES))],
        out_specs=[pl.BlockSpec((tm, tn), lambda i, j: (i, j)),
                   pl.BlockSpec((tm, LANES), lambda i, j: (i, 0))],
        out_shape=[jax.ShapeDtypeStruct((m, n), jnp.bfloat16),
                   jax.ShapeDtypeStruct((m, LANES), jnp.float32)],
        scratch_shapes=[pltpu.VMEM((tm, d), jnp.bfloat16)],
        compiler_params=pltpu.CompilerParams(
            dimension_semantics=("parallel", "arbitrary"),
            vmem_limit_bytes=VMEM_LIMIT_BYTES),
        name="inproj",
    )(x2, ln_g[None, :], ln_b[None, :], w_main, w_gate)


def _mm_kernel(a_ref, b_ref, o_ref):
    bf = jnp.bfloat16
    o_ref[...] = jnp.dot(a_ref[...].astype(bf), b_ref[...].astype(bf),
                         preferred_element_type=jnp.float32).astype(o_ref.dtype)


def _matmul(a, b, out_dtype, tm, tn):
    m, k = a.shape
    _, n = b.shape
    tm = min(tm, m)
    tn = min(tn, n)
    assert m % tm == 0 and n % tn == 0
    return pl.pallas_call(
        _mm_kernel,
        grid=(m // tm, n // tn),
        in_specs=[pl.BlockSpec((tm, k), lambda i, j: (i, 0)),
                  pl.BlockSpec((k, tn), lambda i, j: (0, j))],
        out_specs=pl.BlockSpec((tm, tn), lambda i, j: (i, j)),
        out_shape=jax.ShapeDtypeStruct((m, n), out_dtype),
        compiler_params=pltpu.CompilerParams(
            dimension_semantics=("parallel", "parallel"),
            vmem_limit_bytes=VMEM_LIMIT_BYTES),
        name="matmul",
    )(a, b)


def _pow2_scale(amax):
    safe = jnp.where(amax > 0, amax, FP8_TARGET)
    return jnp.exp2(jnp.floor(jnp.log2(FP8_TARGET / safe)))


def _peer_dense_kernel(sc_ref, ht_ref, e0_ref, thr_ref, e1_ref, rank_ref, u_ref, vt_ref,
                       g_ref, b_ref, o_ref, hb_ref, sh_ref, acc_ref, w_ref, p_ref,
                       *, n_heads, rows_per_step, alpha):
    k = pl.program_id(1)
    bf, f8 = jnp.bfloat16, FP8
    tm = ht_ref.shape[1]
    zero = jnp.zeros((), bf)

    def row_tile(ref, h, ii):
        row = jnp.broadcast_to(ref[h, ii:ii + 1, :], (BF16_SUBLANES, tm)).astype(bf)
        return pltpu.repeat(row, LANES // BF16_SUBLANES, axis=0)

    @pl.when(k == 0)
    def _():
        acc_ref[...] = jnp.zeros_like(acc_ref)
        h = ht_ref[...]
        amax = jnp.max(jnp.max(jnp.abs(h), axis=0, keepdims=True), axis=1, keepdims=True)
        s_h = _pow2_scale(amax)
        hb_ref[...] = (h * s_h).astype(f8)
        sh_ref[...] = jnp.broadcast_to(1.0 / s_h, sh_ref.shape)

    for ii in range(rows_per_step):
        w = jnp.zeros((LANES, tm), bf)
        for h in range(n_heads):
            sel = rank_ref[h] < row_tile(thr_ref, h, ii)
            w = w + jnp.where(sel, e1_ref[h], zero) * row_tile(e0_ref, h, ii)
        w_ref[ii * LANES:(ii + 1) * LANES, :] = w
    a = jnp.dot(u_ref[...], hb_ref[...], preferred_element_type=jnp.float32)
    x = a * (sc_ref[0] * sh_ref[0:1, 0:1])
    act = ((0.5 * PEER_P_SCALE) * x * (1.0 + lax.erf(x * (2.0 ** -0.5)))).astype(bf)
    p = jnp.clip(w_ref[...] * act, -FP8_CLIP, FP8_CLIP)
    p_ref[...] = p.astype(f8)
    acc_ref[...] += jnp.dot(vt_ref[...], p_ref[...], preferred_element_type=jnp.float32)

    @pl.when(k == pl.num_programs(1) - 1)
    def _():
        y = alpha * ht_ref[...] + acc_ref[...] * sc_ref[1]
        mu = jnp.mean(y, axis=0, keepdims=True)
        yc = y - mu
        var = jnp.mean(yc * yc, axis=0, keepdims=True)
        o_ref[...] = (yc * lax.rsqrt(var + LN_EPS)).T * g_ref[...] + b_ref[...]


def _peer_dense(ht, e0, thr, e1, rank, peer_u, peer_v, ln_g, ln_b, *, alpha, tm, rows_per_step):
    s_u = _pow2_scale(jnp.max(jnp.abs(peer_u)))
    s_v = _pow2_scale(jnp.max(jnp.abs(peer_v)))
    u = (peer_u * s_u).astype(FP8)
    vt = (peer_v.T * s_v).astype(FP8)
    scales = jnp.stack([1.0 / s_u, 1.0 / (s_v * PEER_P_SCALE)]).astype(jnp.float32)
    d, t = ht.shape
    n_heads, n_keys, _ = e0.shape
    assert n_keys == LANES
    n_experts = u.shape[0]
    eb = rows_per_step * LANES
    sel_spec = pl.BlockSpec((n_heads, n_keys, tm), lambda i, k: (0, 0, i))
    row_spec = pl.BlockSpec((n_heads, rows_per_step, tm), lambda i, k: (0, k, i))
    vec_spec = pl.BlockSpec((1, d), lambda i, k: (0, 0))
    return pl.pallas_call(
        functools.partial(_peer_dense_kernel, n_heads=n_heads, rows_per_step=rows_per_step,
                          alpha=alpha),
        grid=(t // tm, n_experts // eb),
        in_specs=[pl.BlockSpec(memory_space=pltpu.SMEM),
                  pl.BlockSpec((d, tm), lambda i, k: (0, i)),
                  row_spec, row_spec, sel_spec, sel_spec,
                  pl.BlockSpec((eb, d), lambda i, k: (k, 0)),
                  pl.BlockSpec((d, eb), lambda i, k: (0, k)),
                  vec_spec, vec_spec],
        out_specs=pl.BlockSpec((tm, d), lambda i, k: (i, 0)),
        out_shape=jax.ShapeDtypeStruct((t, d), jnp.float32),
        scratch_shapes=[pltpu.VMEM((d, tm), FP8),
                        pltpu.VMEM((8, LANES), jnp.float32),
                        pltpu.VMEM((d, tm), jnp.float32),
                        pltpu.VMEM((eb, tm), jnp.bfloat16),
                        pltpu.VMEM((eb, tm), FP8)],
        compiler_params=pltpu.CompilerParams(
            dimension_semantics=("parallel", "arbitrary"),
            vmem_limit_bytes=PEER_VMEM_LIMIT_BYTES),
        name="peer_dense",
    )(scales, ht, e0, thr, e1, rank, u, vt, ln_g[None, :], ln_b[None, :])


def _fold_keys_kernel(keys_ref, wq_ref, o_ref):
    o_ref[...] = lax.dot_general(
        keys_ref[0], wq_ref[...], (((1,), (1,)), ((), ())),
        precision=lax.Precision.HIGHEST, preferred_element_type=jnp.float32)


def _fold_keys(keys, wq):
    n_blocks, n_keys, half_key = keys.shape
    d = wq.shape[0]
    return pl.pallas_call(
        _fold_keys_kernel,
        grid=(n_blocks,),
        in_specs=[pl.BlockSpec((1, n_keys, half_key), lambda b: (b, 0, 0)),
                  pl.BlockSpec((d, half_key), lambda b: (0, b))],
        out_specs=pl.BlockSpec((n_keys, d), lambda b: (b, 0)),
        out_shape=jax.ShapeDtypeStruct((n_blocks * n_keys, d), jnp.float32),
        compiler_params=pltpu.CompilerParams(dimension_semantics=("parallel",)),
        name="fold_keys",
    )(keys, wq)


def _extract_top(work, n, row_ref, rank=None):
    for r in range(n):
        m = jnp.max(work, axis=0, keepdims=True)
        row_ref[r:r + 1, :] = m
        hit = work == m
        if rank is not None:
            rank = jnp.where(hit, float(r), rank)
        work = jnp.where(hit, -jnp.inf, work)
    return rank


def _peer_select_kernel(sc_ref, e0_ref, thr_ref, e1_ref, rank_ref, a_ref, b_ref, c_ref,
                        *, n_heads, n_keys):
    tl = sc_ref.shape[1]
    half = TOPK // 2

    def head_body(h, carry):
        r0 = pl.multiple_of(h * 2 * n_keys, 2 * n_keys)
        s0 = sc_ref[pl.ds(r0, n_keys), :]
        s1 = sc_ref[pl.ds(r0 + n_keys, n_keys), :]
        _extract_top(s0, TOPK, a_ref)
        rank1 = _extract_top(s1, TOPK, b_ref, jnp.full((n_keys, tl), 100.0, jnp.float32))
        pieces = [a_ref[0:1, :] + b_ref[...]]
        pieces += [a_ref[p:p + 1, :] + b_ref[0:half, :] for p in range(1, half)]
        pieces += [a_ref[half:TOPK, :] + b_ref[0:1, :]]
        _extract_top(jnp.concatenate(pieces, axis=0), TOPK, c_ref)
        best = c_ref[...]
        tau = best[TOPK - 1:TOPK, :]
        zsum = jnp.sum(jnp.exp(best - best[0:1, :]), axis=0, keepdims=True)
        b_all = b_ref[...]
        thr = jnp.zeros((n_keys, tl), jnp.float32)
        for p in range(TOPK):
            a_p = a_ref[p:p + 1, :]
            n_sel = jnp.sum(jnp.where(a_p + b_all >= tau, 1.0, 0.0), axis=0, keepdims=True)
            thr = jnp.where(s0 == a_p, n_sel, thr)
        e0_ref[h] = jnp.exp(s0 - a_ref[0:1, :]) / zsum
        thr_ref[h] = thr
        e1_ref[h] = jnp.exp(s1 - b_ref[0:1, :]).astype(e1_ref.dtype)
        rank_ref[h] = rank1.astype(rank_ref.dtype)
        return carry

    lax.fori_loop(0, n_heads, head_body, 0)


def _peer_select(scores_t, n_heads, n_keys, tl):
    rows, t = scores_t.shape
    assert rows == n_heads * 2 * n_keys
    out_spec = pl.BlockSpec((n_heads, n_keys, tl), lambda i: (0, 0, i))
    f32 = jax.ShapeDtypeStruct((n_heads, n_keys, t), jnp.float32)
    b16 = jax.ShapeDtypeStruct((n_heads, n_keys, t), jnp.bfloat16)
    return pl.pallas_call(
        functools.partial(_peer_select_kernel, n_heads=n_heads, n_keys=n_keys),
        grid=(t // tl,),
        in_specs=[pl.BlockSpec((rows, tl), lambda i: (0, i))],
        out_specs=[out_spec, out_spec, out_spec, out_spec],
        out_shape=[f32, f32, b16, b16],
        scratch_shapes=[pltpu.VMEM((TOPK, tl), jnp.float32)] * 3,
        compiler_params=pltpu.CompilerParams(
            dimension_semantics=("parallel",), vmem_limit_bytes=VMEM_LIMIT_BYTES),
        name="peer_select",
    )(scores_t)


def _log_sigmoid(x):
    return jnp.minimum(x, 0.0) - jnp.log1p(jnp.exp(-jnp.abs(x)))


def _mlstm_kernel(q_ref, k_ref, v_ref, o_ref, g_ref, gt_ref,
                  qm_ref, km_ref, vm_ref, om_ref, gm_ref, gmt_ref,
                  bias_ref, biast_ref, nw_ref, y_ref, c_ref, m_ref,
                  *, n_heads, head_qk, head_v):
    c = pl.program_id(1)
    tc = q_ref.shape[0]
    f32, bf = jnp.float32, jnp.bfloat16
    first = c == 0

    @pl.when(first)
    def _():
        c_ref[...] = jnp.zeros_like(c_ref)
        m_ref[...] = jnp.zeros_like(m_ref)

    q = jnp.where(first, qm_ref[...], q_ref[...])
    k = jnp.where(first, km_ref[...], k_ref[...])
    v = jnp.where(first, vm_ref[...], v_ref[...])
    o = jnp.where(first, om_ref[...], o_ref[...])
    g = jnp.where(first, gm_ref[...], g_ref[...]) + bias_ref[...]
    gt = jnp.where(first, gmt_ref[...], gt_ref[...]) + biast_ref[...]

    row = lax.broadcasted_iota(jnp.int32, (tc, tc), 0)
    col = lax.broadcasted_iota(jnp.int32, (tc, tc), 1)
    causal = col <= row
    tri = jnp.where(causal, 1.0, 0.0).astype(f32)
    b_cols = jnp.dot(tri, _log_sigmoid(g), precision=lax.Precision.HIGHEST,
                     preferred_element_type=f32)
    b_rows = lax.dot_general(_log_sigmoid(gt), tri, (((1,), (1,)), ((), ())),
                             precision=lax.Precision.HIGHEST, preferred_element_type=f32)
    ln_scale = -0.5 * math.log(head_qk)
    ones_blk = jnp.where(lax.broadcasted_iota(jnp.int32, (tc, LANES), 1) == 0, 1.0, 0.0).astype(bf)

    for h in range(n_heads):
        qh = q[:, h * head_qk:(h + 1) * head_qk]
        kh = k[:, h * head_qk:(h + 1) * head_qk]
        v_ext = jnp.concatenate([v[:, h * head_v:(h + 1) * head_v], ones_blk], axis=1)
        li_col = g[:, h:h + 1]
        b_col = b_cols[:, n_heads + h:n_heads + h + 1]
        li_row = gt[h:h + 1, :]
        b_row = b_rows[n_heads + h:n_heads + h + 1, :]
        m_prev = m_ref[h:h + 1, 0:1]
        d = jnp.where(causal, b_col + (li_row - b_row), -jnp.inf)
        inter = b_col + m_prev
        m_t = jnp.maximum(inter, jnp.max(d, axis=1, keepdims=True))
        w = jnp.exp(d - (m_t - ln_scale))
        s = lax.dot_general(qh, kh, (((1,), (1,)), ((), ())), preferred_element_type=f32) * w
        sc_inter = jnp.exp(inter - (m_t - ln_scale))
        c_ext = c_ref[h]
        num_ext = (jnp.dot(s.astype(bf), v_ext, preferred_element_type=f32)
                   + sc_inter * jnp.dot(qh, c_ext.astype(bf), preferred_element_type=f32))
        den = num_ext[:, head_v:head_v + 1]
        hm = num_ext[:, :head_v] / jnp.maximum(jnp.abs(den), jnp.exp(-m_t))
        hm = hm * lax.rsqrt(jnp.mean(hm * hm, axis=1, keepdims=True) + LN_EPS)
        gate_o = jax.nn.sigmoid(o[:, h * head_v:(h + 1) * head_v].astype(f32))
        y_ref[:, h * head_v:(h + 1) * head_v] = (
            gate_o * (hm * nw_ref[:, h * head_v:(h + 1) * head_v])).astype(y_ref.dtype)
        b_last = b_col[tc - 1:tc, :]
        dl = b_last - b_col + li_col
        m_new = jnp.maximum(b_last + m_prev, jnp.max(dl, axis=0, keepdims=True))
        wv = (jnp.exp(dl - m_new) * v_ext.astype(f32)).astype(bf)
        c_ref[h] = (jnp.exp(b_last + m_prev - m_new) * c_ext
                    + lax.dot_general(kh, wv, (((0,), (0,)), ((), ())),
                                      preferred_element_type=f32))
        m_ref[h:h + 1, :] = jnp.broadcast_to(m_new, (1, LANES))


def _mlstm(proj_x, proj_m, gif_x, gif_m, b_if, norm_w, *, bsz, seq, n_heads, head_qk, head_v,
           off_q, off_v):
    tc = min(MLSTM_CHUNK, seq)
    assert seq % tc == 0
    n_meta = proj_m.shape[0]
    d_mqk, d_mv = n_heads * head_qk, n_heads * head_v
    assert off_q % d_mqk == 0 and off_v % d_mv == 0 and n_meta <= tc
    nc_x = seq // tc
    bf = jnp.bfloat16
    lead = tc - n_meta
    pad_rows = lambda a: jnp.pad(a, ((lead, 0), (0, 0)))
    qm = pad_rows(proj_m[:, off_q:off_q + d_mqk])
    km = pad_rows(proj_m[:, off_q + d_mqk:off_q + 2 * d_mqk])
    vm = pad_rows(proj_m[:, off_v:off_v + d_mv])
    om = pad_rows(proj_m[:, off_v + d_mv:off_v + 2 * d_mv])
    pad_gate = jnp.concatenate([jnp.full((lead, n_heads), I_GATE_PAD, jnp.float32),
                                jnp.full((lead, n_heads), -I_GATE_PAD, jnp.float32),
                                jnp.zeros((lead, LANES - 2 * n_heads), jnp.float32)], axis=1)
    gm = jnp.concatenate([pad_gate, gif_m], axis=0)
    bias = jnp.pad(b_if, (0, LANES - 2 * n_heads))[None, :]
    biast = jnp.broadcast_to(b_if[:, None], (2 * n_heads, tc))
    gt_x = gif_x[:, :2 * n_heads].T
    gt_m = gm[:, :2 * n_heads].T

    xrow = lambda b, c: b * nc_x + jnp.maximum(c - 1, 0)
    const = lambda b, c: (0, 0)
    return pl.pallas_call(
        functools.partial(_mlstm_kernel, n_heads=n_heads, head_qk=head_qk, head_v=head_v),
        grid=(bsz, nc_x + 1),
        in_specs=[pl.BlockSpec((tc, d_mqk), lambda b, c: (xrow(b, c), off_q // d_mqk)),
                  pl.BlockSpec((tc, d_mqk), lambda b, c: (xrow(b, c), off_q // d_mqk + 1)),
                  pl.BlockSpec((tc, d_mv), lambda b, c: (xrow(b, c), off_v // d_mv)),
                  pl.BlockSpec((tc, d_mv), lambda b, c: (xrow(b, c), off_v // d_mv + 1)),
                  pl.BlockSpec((tc, LANES), lambda b, c: (xrow(b, c), 0)),
                  pl.BlockSpec((2 * n_heads, tc), lambda b, c: (0, xrow(b, c))),
                  pl.BlockSpec((tc, d_mqk), const), pl.BlockSpec((tc, d_mqk), const),
                  pl.BlockSpec((tc, d_mv), const), pl.BlockSpec((tc, d_mv), const),
                  pl.BlockSpec((tc, LANES), const), pl.BlockSpec((2 * n_heads, tc), const),
                  pl.BlockSpec((1, LANES), const), pl.BlockSpec((2 * n_heads, tc), const),
                  pl.BlockSpec((1, d_mv), const)],
        out_specs=pl.BlockSpec((tc, d_mv), lambda b, c: (xrow(b, c), 0)),
        out_shape=jax.ShapeDtypeStruct((bsz * seq, d_mv), bf),
        scratch_shapes=[pltpu.VMEM((n_heads, head_qk, head_v + LANES), jnp.float32),
                        pltpu.VMEM((n_heads, LANES), jnp.float32)],
        compiler_params=pltpu.CompilerParams(
            dimension_semantics=("parallel", "arbitrary"),
            vmem_limit_bytes=VMEM_LIMIT_BYTES),
        name="mlstm",
    )(proj_x, proj_x, proj_x, proj_x, gif_x, gt_x, qm, km, vm, om, gm, gt_m,
      bias, biast, norm_w[None, :])


def _mix_kernel(x_ref, cb_ref, cc_ref, cx_ref, cch_ref, cxh_ref, ccm_ref, cxm_ref,
                yb_ref, ga_ref, gb_ref, pa_ref, pb_ref, wo_ref, cw_ref,
                g0_ref, b0_ref, g1_ref, b1_ref, h1t_ref,
                *, tiles_per_batch, alpha):
    i = pl.program_id(0)
    f32, bf = jnp.float32, jnp.bfloat16
    tm = x_ref.shape[0]
    z = cc_ref[...].astype(f32) * cx_ref[...].astype(f32)
    batch_start = (i % tiles_per_batch) == 0
    halo = jnp.where(batch_start,
                     ccm_ref[...].astype(f32) * cxm_ref[...].astype(f32),
                     cch_ref[...].astype(f32) * cxh_ref[...].astype(f32))
    sub = lax.broadcasted_iota(jnp.int32, halo.shape, 0)
    n_halo = halo.shape[0]
    y = cw_ref[2:3, :] * z
    for shift in (1, 2):
        rolled = pltpu.roll(z, shift, 0)
        head = jnp.where(sub < shift, pltpu.roll(halo, shift, 0), rolled[:n_halo])
        y = y + cw_ref[2 - shift:3 - shift, :] * jnp.concatenate([head, rolled[n_halo:]], axis=0)
    y_a = (cb_ref[...].astype(f32) * y).astype(bf)
    za = jnp.dot(y_a, pa_ref[...], preferred_element_type=f32)
    zb = jnp.dot(yb_ref[...], pb_ref[...], preferred_element_type=f32)
    merged = (jax.nn.sigmoid(ga_ref[...].astype(f32)) * za
              + jax.nn.sigmoid(gb_ref[...].astype(f32)) * zb).astype(bf)
    mix = jnp.dot(merged, wo_ref[...], preferred_element_type=f32)
    h0 = _ln_rows(x_ref[...], g0_ref[...], b0_ref[...])
    h1t_ref[...] = _ln_rows(alpha * h0 + mix, g1_ref[...], b1_ref[...]).T


def _mix(x2, proj_x, proj_m, y_b, p_a, p_b, w_o, conv_w, ln0_g, ln0_b, ln1_g, ln1_b,
         *, seq, off_g, alpha, tm):
    t_x, d = x2.shape
    d_conv = conv_w.shape[-1]
    d_mv = y_b.shape[-1]
    n_meta = proj_m.shape[0]
    halo = 8
    assert seq % tm == 0 and tm % halo == 0 and n_meta % halo == 0 and off_g % d == 0
    assert conv_w.shape[0] == 3
    tile = lambda j: pl.BlockSpec((tm, d_conv), lambda i: (i, j))
    prev = lambda j: pl.BlockSpec((halo, d_conv),
                                  lambda i: (jnp.maximum(i * (tm // halo) - 1, 0), j))
    meta = lambda j: pl.BlockSpec((halo, d_conv), lambda i: (n_meta // halo - 1, j))
    const = lambda shape: pl.BlockSpec(shape, lambda i: (0, 0), pipeline_mode=pl.Buffered(1))
    vec = lambda a: a[None, :]
    return pl.pallas_call(
        functools.partial(_mix_kernel, tiles_per_batch=seq // tm, alpha=alpha),
        grid=(t_x // tm,),
        in_specs=[pl.BlockSpec((tm, d), lambda i: (i, 0)),
                  tile(0), tile(1), tile(2), prev(1), prev(2), meta(1), meta(2),
                  pl.BlockSpec((tm, d_mv), lambda i: (i, 0)),
                  pl.BlockSpec((tm, d), lambda i: (i, off_g // d)),
                  pl.BlockSpec((tm, d), lambda i: (i, off_g // d + 1)),
                  const(p_a.shape), const(p_b.shape), const(w_o.shape), const(conv_w.shape),
                  const((1, d)), const((1, d)), const((1, d)), const((1, d))],
        out_specs=pl.BlockSpec((d, tm), lambda i: (0, i)),
        out_shape=jax.ShapeDtypeStruct((d, t_x), jnp.float32),
        compiler_params=pltpu.CompilerParams(
            dimension_semantics=("parallel",), vmem_limit_bytes=VMEM_LIMIT_BYTES),
        name="mix",
    )(x2, proj_x, proj_x, proj_x, proj_x, proj_x, proj_m, proj_m, y_b, proj_x, proj_x,
      p_a, p_b, w_o, conv_w, vec(ln0_g), vec(ln0_b), vec(ln1_g), vec(ln1_b))


def kernel(x, meta_tokens, ln0_g, ln0_b, w_in, b_if, conv_w, mh_norm_w, p_a, p_b, w_o,
           ln1_g, ln1_b, peer_wq, peer_keys, peer_u, peer_v, ln2_g, ln2_b):
    depth = w_in.shape[0]
    assert depth == 1
    bsz, seq, d = x.shape
    n_meta = meta_tokens.shape[0]
    n_in = w_in.shape[-1]
    d_conv = conv_w.shape[-1]
    d_mv = mh_norm_w.shape[-1]
    n_mh = b_if.shape[-1] // 2
    d_mqk = (n_in - 3 * d_conv - 2 * d_mv - 2 * n_mh - 2 * d) // 2
    head_qk = d_mqk // n_mh
    head_v = d_mv // n_mh
    alpha = (2 * depth) ** 0.25
    bf = jnp.bfloat16
    t_x = bsz * seq

    x2 = x.reshape(t_x, d)
    gate_off = 3 * d_conv + 2 * d_mqk + 2 * d_mv
    w = w_in[0]
    w_main = jnp.concatenate([w[:, :gate_off], w[:, gate_off + 2 * n_mh:]], axis=1).astype(bf)
    w_gif = jnp.pad(w[:, gate_off:gate_off + 2 * n_mh], ((0, 0), (0, LANES - 2 * n_mh))).astype(bf)
    inproj_tn = min(INPROJ_TILE_N, w_main.shape[1])
    proj_x, gif_x = _inproj(x2, ln0_g, ln0_b, w_main, w_gif, INPROJ_TILE_M, inproj_tn)
    proj_m, gif_m = _inproj(meta_tokens, ln0_g, ln0_b, w_main, w_gif, INPROJ_TILE_M, inproj_tn)

    off_q = 3 * d_conv
    off_v = off_q + 2 * d_mqk
    off_g = off_v + 2 * d_mv
    y_b = _mlstm(proj_x, proj_m, gif_x, gif_m, b_if[0], mh_norm_w[0], bsz=bsz, seq=seq,
                 n_heads=n_mh, head_qk=head_qk, head_v=head_v, off_q=off_q, off_v=off_v)
    h1_t = _mix(x2, proj_x, proj_m, y_b,
                p_a[0].astype(bf), p_b[0].astype(bf), w_o[0].astype(bf), conv_w[0],
                ln0_g, ln0_b, ln1_g[0], ln1_b[0], seq=seq, off_g=off_g, alpha=alpha,
                tm=MIX_TILE)

    n_ph, _, n_keys, half_key = peer_keys.shape[1:]
    w_fold = _fold_keys(peer_keys[0].reshape(n_ph * 2, n_keys, half_key), peer_wq[0])
    scores_t = _matmul(w_fold, h1_t, jnp.float32, 1024, 512)
    e0, thr, e1, rank = _peer_select(scores_t, n_ph, n_keys, LANES)
    out = _peer_dense(h1_t, e0, thr, e1, rank, peer_u[0], peer_v[0],
                      ln2_g[0], ln2_b[0], alpha=alpha, tm=512, rows_per_step=8)
    return out.reshape(bsz, seq, d)
```

```python
import functools
import math

import jax
import jax.numpy as jnp
from jax import lax
from jax.experimental import pallas as pl
from jax.experimental.pallas import tpu as pltpu

TOPK = 16
LN_EPS = 1e-5
I_GATE_PAD = -1e30
MLSTM_CHUNK = 256
MIX_TILE = 256
INPROJ_TILE_M = 1024
INPROJ_TILE_N = 1024
PEER_TOKEN_TILE = 512
PEER_ROWS_PER_STEP = 8
QUANT_ROWS = 512
LANES = 128
BF16_SUBLANES = 16
VMEM_LIMIT_BYTES = 56 * 1024 * 1024
PEER_VMEM_LIMIT_BYTES = 58 * 1024 * 1024
FP8 = jnp.float8_e4m3fn
FP8_CLIP = 448.0
FP8_TARGET = 256.0
PEER_P_SCALE = 64.0


def _gelu_exact(x):
    return 0.5 * x * (1.0 + lax.erf(x * (2.0 ** -0.5)))


def _ln_rows(x, g, b):
    mu = jnp.mean(x, axis=1, keepdims=True)
    xc = x - mu
    var = jnp.mean(xc * xc, axis=1, keepdims=True)
    return xc * lax.rsqrt(var + LN_EPS) * g + b


def _inproj_kernel(x_ref, g_ref, b_ref, w_ref, wg_ref, o_ref, og_ref, h_ref):
    @pl.when(pl.program_id(1) == 0)
    def _():
        h = _ln_rows(x_ref[...], g_ref[...], b_ref[...]).astype(h_ref.dtype)
        h_ref[...] = h
        og_ref[...] = jnp.dot(h, wg_ref[...], preferred_element_type=jnp.float32)

    o_ref[...] = jnp.dot(h_ref[...], w_ref[...],
                         preferred_element_type=jnp.float32).astype(o_ref.dtype)


def _inproj(x2, ln_g, ln_b, w_main, w_gate, tm, tn):
    m, d = x2.shape
    n = w_main.shape[1]
    tm = min(tm, m)
    assert m % tm == 0 and n % tn == 0 and w_gate.shape[1] == LANES
    const = lambda shape: pl.BlockSpec(shape, lambda i, j: (0, 0))
    return pl.pallas_call(
        _inproj_kernel,
        grid=(m // tm, n // tn),
        in_specs=[pl.BlockSpec((tm, d), lambda i, j: (i, 0)),
                  const((1, d)), const((1, d)),
                  pl.BlockSpec((d, tn), lambda i, j: (0, j)),
                  const((d, LANES))],
        out_specs=[pl.BlockSpec((tm, tn), lambda i, j: (i, j)),
                   pl.BlockSpec((tm, LANES), lambda i, j: (i, 0))],
        out_shape=[jax.ShapeDtypeStruct((m, n), jnp.bfloat16),
                   jax.ShapeDtypeStruct((m, LANES), jnp.float32)],
        scratch_shapes=[pltpu.VMEM((tm, d), jnp.bfloat16)],
        compiler_params=pltpu.CompilerParams(
            dimension_semantics=("parallel", "arbitrary"),
            vmem_limit_bytes=VMEM_LIMIT_BYTES),
        name="inproj",
    )(x2, ln_g[None, :], ln_b[None, :], w_main, w_gate)


def _mm_kernel(a_ref, b_ref, o_ref):
    bf = jnp.bfloat16
    o_ref[...] = jnp.dot(a_ref[...].astype(bf), b_ref[...].astype(bf),
                         preferred_element_type=jnp.float32).astype(o_ref.dtype)


def _matmul(a, b, out_dtype, tm, tn):
    m, k = a.shape
    _, n = b.shape
    tm = min(tm, m)
    tn = min(tn, n)
    assert m % tm == 0 and n % tn == 0
    return pl.pallas_call(
        _mm_kernel,
        grid=(m // tm, n // tn),
        in_specs=[pl.BlockSpec((tm, k), lambda i, j: (i, 0)),
                  pl.BlockSpec((k, tn), lambda i, j: (0, j))],
        out_specs=pl.BlockSpec((tm, tn), lambda i, j: (i, j)),
        out_shape=jax.ShapeDtypeStruct((m, n), out_dtype),
        compiler_params=pltpu.CompilerParams(
            dimension_semantics=("parallel", "parallel"),
            vmem_limit_bytes=VMEM_LIMIT_BYTES),
        name="matmul",
    )(a, b)


def _pow2_scale(amax):
    safe = jnp.where(amax > 0, amax, FP8_TARGET)
    return jnp.exp2(jnp.floor(jnp.log2(FP8_TARGET / safe)))


def _quantize_rows_kernel(x_ref, q_ref, inv_ref, *, transpose):
    x = x_ref[...]
    scale = _pow2_scale(jnp.max(jnp.abs(x), axis=1, keepdims=True))
    xs = x * scale
    q_ref[...] = (xs.T if transpose else xs).astype(q_ref.dtype)
    inv_ref[...] = jnp.broadcast_to(1.0 / scale, inv_ref.shape)


def _quantize_rows(x, rows, transpose):
    n, d = x.shape
    rows = min(rows, n)
    assert n % rows == 0
    q_shape, q_spec = (((d, n), pl.BlockSpec((d, rows), lambda i: (0, i))) if transpose
                       else ((n, d), pl.BlockSpec((rows, d), lambda i: (i, 0))))
    return pl.pallas_call(
        functools.partial(_quantize_rows_kernel, transpose=transpose),
        grid=(n // rows,),
        in_specs=[pl.BlockSpec((rows, d), lambda i: (i, 0))],
        out_specs=[q_spec, pl.BlockSpec((rows, LANES), lambda i: (i, 0))],
        out_shape=[jax.ShapeDtypeStruct(q_shape, FP8),
                   jax.ShapeDtypeStruct((n, LANES), jnp.float32)],
        compiler_params=pltpu.CompilerParams(
            dimension_semantics=("parallel",), vmem_limit_bytes=VMEM_LIMIT_BYTES),
        name="quantize_rows",
    )(x)


def _peer_dense_kernel(ht_ref, e0_ref, thr_ref, e1_ref, rank_ref, u_ref, vt_ref, c1_ref, c2_ref,
                       post_ref, g_ref, b_ref, o_ref, hb_ref, sh_ref, acc_ref, w_ref, p_ref,
                       *, n_heads, rows_per_step, alpha):
    k = pl.program_id(1)
    bf, f8 = jnp.bfloat16, FP8
    tm = ht_ref.shape[1]
    zero = jnp.zeros((), bf)

    def row_tile(ref, h, ii):
        row = jnp.broadcast_to(ref[h, ii:ii + 1, :], (BF16_SUBLANES, tm)).astype(bf)
        return pltpu.repeat(row, LANES // BF16_SUBLANES, axis=0)

    @pl.when(k == 0)
    def _():
        acc_ref[...] = jnp.zeros_like(acc_ref)
        h = ht_ref[...]
        amax = jnp.max(jnp.max(jnp.abs(h), axis=0, keepdims=True), axis=1, keepdims=True)
        s_h = _pow2_scale(amax)
        hb_ref[...] = (h * s_h).astype(f8)
        sh_ref[...] = jnp.broadcast_to(1.0 / s_h, sh_ref.shape)

    for ii in range(rows_per_step):
        w = jnp.zeros((LANES, tm), bf)
        for h in range(n_heads):
            sel = rank_ref[h] < row_tile(thr_ref, h, ii)
            w = w + jnp.where(sel, e1_ref[h], zero) * row_tile(e0_ref, h, ii)
        w_ref[ii * LANES:(ii + 1) * LANES, :] = w
    a = jnp.dot(u_ref[...], hb_ref[...], preferred_element_type=jnp.float32)
    inv_sh = sh_ref[0:1, 0:1]
    c1 = pltpu.repeat(c1_ref[...] * inv_sh, tm // LANES, axis=1)
    c2 = pltpu.repeat(c2_ref[...] * inv_sh, tm // LANES, axis=1)
    act = ((a * c1) * (1.0 + lax.erf(a * c2))).astype(bf)
    p = jnp.clip(w_ref[...] * act, -FP8_CLIP, FP8_CLIP)
    p_ref[...] = p.astype(f8)
    acc_ref[...] += jnp.dot(vt_ref[...], p_ref[...], preferred_element_type=jnp.float32)

    @pl.when(k == pl.num_programs(1) - 1)
    def _():
        y = alpha * ht_ref[...] + acc_ref[...] * post_ref[0:1, 0:1]
        mu = jnp.mean(y, axis=0, keepdims=True)
        yc = y - mu
        var = jnp.mean(yc * yc, axis=0, keepdims=True)
        o_ref[...] = (yc * lax.rsqrt(var + LN_EPS)).T * g_ref[...] + b_ref[...]


def _peer_dense(ht, e0, thr, e1, rank, peer_u, peer_v, ln_g, ln_b, *, alpha, tm, rows_per_step):
    u, inv_u = _quantize_rows(peer_u, QUANT_ROWS, transpose=False)
    vt, inv_v = _quantize_rows(peer_v, QUANT_ROWS, transpose=True)
    inv_v_max = jnp.max(inv_v)
    c1 = inv_u * (inv_v / inv_v_max) * (0.5 * PEER_P_SCALE)
    c2 = inv_u * (2.0 ** -0.5)
    post = jnp.broadcast_to(inv_v_max / PEER_P_SCALE, (1, LANES)).astype(jnp.float32)
    d, t = ht.shape
    n_heads, n_keys, _ = e0.shape
    assert n_keys == LANES
    n_experts = u.shape[0]
    eb = rows_per_step * LANES
    sel_spec = pl.BlockSpec((n_heads, n_keys, tm), lambda i, k: (0, 0, i))
    row_spec = pl.BlockSpec((n_heads, rows_per_step, tm), lambda i, k: (0, k, i))
    vec_spec = pl.BlockSpec((1, d), lambda i, k: (0, 0))
    return pl.pallas_call(
        functools.partial(_peer_dense_kernel, n_heads=n_heads, rows_per_step=rows_per_step,
                          alpha=alpha),
        grid=(t // tm, n_experts // eb),
        in_specs=[pl.BlockSpec((d, tm), lambda i, k: (0, i)),
                  row_spec, row_spec, sel_spec, sel_spec,
                  pl.BlockSpec((eb, d), lambda i, k: (k, 0)),
                  pl.BlockSpec((d, eb), lambda i, k: (0, k)),
                  pl.BlockSpec((eb, LANES), lambda i, k: (k, 0)),
                  pl.BlockSpec((eb, LANES), lambda i, k: (k, 0)),
                  pl.BlockSpec((1, LANES), lambda i, k: (0, 0)),
                  vec_spec, vec_spec],
        out_specs=pl.BlockSpec((tm, d), lambda i, k: (i, 0)),
        out_shape=jax.ShapeDtypeStruct((t, d), jnp.float32),
        scratch_shapes=[pltpu.VMEM((d, tm), FP8),
                        pltpu.VMEM((8, LANES), jnp.float32),
                        pltpu.VMEM((d, tm), jnp.float32),
                        pltpu.VMEM((eb, tm), jnp.bfloat16),
                        pltpu.VMEM((eb, tm), FP8)],
        compiler_params=pltpu.CompilerParams(
            dimension_semantics=("parallel", "arbitrary"),
            vmem_limit_bytes=PEER_VMEM_LIMIT_BYTES),
        name="peer_dense",
    )(ht, e0, thr, e1, rank, u, vt, c1, c2, post, ln_g[None, :], ln_b[None, :])


def _fold_keys_kernel(keys_ref, wq_ref, o_ref):
    o_ref[...] = lax.dot_general(
        keys_ref[0], wq_ref[...], (((1,), (1,)), ((), ())),
        precision=lax.Precision.HIGHEST, preferred_element_type=jnp.float32)


def _fold_keys(keys, wq):
    n_blocks, n_keys, half_key = keys.shape
    d = wq.shape[0]
    return pl.pallas_call(
        _fold_keys_kernel,
        grid=(n_blocks,),
        in_specs=[pl.BlockSpec((1, n_keys, half_key), lambda b: (b, 0, 0)),
                  pl.BlockSpec((d, half_key), lambda b: (0, b))],
        out_specs=pl.BlockSpec((n_keys, d), lambda b: (b, 0)),
        out_shape=jax.ShapeDtypeStruct((n_blocks * n_keys, d), jnp.float32),
        compiler_params=pltpu.CompilerParams(dimension_semantics=("parallel",)),
        name="fold_keys",
    )(keys, wq)


def _extract_top(work, n, row_ref, rank=None):
    for r in range(n):
        m = jnp.max(work, axis=0, keepdims=True)
        row_ref[r:r + 1, :] = m
        hit = work == m
        if rank is not None:
            rank = jnp.where(hit, float(r), rank)
        work = jnp.where(hit, -jnp.inf, work)
    return rank


def _peer_select_kernel(sc_ref, e0_ref, thr_ref, e1_ref, rank_ref, a_ref, b_ref, c_ref,
                        *, n_heads, n_keys):
    tl = sc_ref.shape[1]
    half = TOPK // 2

    def head_body(h, carry):
        r0 = pl.multiple_of(h * 2 * n_keys, 2 * n_keys)
        s0 = sc_ref[pl.ds(r0, n_keys), :]
        s1 = sc_ref[pl.ds(r0 + n_keys, n_keys), :]
        _extract_top(s0, TOPK, a_ref)
        rank1 = _extract_top(s1, TOPK, b_ref, jnp.full((n_keys, tl), 100.0, jnp.float32))
        pieces = [a_ref[0:1, :] + b_ref[...]]
        pieces += [a_ref[p:p + 1, :] + b_ref[0:half, :] for p in range(1, half)]
        pieces += [a_ref[half:TOPK, :] + b_ref[0:1, :]]
        _extract_top(jnp.concatenate(pieces, axis=0), TOPK, c_ref)
        best = c_ref[...]
        tau = best[TOPK - 1:TOPK, :]
        zsum = jnp.sum(jnp.exp(best - best[0:1, :]), axis=0, keepdims=True)
        b_all = b_ref[...]
        thr = jnp.zeros((n_keys, tl), jnp.float32)
        for p in range(TOPK):
            a_p = a_ref[p:p + 1, :]
            n_sel = jnp.sum(jnp.where(a_p + b_all >= tau, 1.0, 0.0), axis=0, keepdims=True)
            thr = jnp.where(s0 == a_p, n_sel, thr)
        e0_ref[h] = jnp.exp(s0 - a_ref[0:1, :]) / zsum
        thr_ref[h] = thr
        e1_ref[h] = jnp.exp(s1 - b_ref[0:1, :]).astype(e1_ref.dtype)
        rank_ref[h] = rank1.astype(rank_ref.dtype)
        return carry

    lax.fori_loop(0, n_heads, head_body, 0)


def _peer_select(scores_t, n_heads, n_keys, tl):
    rows, t = scores_t.shape
    assert rows == n_heads * 2 * n_keys
    out_spec = pl.BlockSpec((n_heads, n_keys, tl), lambda i: (0, 0, i))
    f32 = jax.ShapeDtypeStruct((n_heads, n_keys, t), jnp.float32)
    b16 = jax.ShapeDtypeStruct((n_heads, n_keys, t), jnp.bfloat16)
    return pl.pallas_call(
        functools.partial(_peer_select_kernel, n_heads=n_heads, n_keys=n_keys),
        grid=(t // tl,),
        in_specs=[pl.BlockSpec((rows, tl), lambda i: (0, i))],
        out_specs=[out_spec, out_spec, out_spec, out_spec],
        out_shape=[f32, f32, b16, b16],
        scratch_shapes=[pltpu.VMEM((TOPK, tl), jnp.float32)] * 3,
        compiler_params=pltpu.CompilerParams(
            dimension_semantics=("parallel",), vmem_limit_bytes=VMEM_LIMIT_BYTES),
        name="peer_select",
    )(scores_t)


def _log_sigmoid(x):
    return jnp.minimum(x, 0.0) - jnp.log1p(jnp.exp(-jnp.abs(x)))


def _mlstm_kernel(q_ref, k_ref, v_ref, o_ref, g_ref, gt_ref,
                  qm_ref, km_ref, vm_ref, om_ref, gm_ref, gmt_ref,
                  bias_ref, biast_ref, nw_ref, y_ref, c_ref, m_ref,
                  *, n_heads, head_qk, head_v):
    c = pl.program_id(1)
    tc = q_ref.shape[0]
    f32, bf = jnp.float32, jnp.bfloat16
    first = c == 0

    @pl.when(first)
    def _():
        c_ref[...] = jnp.zeros_like(c_ref)
        m_ref[...] = jnp.zeros_like(m_ref)

    q = jnp.where(first, qm_ref[...], q_ref[...])
    k = jnp.where(first, km_ref[...], k_ref[...])
    v = jnp.where(first, vm_ref[...], v_ref[...])
    o = jnp.where(first, om_ref[...], o_ref[...])
    g = jnp.where(first, gm_ref[...], g_ref[...]) + bias_ref[...]
    gt = jnp.where(first, gmt_ref[...], gt_ref[...]) + biast_ref[...]

    row = lax.broadcasted_iota(jnp.int32, (tc, tc), 0)
    col = lax.broadcasted_iota(jnp.int32, (tc, tc), 1)
    causal = col <= row
    tri = jnp.where(causal, 1.0, 0.0).astype(f32)
    b_cols = jnp.dot(tri, _log_sigmoid(g), precision=lax.Precision.HIGHEST,
                     preferred_element_type=f32)
    b_rows = lax.dot_general(_log_sigmoid(gt), tri, (((1,), (1,)), ((), ())),
                             precision=lax.Precision.HIGHEST, preferred_element_type=f32)
    ln_scale = -0.5 * math.log(head_qk)
    ones_blk = jnp.where(lax.broadcasted_iota(jnp.int32, (tc, LANES), 1) == 0, 1.0, 0.0).astype(bf)

    for h in range(n_heads):
        qh = q[:, h * head_qk:(h + 1) * head_qk]
        kh = k[:, h * head_qk:(h + 1) * head_qk]
        v_ext = jnp.concatenate([v[:, h * head_v:(h + 1) * head_v], ones_blk], axis=1)
        li_col = g[:, h:h + 1]
        b_col = b_cols[:, n_heads + h:n_heads + h + 1]
        li_row = gt[h:h + 1, :]
        b_row = b_rows[n_heads + h:n_heads + h + 1, :]
        m_prev = m_ref[h:h + 1, 0:1]
        d = jnp.where(causal, b_col + (li_row - b_row), -jnp.inf)
        inter = b_col + m_prev
        m_t = jnp.maximum(inter, jnp.max(d, axis=1, keepdims=True))
        w = jnp.exp(d - (m_t - ln_scale))
        s = lax.dot_general(qh, kh, (((1,), (1,)), ((), ())), preferred_element_type=f32) * w
        sc_inter = jnp.exp(inter - (m_t - ln_scale))
        c_ext = c_ref[h]
        num_ext = (jnp.dot(s.astype(bf), v_ext, preferred_element_type=f32)
                   + sc_inter * jnp.dot(qh, c_ext.astype(bf), preferred_element_type=f32))
        den = num_ext[:, head_v:head_v + 1]
        hm = num_ext[:, :head_v] / jnp.maximum(jnp.abs(den), jnp.exp(-m_t))
        hm = hm * lax.rsqrt(jnp.mean(hm * hm, axis=1, keepdims=True) + LN_EPS)
        gate_o = jax.nn.sigmoid(o[:, h * head_v:(h + 1) * head_v].astype(f32))
        y_ref[:, h * head_v:(h + 1) * head_v] = (
            gate_o * (hm * nw_ref[:, h * head_v:(h + 1) * head_v])).astype(y_ref.dtype)
        b_last = b_col[tc - 1:tc, :]
        dl = b_last - b_col + li_col
        m_new = jnp.maximum(b_last + m_prev, jnp.max(dl, axis=0, keepdims=True))
        wv = (jnp.exp(dl - m_new) * v_ext.astype(f32)).astype(bf)
        c_ref[h] = (jnp.exp(b_last + m_prev - m_new) * c_ext
                    + lax.dot_general(kh, wv, (((0,), (0,)), ((), ())),
                                      preferred_element_type=f32))
        m_ref[h:h + 1, :] = jnp.broadcast_to(m_new, (1, LANES))


def _mlstm(proj_x, proj_m, gif_x, gif_m, b_if, norm_w, *, bsz, seq, n_heads, head_qk, head_v,
           off_q, off_v):
    tc = min(MLSTM_CHUNK, seq)
    assert seq % tc == 0
    n_meta = proj_m.shape[0]
    d_mqk, d_mv = n_heads * head_qk, n_heads * head_v
    assert off_q % d_mqk == 0 and off_v % d_mv == 0 and n_meta <= tc
    nc_x = seq // tc
    bf = jnp.bfloat16
    lead = tc - n_meta
    pad_rows = lambda a: jnp.pad(a, ((lead, 0), (0, 0)))
    qm = pad_rows(proj_m[:, off_q:off_q + d_mqk])
    km = pad_rows(proj_m[:, off_q + d_mqk:off_q + 2 * d_mqk])
    vm = pad_rows(proj_m[:, off_v:off_v + d_mv])
    om = pad_rows(proj_m[:, off_v + d_mv:off_v + 2 * d_mv])
    pad_gate = jnp.concatenate([jnp.full((lead, n_heads), I_GATE_PAD, jnp.float32),
                                jnp.full((lead, n_heads), -I_GATE_PAD, jnp.float32),
                                jnp.zeros((lead, LANES - 2 * n_heads), jnp.float32)], axis=1)
    gm = jnp.concatenate([pad_gate, gif_m], axis=0)
    bias = jnp.pad(b_if, (0, LANES - 2 * n_heads))[None, :]
    biast = jnp.broadcast_to(b_if[:, None], (2 * n_heads, tc))
    gt_x = gif_x[:, :2 * n_heads].T
    gt_m = gm[:, :2 * n_heads].T

    xrow = lambda b, c: b * nc_x + jnp.maximum(c - 1, 0)
    const = lambda b, c: (0, 0)
    return pl.pallas_call(
        functools.partial(_mlstm_kernel, n_heads=n_heads, head_qk=head_qk, head_v=head_v),
        grid=(bsz, nc_x + 1),
        in_specs=[pl.BlockSpec((tc, d_mqk), lambda b, c: (xrow(b, c), off_q // d_mqk)),
                  pl.BlockSpec((tc, d_mqk), lambda b, c: (xrow(b, c), off_q // d_mqk + 1)),
                  pl.BlockSpec((tc, d_mv), lambda b, c: (xrow(b, c), off_v // d_mv)),
                  pl.BlockSpec((tc, d_mv), lambda b, c: (xrow(b, c), off_v // d_mv + 1)),
                  pl.BlockSpec((tc, LANES), lambda b, c: (xrow(b, c), 0)),
                  pl.BlockSpec((2 * n_heads, tc), lambda b, c: (0, xrow(b, c))),
                  pl.BlockSpec((tc, d_mqk), const), pl.BlockSpec((tc, d_mqk), const),
                  pl.BlockSpec((tc, d_mv), const), pl.BlockSpec((tc, d_mv), const),
                  pl.BlockSpec((tc, LANES), const), pl.BlockSpec((2 * n_heads, tc), const),
                  pl.BlockSpec((1, LANES), const), pl.BlockSpec((2 * n_heads, tc), const),
                  pl.BlockSpec((1, d_mv), const)],
        out_specs=pl.BlockSpec((tc, d_mv), lambda b, c: (xrow(b, c), 0)),
        out_shape=jax.ShapeDtypeStruct((bsz * seq, d_mv), bf),
        scratch_shapes=[pltpu.VMEM((n_heads, head_qk, head_v + LANES), jnp.float32),
                        pltpu.VMEM((n_heads, LANES), jnp.float32)],
        compiler_params=pltpu.CompilerParams(
            dimension_semantics=("parallel", "arbitrary"),
            vmem_limit_bytes=VMEM_LIMIT_BYTES),
        name="mlstm",
    )(proj_x, proj_x, proj_x, proj_x, gif_x, gt_x, qm, km, vm, om, gm, gt_m,
      bias, biast, norm_w[None, :])


def _mix_kernel(x_ref, cb_ref, cc_ref, cx_ref, cch_ref, cxh_ref, ccm_ref, cxm_ref,
                yb_ref, ga_ref, gb_ref, pa_ref, pb_ref, wo_ref, cw_ref,
                g0_ref, b0_ref, g1_ref, b1_ref, h1t_ref,
                *, tiles_per_batch, alpha):
    i = pl.program_id(0)
    f32, bf = jnp.float32, jnp.bfloat16
    tm = x_ref.shape[0]
    z = cc_ref[...].astype(f32) * cx_ref[...].astype(f32)
    batch_start = (i % tiles_per_batch) == 0
    halo = jnp.where(batch_start,
                     ccm_ref[...].astype(f32) * cxm_ref[...].astype(f32),
                     cch_ref[...].astype(f32) * cxh_ref[...].astype(f32))
    sub = lax.broadcasted_iota(jnp.int32, halo.shape, 0)
    n_halo = halo.shape[0]
    y = cw_ref[2:3, :] * z
    for shift in (1, 2):
        rolled = pltpu.roll(z, shift, 0)
        head = jnp.where(sub < shift, pltpu.roll(halo, shift, 0), rolled[:n_halo])
        y = y + cw_ref[2 - shift:3 - shift, :] * jnp.concatenate([head, rolled[n_halo:]], axis=0)
    y_a = (cb_ref[...].astype(f32) * y).astype(bf)
    za = jnp.dot(y_a, pa_ref[...], preferred_element_type=f32)
    zb = jnp.dot(yb_ref[...], pb_ref[...], preferred_element_type=f32)
    merged = (jax.nn.sigmoid(ga_ref[...].astype(f32)) * za
              + jax.nn.sigmoid(gb_ref[...].astype(f32)) * zb).astype(bf)
    mix = jnp.dot(merged, wo_ref[...], preferred_element_type=f32)
    h0 = _ln_rows(x_ref[...], g0_ref[...], b0_ref[...])
    h1t_ref[...] = _ln_rows(alpha * h0 + mix, g1_ref[...], b1_ref[...]).T


def _mix(x2, proj_x, proj_m, y_b, p_a, p_b, w_o, conv_w, ln0_g, ln0_b, ln1_g, ln1_b,
         *, seq, off_g, alpha, tm):
    t_x, d = x2.shape
    d_conv = conv_w.shape[-1]
    d_mv = y_b.shape[-1]
    n_meta = proj_m.shape[0]
    halo = 8
    assert seq % tm == 0 and tm % halo == 0 and n_meta % halo == 0 and off_g % d == 0
    assert conv_w.shape[0] == 3
    tile = lambda j: pl.BlockSpec((tm, d_conv), lambda i: (i, j))
    prev = lambda j: pl.BlockSpec((halo, d_conv),
                                  lambda i: (jnp.maximum(i * (tm // halo) - 1, 0), j))
    meta = lambda j: pl.BlockSpec((halo, d_conv), lambda i: (n_meta // halo - 1, j))
    const = lambda shape: pl.BlockSpec(shape, lambda i: (0, 0), pipeline_mode=pl.Buffered(1))
    vec = lambda a: a[None, :]
    return pl.pallas_call(
        functools.partial(_mix_kernel, tiles_per_batch=seq // tm, alpha=alpha),
        grid=(t_x // tm,),
        in_specs=[pl.BlockSpec((tm, d), lambda i: (i, 0)),
                  tile(0), tile(1), tile(2), prev(1), prev(2), meta(1), meta(2),
                  pl.BlockSpec((tm, d_mv), lambda i: (i, 0)),
                  pl.BlockSpec((tm, d), lambda i: (i, off_g // d)),
                  pl.BlockSpec((tm, d), lambda i: (i, off_g // d + 1)),
                  const(p_a.shape), const(p_b.shape), const(w_o.shape), const(conv_w.shape),
                  const((1, d)), const((1, d)), const((1, d)), const((1, d))],
        out_specs=pl.BlockSpec((d, tm), lambda i: (0, i)),
        out_shape=jax.ShapeDtypeStruct((d, t_x), jnp.float32),
        compiler_params=pltpu.CompilerParams(
            dimension_semantics=("parallel",), vmem_limit_bytes=VMEM_LIMIT_BYTES),
        name="mix",
    )(x2, proj_x, proj_x, proj_x, proj_x, proj_x, proj_m, proj_m, y_b, proj_x, proj_x,
      p_a, p_b, w_o, conv_w, vec(ln0_g), vec(ln0_b), vec(ln1_g), vec(ln1_b))


def kernel(x, meta_tokens, ln0_g, ln0_b, w_in, b_if, conv_w, mh_norm_w, p_a, p_b, w_o,
           ln1_g, ln1_b, peer_wq, peer_keys, peer_u, peer_v, ln2_g, ln2_b):
    depth = w_in.shape[0]
    assert depth == 1
    bsz, seq, d = x.shape
    n_meta = meta_tokens.shape[0]
    n_in = w_in.shape[-1]
    d_conv = conv_w.shape[-1]
    d_mv = mh_norm_w.shape[-1]
    n_mh = b_if.shape[-1] // 2
    d_mqk = (n_in - 3 * d_conv - 2 * d_mv - 2 * n_mh - 2 * d) // 2
    head_qk = d_mqk // n_mh
    head_v = d_mv // n_mh
    alpha = (2 * depth) ** 0.25
    bf = jnp.bfloat16
    t_x = bsz * seq

    x2 = x.reshape(t_x, d)
    gate_off = 3 * d_conv + 2 * d_mqk + 2 * d_mv
    w = w_in[0]
    w_main = jnp.concatenate([w[:, :gate_off], w[:, gate_off + 2 * n_mh:]], axis=1).astype(bf)
    w_gif = jnp.pad(w[:, gate_off:gate_off + 2 * n_mh], ((0, 0), (0, LANES - 2 * n_mh))).astype(bf)
    inproj_tn = min(INPROJ_TILE_N, w_main.shape[1])
    proj_x, gif_x = _inproj(x2, ln0_g, ln0_b, w_main, w_gif, INPROJ_TILE_M, inproj_tn)
    proj_m, gif_m = _inproj(meta_tokens, ln0_g, ln0_b, w_main, w_gif, INPROJ_TILE_M, inproj_tn)

    off_q = 3 * d_conv
    off_v = off_q + 2 * d_mqk
    off_g = off_v + 2 * d_mv
    y_b = _mlstm(proj_x, proj_m, gif_x, gif_m, b_if[0], mh_norm_w[0], bsz=bsz, seq=seq,
                 n_heads=n_mh, head_qk=head_qk, head_v=head_v, off_q=off_q, off_v=off_v)
    h1_t = _mix(x2, proj_x, proj_m, y_b,
                p_a[0].astype(bf), p_b[0].astype(bf), w_o[0].astype(bf), conv_w[0],
                ln0_g, ln0_b, ln1_g[0], ln1_b[0], seq=seq, off_g=off_g, alpha=alpha,
                tm=MIX_TILE)

    n_ph, _, n_keys, half_key = peer_keys.shape[1:]
    w_fold = _fold_keys(peer_keys[0].reshape(n_ph * 2, n_keys, half_key), peer_wq[0])
    scores_t = _matmul(w_fold, h1_t, jnp.float32, 1024, 512)
    e0, thr, e1, rank = _peer_select(scores_t, n_ph, n_keys, LANES)
    out = _peer_dense(h1_t, e0, thr, e1, rank, peer_u[0], peer_v[0],
                      ln2_g[0], ln2_b[0], alpha=alpha, tm=PEER_TOKEN_TILE,
                      rows_per_step=PEER_ROWS_PER_STEP)
    return out.reshape(bsz, seq, d)
```

```python
import functools
import math

import jax
import jax.numpy as jnp
from jax import lax
from jax.experimental import pallas as pl
from jax.experimental.pallas import tpu as pltpu

TOPK = 16
LN_EPS = 1e-5
I_GATE_PAD = -1e30
MLSTM_CHUNK = 256
MIX_TILE = 256
INPROJ_TILE_M = 1024
INPROJ_TILE_N = 1024
PEER_TOKEN_TILE = 512
PEER_ROWS_PER_STEP = 8
QUANT_ROWS = 512
LANES = 128
BF16_SUBLANES = 16
VMEM_LIMIT_BYTES = 56 * 1024 * 1024
PEER_VMEM_LIMIT_BYTES = 58 * 1024 * 1024
FP8 = jnp.float8_e4m3fn
FP8_CLIP = 448.0
FP8_TARGET = 256.0
PEER_P_SCALE = 64.0


def _gelu_exact(x):
    return 0.5 * x * (1.0 + lax.erf(x * (2.0 ** -0.5)))


def _ln_rows(x, g, b):
    mu = jnp.mean(x, axis=1, keepdims=True)
    xc = x - mu
    var = jnp.mean(xc * xc, axis=1, keepdims=True)
    return xc * lax.rsqrt(var + LN_EPS) * g + b


def _inproj_kernel(x_ref, g_ref, b_ref, w_ref, wg_ref, o_ref, og_ref, ogt_ref, h_ref):
    @pl.when(pl.program_id(1) == 0)
    def _():
        h = _ln_rows(x_ref[...], g_ref[...], b_ref[...]).astype(h_ref.dtype)
        h_ref[...] = h
        gates = jnp.dot(h, wg_ref[...], preferred_element_type=jnp.float32)
        og_ref[...] = gates
        ogt_ref[...] = gates.T

    o_ref[...] = jnp.dot(h_ref[...], w_ref[...],
                         preferred_element_type=jnp.float32).astype(o_ref.dtype)


def _inproj(x2, ln_g, ln_b, w_main, w_gate, tm, tn):
    m, d = x2.shape
    n = w_main.shape[1]
    tm = min(tm, m)
    assert m % tm == 0 and n % tn == 0 and w_gate.shape[1] == LANES
    const = lambda shape: pl.BlockSpec(shape, lambda i, j: (0, 0))
    return pl.pallas_call(
        _inproj_kernel,
        grid=(m // tm, n // tn),
        in_specs=[pl.BlockSpec((tm, d), lambda i, j: (i, 0)),
                  const((1, d)), const((1, d)),
                  pl.BlockSpec((d, tn), lambda i, j: (0, j)),
                  const((d, LANES))],
        out_specs=[pl.BlockSpec((tm, tn), lambda i, j: (i, j)),
                   pl.BlockSpec((tm, LANES), lambda i, j: (i, 0)),
                   pl.BlockSpec((LANES, tm), lambda i, j: (0, i))],
        out_shape=[jax.ShapeDtypeStruct((m, n), jnp.bfloat16),
                   jax.ShapeDtypeStruct((m, LANES), jnp.float32),
                   jax.ShapeDtypeStruct((LANES, m), jnp.float32)],
        scratch_shapes=[pltpu.VMEM((tm, d), jnp.bfloat16)],
        compiler_params=pltpu.CompilerParams(
            dimension_semantics=("parallel", "arbitrary"),
            vmem_limit_bytes=VMEM_LIMIT_BYTES),
        name="inproj",
    )(x2, ln_g[None, :], ln_b[None, :], w_main, w_gate)


def _mm_kernel(a_ref, b_ref, o_ref):
    bf = jnp.bfloat16
    o_ref[...] = jnp.dot(a_ref[...].astype(bf), b_ref[...].astype(bf),
                         preferred_element_type=jnp.float32).astype(o_ref.dtype)


def _matmul(a, b, out_dtype, tm, tn):
    m, k = a.shape
    _, n = b.shape
    tm = min(tm, m)
    tn = min(tn, n)
    assert m % tm == 0 and n % tn == 0
    return pl.pallas_call(
        _mm_kernel,
        grid=(m // tm, n // tn),
        in_specs=[pl.BlockSpec((tm, k), lambda i, j: (i, 0)),
                  pl.BlockSpec((k, tn), lambda i, j: (0, j))],
        out_specs=pl.BlockSpec((tm, tn), lambda i, j: (i, j)),
        out_shape=jax.ShapeDtypeStruct((m, n), out_dtype),
        compiler_params=pltpu.CompilerParams(
            dimension_semantics=("parallel", "parallel"),
            vmem_limit_bytes=VMEM_LIMIT_BYTES),
        name="matmul",
    )(a, b)


def _pow2_scale(amax):
    safe = jnp.where(amax > 0, amax, FP8_TARGET)
    return jnp.exp2(jnp.floor(jnp.log2(FP8_TARGET / safe)))


def _quantize_rows_kernel(x_ref, q_ref, inv_ref, *, transpose):
    x = x_ref[...]
    scale = _pow2_scale(jnp.max(jnp.abs(x), axis=1, keepdims=True))
    xs = x * scale
    q_ref[...] = (xs.T if transpose else xs).astype(q_ref.dtype)
    inv_ref[...] = jnp.broadcast_to(1.0 / scale, inv_ref.shape)


def _quantize_rows(x, rows, transpose):
    n, d = x.shape
    rows = min(rows, n)
    assert n % rows == 0
    q_shape, q_spec = (((d, n), pl.BlockSpec((d, rows), lambda i: (0, i))) if transpose
                       else ((n, d), pl.BlockSpec((rows, d), lambda i: (i, 0))))
    return pl.pallas_call(
        functools.partial(_quantize_rows_kernel, transpose=transpose),
        grid=(n // rows,),
        in_specs=[pl.BlockSpec((rows, d), lambda i: (i, 0))],
        out_specs=[q_spec, pl.BlockSpec((rows, LANES), lambda i: (i, 0))],
        out_shape=[jax.ShapeDtypeStruct(q_shape, FP8),
                   jax.ShapeDtypeStruct((n, LANES), jnp.float32)],
        compiler_params=pltpu.CompilerParams(
            dimension_semantics=("parallel",), vmem_limit_bytes=VMEM_LIMIT_BYTES),
        name="quantize_rows",
    )(x)


def _peer_dense_kernel(ht_ref, e0_ref, thr_ref, e1_ref, rank_ref, u_ref, vt_ref, c1_ref, c2_ref,
                       post_ref, g_ref, b_ref, o_ref, hb_ref, sh_ref, acc_ref, w_ref, p_ref,
                       *, n_heads, rows_per_step, alpha):
    k = pl.program_id(1)
    bf, f8 = jnp.bfloat16, FP8
    tm = ht_ref.shape[1]
    zero = jnp.zeros((), bf)

    def row_tile(ref, h, ii):
        row = jnp.broadcast_to(ref[h, ii:ii + 1, :], (BF16_SUBLANES, tm)).astype(bf)
        return pltpu.repeat(row, LANES // BF16_SUBLANES, axis=0)

    @pl.when(k == 0)
    def _():
        acc_ref[...] = jnp.zeros_like(acc_ref)
        h = ht_ref[...]
        amax = jnp.max(jnp.max(jnp.abs(h), axis=0, keepdims=True), axis=1, keepdims=True)
        s_h = _pow2_scale(amax)
        hb_ref[...] = (h * s_h).astype(f8)
        sh_ref[...] = jnp.broadcast_to(1.0 / s_h, sh_ref.shape)

    for ii in range(rows_per_step):
        w = jnp.zeros((LANES, tm), bf)
        for h in range(n_heads):
            sel = rank_ref[h] < row_tile(thr_ref, h, ii)
            w = w + jnp.where(sel, e1_ref[h], zero) * row_tile(e0_ref, h, ii)
        w_ref[ii * LANES:(ii + 1) * LANES, :] = w
    a = jnp.dot(u_ref[...], hb_ref[...], preferred_element_type=jnp.float32)
    inv_sh = sh_ref[0:1, 0:1]
    c1 = pltpu.repeat(c1_ref[...] * inv_sh, tm // LANES, axis=1)
    c2 = pltpu.repeat(c2_ref[...] * inv_sh, tm // LANES, axis=1)
    act = ((a * c1) * (1.0 + lax.erf(a * c2))).astype(bf)
    p = jnp.clip(w_ref[...] * act, -FP8_CLIP, FP8_CLIP)
    p_ref[...] = p.astype(f8)
    acc_ref[...] += jnp.dot(vt_ref[...], p_ref[...], preferred_element_type=jnp.float32)

    @pl.when(k == pl.num_programs(1) - 1)
    def _():
        y = alpha * ht_ref[...] + acc_ref[...] * post_ref[0:1, 0:1]
        mu = jnp.mean(y, axis=0, keepdims=True)
        yc = y - mu
        var = jnp.mean(yc * yc, axis=0, keepdims=True)
        o_ref[...] = (yc * lax.rsqrt(var + LN_EPS)).T * g_ref[...] + b_ref[...]


def _peer_dense(ht, e0, thr, e1, rank, peer_u, peer_v, ln_g, ln_b, *, alpha, tm, rows_per_step):
    u, inv_u = _quantize_rows(peer_u, QUANT_ROWS, transpose=False)
    vt, inv_v = _quantize_rows(peer_v, QUANT_ROWS, transpose=True)
    inv_v_max = jnp.max(inv_v)
    c1 = inv_u * (inv_v / inv_v_max) * (0.5 * PEER_P_SCALE)
    c2 = inv_u * (2.0 ** -0.5)
    post = jnp.broadcast_to(inv_v_max / PEER_P_SCALE, (1, LANES)).astype(jnp.float32)
    d, t = ht.shape
    n_heads, n_keys, _ = e0.shape
    assert n_keys == LANES
    n_experts = u.shape[0]
    eb = rows_per_step * LANES
    sel_spec = pl.BlockSpec((n_heads, n_keys, tm), lambda i, k: (0, 0, i))
    row_spec = pl.BlockSpec((n_heads, rows_per_step, tm), lambda i, k: (0, k, i))
    vec_spec = pl.BlockSpec((1, d), lambda i, k: (0, 0))
    return pl.pallas_call(
        functools.partial(_peer_dense_kernel, n_heads=n_heads, rows_per_step=rows_per_step,
                          alpha=alpha),
        grid=(t // tm, n_experts // eb),
        in_specs=[pl.BlockSpec((d, tm), lambda i, k: (0, i)),
                  row_spec, row_spec, sel_spec, sel_spec,
                  pl.BlockSpec((eb, d), lambda i, k: (k, 0)),
                  pl.BlockSpec((d, eb), lambda i, k: (0, k)),
                  pl.BlockSpec((eb, LANES), lambda i, k: (k, 0)),
                  pl.BlockSpec((eb, LANES), lambda i, k: (k, 0)),
                  pl.BlockSpec((1, LANES), lambda i, k: (0, 0)),
                  vec_spec, vec_spec],
        out_specs=pl.BlockSpec((tm, d), lambda i, k: (i, 0)),
        out_shape=jax.ShapeDtypeStruct((t, d), jnp.float32),
        scratch_shapes=[pltpu.VMEM((d, tm), FP8),
                        pltpu.VMEM((8, LANES), jnp.float32),
                        pltpu.VMEM((d, tm), jnp.float32),
                        pltpu.VMEM((eb, tm), jnp.bfloat16),
                        pltpu.VMEM((eb, tm), FP8)],
        compiler_params=pltpu.CompilerParams(
            dimension_semantics=("parallel", "arbitrary"),
            vmem_limit_bytes=PEER_VMEM_LIMIT_BYTES),
        name="peer_dense",
    )(ht, e0, thr, e1, rank, u, vt, c1, c2, post, ln_g[None, :], ln_b[None, :])


def _fold_keys_kernel(keys_ref, wq_ref, o_ref):
    o_ref[...] = lax.dot_general(
        keys_ref[0], wq_ref[...], (((1,), (1,)), ((), ())),
        precision=lax.Precision.HIGHEST, preferred_element_type=jnp.float32)


def _fold_keys(keys, wq):
    n_blocks, n_keys, half_key = keys.shape
    d = wq.shape[0]
    return pl.pallas_call(
        _fold_keys_kernel,
        grid=(n_blocks,),
        in_specs=[pl.BlockSpec((1, n_keys, half_key), lambda b: (b, 0, 0)),
                  pl.BlockSpec((d, half_key), lambda b: (0, b))],
        out_specs=pl.BlockSpec((n_keys, d), lambda b: (b, 0)),
        out_shape=jax.ShapeDtypeStruct((n_blocks * n_keys, d), jnp.float32),
        compiler_params=pltpu.CompilerParams(dimension_semantics=("parallel",)),
        name="fold_keys",
    )(keys, wq)


def _batcher_pairs(n):
    pairs = []
    p = 1
    while p < n:
        k = p
        while k >= 1:
            for j in range(k % p, n - k, 2 * k):
                for i in range(min(k, n - j - k)):
                    if (i + j) // (2 * p) == (i + j + k) // (2 * p):
                        pairs.append((i + j, i + j + k))
            k //= 2
        p *= 2
    return pairs


_SORT_PAIRS = _batcher_pairs(TOPK)
_MERGE_PAIRS = [(i, i + d) for d in (8, 4, 2, 1) for i in range(TOPK) if not i & d]


def _compare_exchange(v, pairs):
    v = list(v)
    for i, j in pairs:
        v[i], v[j] = jnp.maximum(v[i], v[j]), jnp.minimum(v[i], v[j])
    return v


def _top16(tiles):
    v = _compare_exchange(tiles, _SORT_PAIRS)
    for shift in (4, 2, 1):
        other = [pltpu.roll(x, shift, 0) for x in v]
        v = [jnp.maximum(v[r], other[TOPK - 1 - r]) for r in range(TOPK)]
        v = _compare_exchange(v, _MERGE_PAIRS)
    return v


def _stack_rows(tiles, sub):
    out = tiles[0]
    for r in range(1, 8):
        out = jnp.where(sub == r, tiles[r], out)
    return out


def _peer_select_kernel(sc_ref, e0_ref, thr_ref, e1_ref, rank_ref, *, n_heads, n_keys):
    tl = sc_ref.shape[1]
    reps = n_keys // 8
    sub = lax.broadcasted_iota(jnp.int32, (8, tl), 0)
    spread = lambda tile: pltpu.repeat(tile, reps, axis=0)

    def head_body(h, carry):
        r0 = pl.multiple_of(h * 2 * n_keys, 2 * n_keys)
        s0 = sc_ref[pl.ds(r0, n_keys), :]
        s1 = sc_ref[pl.ds(r0 + n_keys, n_keys), :]
        a = _top16([s0[8 * r:8 * r + 8, :] for r in range(reps)])
        b = _top16([s1[8 * r:8 * r + 8, :] for r in range(reps)])
        b_lo, b_hi, a_hi = _stack_rows(b[:8], sub), _stack_rows(b[8:], sub), _stack_rows(a[8:], sub)
        cand = [a[0] + b_lo, a[0] + b_hi] + [a[p] + b_lo for p in range(1, 8)] + [a_hi + b[0]]
        cand += [jnp.full((8, tl), -jnp.inf, jnp.float32)] * (TOPK - len(cand))
        best = _top16(cand)
        tau = best[TOPK - 1]
        zsum = sum(jnp.exp(c - best[0]) for c in best)
        thr = jnp.zeros((n_keys, tl), jnp.float32)
        rank1 = jnp.full((n_keys, tl), 100.0, jnp.float32)
        for p in reversed(range(TOPK)):
            n_sel = (jnp.where(a[p] + b_lo >= tau, 1.0, 0.0)
                     + jnp.where(a[p] + b_hi >= tau, 1.0, 0.0))
            for shift in (4, 2, 1):
                n_sel = n_sel + pltpu.roll(n_sel, shift, 0)
            thr = jnp.where(s0 == spread(a[p]), spread(n_sel), thr)
            rank1 = jnp.where(s1 == spread(b[p]), float(p), rank1)
        e0_ref[h] = jnp.exp(s0 - spread(a[0])) / spread(zsum)
        thr_ref[h] = thr
        e1_ref[h] = jnp.exp(s1 - spread(b[0])).astype(e1_ref.dtype)
        rank_ref[h] = rank1.astype(rank_ref.dtype)
        return carry

    lax.fori_loop(0, n_heads, head_body, 0)


def _peer_select(scores_t, n_heads, n_keys, tl):
    rows, t = scores_t.shape
    assert rows == n_heads * 2 * n_keys and n_keys == 8 * TOPK
    out_spec = pl.BlockSpec((n_heads, n_keys, tl), lambda i: (0, 0, i))
    f32 = jax.ShapeDtypeStruct((n_heads, n_keys, t), jnp.float32)
    b16 = jax.ShapeDtypeStruct((n_heads, n_keys, t), jnp.bfloat16)
    return pl.pallas_call(
        functools.partial(_peer_select_kernel, n_heads=n_heads, n_keys=n_keys),
        grid=(t // tl,),
        in_specs=[pl.BlockSpec((rows, tl), lambda i: (0, i))],
        out_specs=[out_spec, out_spec, out_spec, out_spec],
        out_shape=[f32, f32, b16, b16],
        compiler_params=pltpu.CompilerParams(
            dimension_semantics=("parallel",), vmem_limit_bytes=VMEM_LIMIT_BYTES),
        name="peer_select",
    )(scores_t)


def _log_sigmoid(x):
    return jnp.minimum(x, 0.0) - jnp.log1p(jnp.exp(-jnp.abs(x)))


def _mlstm_kernel(q_ref, k_ref, v_ref, o_ref, g_ref, gt_ref,
                  qm_ref, km_ref, vm_ref, om_ref, gm_ref, gmt_ref,
                  bias_ref, biast_ref, nw_ref, y_ref, c_ref, m_ref,
                  *, n_heads, head_qk, head_v):
    c = pl.program_id(1)
    tc = q_ref.shape[0]
    f32, bf = jnp.float32, jnp.bfloat16
    first = c == 0

    @pl.when(first)
    def _():
        c_ref[...] = jnp.zeros_like(c_ref)
        m_ref[...] = jnp.zeros_like(m_ref)

    q = jnp.where(first, qm_ref[...], q_ref[...])
    k = jnp.where(first, km_ref[...], k_ref[...])
    v = jnp.where(first, vm_ref[...], v_ref[...])
    o = jnp.where(first, om_ref[...], o_ref[...])
    g = jnp.where(first, gm_ref[...], g_ref[...]) + bias_ref[...]
    gt = jnp.where(first, gmt_ref[...], gt_ref[...]) + biast_ref[...]

    row = lax.broadcasted_iota(jnp.int32, (tc, tc), 0)
    col = lax.broadcasted_iota(jnp.int32, (tc, tc), 1)
    causal = col <= row
    tri = jnp.where(causal, 1.0, 0.0).astype(f32)
    b_cols = jnp.dot(tri, _log_sigmoid(g), precision=lax.Precision.HIGHEST,
                     preferred_element_type=f32)
    b_rows = lax.dot_general(_log_sigmoid(gt), tri, (((1,), (1,)), ((), ())),
                             precision=lax.Precision.HIGHEST, preferred_element_type=f32)
    ln_scale = -0.5 * math.log(head_qk)
    ones_blk = jnp.where(lax.broadcasted_iota(jnp.int32, (tc, LANES), 1) == 0, 1.0, 0.0).astype(bf)

    for h in range(n_heads):
        qh = q[:, h * head_qk:(h + 1) * head_qk]
        kh = k[:, h * head_qk:(h + 1) * head_qk]
        v_ext = jnp.concatenate([v[:, h * head_v:(h + 1) * head_v], ones_blk], axis=1)
        li_col = g[:, h:h + 1]
        b_col = b_cols[:, n_heads + h:n_heads + h + 1]
        li_row = gt[h:h + 1, :]
        b_row = b_rows[n_heads + h:n_heads + h + 1, :]
        m_prev = m_ref[h:h + 1, 0:1]
        d = jnp.where(causal, b_col + (li_row - b_row), -jnp.inf)
        inter = b_col + m_prev
        m_t = jnp.maximum(inter, jnp.max(d, axis=1, keepdims=True))
        w = jnp.exp(d - (m_t - ln_scale))
        s = lax.dot_general(qh, kh, (((1,), (1,)), ((), ())), preferred_element_type=f32) * w
        sc_inter = jnp.exp(inter - (m_t - ln_scale))
        c_ext = c_ref[h]
        num_ext = (jnp.dot(s.astype(bf), v_ext, preferred_element_type=f32)
                   + sc_inter * jnp.dot(qh, c_ext.astype(bf), preferred_element_type=f32))
        den = num_ext[:, head_v:head_v + 1]
        hm = num_ext[:, :head_v] / jnp.maximum(jnp.abs(den), jnp.exp(-m_t))
        hm = hm * lax.rsqrt(jnp.mean(hm * hm, axis=1, keepdims=True) + LN_EPS)
        gate_o = jax.nn.sigmoid(o[:, h * head_v:(h + 1) * head_v].astype(f32))
        y_ref[:, h * head_v:(h + 1) * head_v] = (
            gate_o * (hm * nw_ref[:, h * head_v:(h + 1) * head_v])).astype(y_ref.dtype)
        b_last = b_col[tc - 1:tc, :]
        dl = b_last - b_col + li_col
        m_new = jnp.maximum(b_last + m_prev, jnp.max(dl, axis=0, keepdims=True))
        wv = (jnp.exp(dl - m_new) * v_ext.astype(f32)).astype(bf)
        c_ref[h] = (jnp.exp(b_last + m_prev - m_new) * c_ext
                    + lax.dot_general(kh, wv, (((0,), (0,)), ((), ())),
                                      preferred_element_type=f32))
        m_ref[h:h + 1, :] = jnp.broadcast_to(m_new, (1, LANES))


def _mlstm(proj_x, proj_m, gif_x, gt_x, gif_m, b_if, norm_w, *, bsz, seq, n_heads, head_qk,
           head_v, off_q, off_v):
    tc = min(MLSTM_CHUNK, seq)
    assert seq % tc == 0
    n_meta = proj_m.shape[0]
    d_mqk, d_mv = n_heads * head_qk, n_heads * head_v
    assert off_q % d_mqk == 0 and off_v % d_mv == 0 and n_meta <= tc
    nc_x = seq // tc
    bf = jnp.bfloat16
    lead = tc - n_meta
    pad_rows = lambda a: jnp.pad(a, ((lead, 0), (0, 0)))
    qm = pad_rows(proj_m[:, off_q:off_q + d_mqk])
    km = pad_rows(proj_m[:, off_q + d_mqk:off_q + 2 * d_mqk])
    vm = pad_rows(proj_m[:, off_v:off_v + d_mv])
    om = pad_rows(proj_m[:, off_v + d_mv:off_v + 2 * d_mv])
    pad_gate = jnp.concatenate([jnp.full((lead, n_heads), I_GATE_PAD, jnp.float32),
                                jnp.full((lead, n_heads), -I_GATE_PAD, jnp.float32),
                                jnp.zeros((lead, LANES - 2 * n_heads), jnp.float32)], axis=1)
    gm = jnp.concatenate([pad_gate, gif_m], axis=0)
    bias = jnp.pad(b_if, (0, LANES - 2 * n_heads))[None, :]
    biast = jnp.broadcast_to(b_if[:, None], (2 * n_heads, tc))
    gt_m = gm[:, :2 * n_heads].T

    xrow = lambda b, c: b * nc_x + jnp.maximum(c - 1, 0)
    const = lambda b, c: (0, 0)
    return pl.pallas_call(
        functools.partial(_mlstm_kernel, n_heads=n_heads, head_qk=head_qk, head_v=head_v),
        grid=(bsz, nc_x + 1),
        in_specs=[pl.BlockSpec((tc, d_mqk), lambda b, c: (xrow(b, c), off_q // d_mqk)),
                  pl.BlockSpec((tc, d_mqk), lambda b, c: (xrow(b, c), off_q // d_mqk + 1)),
                  pl.BlockSpec((tc, d_mv), lambda b, c: (xrow(b, c), off_v // d_mv)),
                  pl.BlockSpec((tc, d_mv), lambda b, c: (xrow(b, c), off_v // d_mv + 1)),
                  pl.BlockSpec((tc, LANES), lambda b, c: (xrow(b, c), 0)),
                  pl.BlockSpec((2 * n_heads, tc), lambda b, c: (0, xrow(b, c))),
                  pl.BlockSpec((tc, d_mqk), const), pl.BlockSpec((tc, d_mqk), const),
                  pl.BlockSpec((tc, d_mv), const), pl.BlockSpec((tc, d_mv), const),
                  pl.BlockSpec((tc, LANES), const), pl.BlockSpec((2 * n_heads, tc), const),
                  pl.BlockSpec((1, LANES), const), pl.BlockSpec((2 * n_heads, tc), const),
                  pl.BlockSpec((1, d_mv), const)],
        out_specs=pl.BlockSpec((tc, d_mv), lambda b, c: (xrow(b, c), 0)),
        out_shape=jax.ShapeDtypeStruct((bsz * seq, d_mv), bf),
        scratch_shapes=[pltpu.VMEM((n_heads, head_qk, head_v + LANES), jnp.float32),
                        pltpu.VMEM((n_heads, LANES), jnp.float32)],
        compiler_params=pltpu.CompilerParams(
            dimension_semantics=("parallel", "arbitrary"),
            vmem_limit_bytes=VMEM_LIMIT_BYTES),
        name="mlstm",
    )(proj_x, proj_x, proj_x, proj_x, gif_x, gt_x, qm, km, vm, om, gm, gt_m,
      bias, biast, norm_w[None, :])


def _mix_kernel(x_ref, cb_ref, cc_ref, cx_ref, cch_ref, cxh_ref, ccm_ref, cxm_ref,
                yb_ref, ga_ref, gb_ref, pa_ref, pb_ref, wo_ref, cw_ref,
                g0_ref, b0_ref, g1_ref, b1_ref, h1t_ref,
                *, tiles_per_batch, alpha):
    i = pl.program_id(0)
    f32, bf = jnp.float32, jnp.bfloat16
    tm = x_ref.shape[0]
    z = cc_ref[...].astype(f32) * cx_ref[...].astype(f32)
    batch_start = (i % tiles_per_batch) == 0
    halo = jnp.where(batch_start,
                     ccm_ref[...].astype(f32) * cxm_ref[...].astype(f32),
                     cch_ref[...].astype(f32) * cxh_ref[...].astype(f32))
    sub = lax.broadcasted_iota(jnp.int32, halo.shape, 0)
    n_halo = halo.shape[0]
    y = cw_ref[2:3, :] * z
    for shift in (1, 2):
        rolled = pltpu.roll(z, shift, 0)
        head = jnp.where(sub < shift, pltpu.roll(halo, shift, 0), rolled[:n_halo])
        y = y + cw_ref[2 - shift:3 - shift, :] * jnp.concatenate([head, rolled[n_halo:]], axis=0)
    y_a = (cb_ref[...].astype(f32) * y).astype(bf)
    za = jnp.dot(y_a, pa_ref[...], preferred_element_type=f32)
    zb = jnp.dot(yb_ref[...], pb_ref[...], preferred_element_type=f32)
    merged = (jax.nn.sigmoid(ga_ref[...].astype(f32)) * za
              + jax.nn.sigmoid(gb_ref[...].astype(f32)) * zb).astype(bf)
    mix = jnp.dot(merged, wo_ref[...], preferred_element_type=f32)
    h0 = _ln_rows(x_ref[...], g0_ref[...], b0_ref[...])
    h1t_ref[...] = _ln_rows(alpha * h0 + mix, g1_ref[...], b1_ref[...]).T


def _mix(x2, proj_x, proj_m, y_b, p_a, p_b, w_o, conv_w, ln0_g, ln0_b, ln1_g, ln1_b,
         *, seq, off_g, alpha, tm):
    t_x, d = x2.shape
    d_conv = conv_w.shape[-1]
    d_mv = y_b.shape[-1]
    n_meta = proj_m.shape[0]
    halo = 8
    assert seq % tm == 0 and tm % halo == 0 and n_meta % halo == 0 and off_g % d == 0
    assert conv_w.shape[0] == 3
    tile = lambda j: pl.BlockSpec((tm, d_conv), lambda i: (i, j))
    prev = lambda j: pl.BlockSpec((halo, d_conv),
                                  lambda i: (jnp.maximum(i * (tm // halo) - 1, 0), j))
    meta = lambda j: pl.BlockSpec((halo, d_conv), lambda i: (n_meta // halo - 1, j))
    const = lambda shape: pl.BlockSpec(shape, lambda i: (0, 0), pipeline_mode=pl.Buffered(1))
    vec = lambda a: a[None, :]
    return pl.pallas_call(
        functools.partial(_mix_kernel, tiles_per_batch=seq // tm, alpha=alpha),
        grid=(t_x // tm,),
        in_specs=[pl.BlockSpec((tm, d), lambda i: (i, 0)),
                  tile(0), tile(1), tile(2), prev(1), prev(2), meta(1), meta(2),
                  pl.BlockSpec((tm, d_mv), lambda i: (i, 0)),
                  pl.BlockSpec((tm, d), lambda i: (i, off_g // d)),
                  pl.BlockSpec((tm, d), lambda i: (i, off_g // d + 1)),
                  const(p_a.shape), const(p_b.shape), const(w_o.shape), const(conv_w.shape),
                  const((1, d)), const((1, d)), const((1, d)), const((1, d))],
        out_specs=pl.BlockSpec((d, tm), lambda i: (0, i)),
        out_shape=jax.ShapeDtypeStruct((d, t_x), jnp.float32),
        compiler_params=pltpu.CompilerParams(
            dimension_semantics=("parallel",), vmem_limit_bytes=VMEM_LIMIT_BYTES),
        name="mix",
    )(x2, proj_x, proj_x, proj_x, proj_x, proj_x, proj_m, proj_m, y_b, proj_x, proj_x,
      p_a, p_b, w_o, conv_w, vec(ln0_g), vec(ln0_b), vec(ln1_g), vec(ln1_b))


def kernel(x, meta_tokens, ln0_g, ln0_b, w_in, b_if, conv_w, mh_norm_w, p_a, p_b, w_o,
           ln1_g, ln1_b, peer_wq, peer_keys, peer_u, peer_v, ln2_g, ln2_b):
    depth = w_in.shape[0]
    assert depth == 1
    bsz, seq, d = x.shape
    n_meta = meta_tokens.shape[0]
    n_in = w_in.shape[-1]
    d_conv = conv_w.shape[-1]
    d_mv = mh_norm_w.shape[-1]
    n_mh = b_if.shape[-1] // 2
    d_mqk = (n_in - 3 * d_conv - 2 * d_mv - 2 * n_mh - 2 * d) // 2
    head_qk = d_mqk // n_mh
    head_v = d_mv // n_mh
    alpha = (2 * depth) ** 0.25
    bf = jnp.bfloat16
    t_x = bsz * seq

    x2 = x.reshape(t_x, d)
    gate_off = 3 * d_conv + 2 * d_mqk + 2 * d_mv
    w = w_in[0]
    w_main = jnp.concatenate([w[:, :gate_off], w[:, gate_off + 2 * n_mh:]], axis=1).astype(bf)
    w_gif = jnp.pad(w[:, gate_off:gate_off + 2 * n_mh], ((0, 0), (0, LANES - 2 * n_mh))).astype(bf)
    inproj_tn = min(INPROJ_TILE_N, w_main.shape[1])
    proj_x, gif_x, gt_x = _inproj(x2, ln0_g, ln0_b, w_main, w_gif, INPROJ_TILE_M, inproj_tn)
    proj_m, gif_m, _ = _inproj(meta_tokens, ln0_g, ln0_b, w_main, w_gif, INPROJ_TILE_M,
                               inproj_tn)

    off_q = 3 * d_conv
    off_v = off_q + 2 * d_mqk
    off_g = off_v + 2 * d_mv
    y_b = _mlstm(proj_x, proj_m, gif_x, gt_x, gif_m, b_if[0], mh_norm_w[0], bsz=bsz, seq=seq,
                 n_heads=n_mh, head_qk=head_qk, head_v=head_v, off_q=off_q, off_v=off_v)
    h1_t = _mix(x2, proj_x, proj_m, y_b,
                p_a[0].astype(bf), p_b[0].astype(bf), w_o[0].astype(bf), conv_w[0],
                ln0_g, ln0_b, ln1_g[0], ln1_b[0], seq=seq, off_g=off_g, alpha=alpha,
                tm=MIX_TILE)

    n_ph, _, n_keys, half_key = peer_keys.shape[1:]
    w_fold = _fold_keys(peer_keys[0].reshape(n_ph * 2, n_keys, half_key), peer_wq[0])
    scores_t = _matmul(w_fold, h1_t, jnp.float32, 1024, 512)
    e0, thr, e1, rank = _peer_select(scores_t, n_ph, n_keys, LANES)
    out = _peer_dense(h1_t, e0, thr, e1, rank, peer_u[0], peer_v[0],
                      ln2_g[0], ln2_b[0], alpha=alpha, tm=PEER_TOKEN_TILE,
                      rows_per_step=PEER_ROWS_PER_STEP)
    return out.reshape(bsz, seq, d)
```

```python
import functools
import math

import jax
import jax.numpy as jnp
from jax import lax
from jax.experimental import pallas as pl
from jax.experimental.pallas import tpu as pltpu

TOPK = 16
LN_EPS = 1e-5
I_GATE_PAD = -1e30
MLSTM_CHUNK = 256
MIX_TILE = 256
INPROJ_TILE_M = 1024
INPROJ_TILE_N = 1024
PEER_TOKEN_TILE = 512
PEER_ROWS_PER_STEP = 8
QUANT_ROWS = 512
LANES = 128
BF16_SUBLANES = 16
VMEM_LIMIT_BYTES = 56 * 1024 * 1024
LARGE_VMEM_LIMIT_BYTES = 58 * 1024 * 1024
FP8 = jnp.float8_e4m3fn
FP8_CLIP = 448.0
FP8_TARGET = 256.0
PEER_P_SCALE = 64.0


def _gelu_exact(x):
    return 0.5 * x * (1.0 + lax.erf(x * (2.0 ** -0.5)))


def _ln_rows(x, g, b):
    mu = jnp.mean(x, axis=1, keepdims=True)
    xc = x - mu
    var = jnp.mean(xc * xc, axis=1, keepdims=True)
    return xc * lax.rsqrt(var + LN_EPS) * g + b


def _inproj_kernel(x_ref, g_ref, b_ref, wh_ref, wt_ref, wg_ref, o_ref, og_ref, ogt_ref, h_ref,
                   *, n_head_blocks):
    j = pl.program_id(1)

    @pl.when(j == 0)
    def _():
        h = _ln_rows(x_ref[...], g_ref[...], b_ref[...]).astype(h_ref.dtype)
        h_ref[...] = h
        gates = jnp.dot(h, wg_ref[...], preferred_element_type=jnp.float32)
        og_ref[...] = gates
        ogt_ref[...] = gates.T

    @pl.when(j < n_head_blocks)
    def _():
        o_ref[...] = jnp.dot(h_ref[...], wh_ref[...].astype(h_ref.dtype),
                             preferred_element_type=jnp.float32).astype(o_ref.dtype)

    @pl.when(j >= n_head_blocks)
    def _():
        o_ref[...] = jnp.dot(h_ref[...], wt_ref[...],
                             preferred_element_type=jnp.float32).astype(o_ref.dtype)


def _inproj(x2, ln_g, ln_b, w_all, head_cols, w_tail, w_gate, tm, tn):
    m, d = x2.shape
    n = head_cols + w_tail.shape[1]
    tm = min(tm, m)
    assert m % tm == 0 and head_cols % tn == 0 and w_tail.shape[1] % tn == 0
    assert w_gate.shape[1] == LANES
    nh = head_cols // tn
    const = lambda shape: pl.BlockSpec(shape, lambda i, j: (0, 0))
    return pl.pallas_call(
        functools.partial(_inproj_kernel, n_head_blocks=nh),
        grid=(m // tm, n // tn),
        in_specs=[pl.BlockSpec((tm, d), lambda i, j: (i, 0)),
                  const((1, d)), const((1, d)),
                  pl.BlockSpec((d, tn), lambda i, j: (0, jnp.minimum(j, nh - 1))),
                  pl.BlockSpec((d, tn), lambda i, j: (0, jnp.maximum(j - nh, 0))),
                  const((d, LANES))],
        out_specs=[pl.BlockSpec((tm, tn), lambda i, j: (i, j)),
                   pl.BlockSpec((tm, LANES), lambda i, j: (i, 0)),
                   pl.BlockSpec((LANES, tm), lambda i, j: (0, i))],
        out_shape=[jax.ShapeDtypeStruct((m, n), jnp.bfloat16),
                   jax.ShapeDtypeStruct((m, LANES), jnp.float32),
                   jax.ShapeDtypeStruct((LANES, m), jnp.float32)],
        scratch_shapes=[pltpu.VMEM((tm, d), jnp.bfloat16)],
        compiler_params=pltpu.CompilerParams(
            dimension_semantics=("parallel", "arbitrary"),
            vmem_limit_bytes=LARGE_VMEM_LIMIT_BYTES),
        name="inproj",
    )(x2, ln_g[None, :], ln_b[None, :], w_all, w_tail, w_gate)


def _mm_kernel(a_ref, b_ref, o_ref):
    bf = jnp.bfloat16
    o_ref[...] = jnp.dot(a_ref[...].astype(bf), b_ref[...].astype(bf),
                         preferred_element_type=jnp.float32).astype(o_ref.dtype)


def _matmul(a, b, out_dtype, tm, tn):
    m, k = a.shape
    _, n = b.shape
    tm = min(tm, m)
    tn = min(tn, n)
    assert m % tm == 0 and n % tn == 0
    return pl.pallas_call(
        _mm_kernel,
        grid=(m // tm, n // tn),
        in_specs=[pl.BlockSpec((tm, k), lambda i, j: (i, 0)),
                  pl.BlockSpec((k, tn), lambda i, j: (0, j))],
        out_specs=pl.BlockSpec((tm, tn), lambda i, j: (i, j)),
        out_shape=jax.ShapeDtypeStruct((m, n), out_dtype),
        compiler_params=pltpu.CompilerParams(
            dimension_semantics=("parallel", "parallel"),
            vmem_limit_bytes=VMEM_LIMIT_BYTES),
        name="matmul",
    )(a, b)


def _pow2_scale(amax):
    safe = jnp.where(amax > 0, amax, FP8_TARGET)
    return jnp.exp2(jnp.floor(jnp.log2(FP8_TARGET / safe)))


def _quantize_rows_kernel(x_ref, q_ref, inv_ref, *, transpose):
    x = x_ref[...]
    scale = _pow2_scale(jnp.max(jnp.abs(x), axis=1, keepdims=True))
    xs = x * scale
    q_ref[...] = (xs.T if transpose else xs).astype(q_ref.dtype)
    inv_ref[...] = jnp.broadcast_to(1.0 / scale, inv_ref.shape)


def _quantize_rows(x, rows, transpose):
    n, d = x.shape
    rows = min(rows, n)
    assert n % rows == 0
    q_shape, q_spec = (((d, n), pl.BlockSpec((d, rows), lambda i: (0, i))) if transpose
                       else ((n, d), pl.BlockSpec((rows, d), lambda i: (i, 0))))
    return pl.pallas_call(
        functools.partial(_quantize_rows_kernel, transpose=transpose),
        grid=(n // rows,),
        in_specs=[pl.BlockSpec((rows, d), lambda i: (i, 0))],
        out_specs=[q_spec, pl.BlockSpec((rows, LANES), lambda i: (i, 0))],
        out_shape=[jax.ShapeDtypeStruct(q_shape, FP8),
                   jax.ShapeDtypeStruct((n, LANES), jnp.float32)],
        compiler_params=pltpu.CompilerParams(
            dimension_semantics=("parallel",), vmem_limit_bytes=VMEM_LIMIT_BYTES),
        name="quantize_rows",
    )(x)


def _peer_dense_kernel(ht_ref, e0_ref, thr_ref, e1_ref, rank_ref, u_ref, vt_ref, c1_ref, c2_ref,
                       post_ref, g_ref, b_ref, o_ref, hb_ref, sh_ref, acc_ref, w_ref, p_ref,
                       *, n_heads, rows_per_step, alpha):
    k = pl.program_id(1)
    bf, f8 = jnp.bfloat16, FP8
    tm = ht_ref.shape[1]
    zero = jnp.zeros((), bf)

    def row_tile(ref, h, ii):
        row = jnp.broadcast_to(ref[h, ii:ii + 1, :], (BF16_SUBLANES, tm)).astype(bf)
        return pltpu.repeat(row, LANES // BF16_SUBLANES, axis=0)

    @pl.when(k == 0)
    def _():
        acc_ref[...] = jnp.zeros_like(acc_ref)
        h = ht_ref[...]
        amax = jnp.max(jnp.max(jnp.abs(h), axis=0, keepdims=True), axis=1, keepdims=True)
        s_h = _pow2_scale(amax)
        hb_ref[...] = (h * s_h).astype(f8)
        sh_ref[...] = jnp.broadcast_to(1.0 / s_h, sh_ref.shape)

    for ii in range(rows_per_step):
        w = jnp.zeros((LANES, tm), bf)
        for h in range(n_heads):
            sel = rank_ref[h] < row_tile(thr_ref, h, ii)
            w = w + jnp.where(sel, e1_ref[h], zero) * row_tile(e0_ref, h, ii)
        w_ref[ii * LANES:(ii + 1) * LANES, :] = w
    a = jnp.dot(u_ref[...], hb_ref[...], preferred_element_type=jnp.float32)
    inv_sh = sh_ref[0:1, 0:1]
    c1 = pltpu.repeat(c1_ref[...] * inv_sh, tm // LANES, axis=1)
    c2 = pltpu.repeat(c2_ref[...] * inv_sh, tm // LANES, axis=1)
    act = ((a * c1) * (1.0 + lax.erf(a * c2))).astype(bf)
    p = jnp.clip(w_ref[...] * act, -FP8_CLIP, FP8_CLIP)
    p_ref[...] = p.astype(f8)
    acc_ref[...] += jnp.dot(vt_ref[...], p_ref[...], preferred_element_type=jnp.float32)

    @pl.when(k == pl.num_programs(1) - 1)
    def _():
        y = alpha * ht_ref[...] + acc_ref[...] * post_ref[0:1, 0:1]
        mu = jnp.mean(y, axis=0, keepdims=True)
        yc = y - mu
        var = jnp.mean(yc * yc, axis=0, keepdims=True)
        o_ref[...] = (yc * lax.rsqrt(var + LN_EPS)).T * g_ref[...] + b_ref[...]


def _peer_dense(ht, e0, thr, e1, rank, peer_u, peer_v, ln_g, ln_b, *, alpha, tm, rows_per_step):
    u, inv_u = _quantize_rows(peer_u, QUANT_ROWS, transpose=False)
    vt, inv_v = _quantize_rows(peer_v, QUANT_ROWS, transpose=True)
    inv_v_max = jnp.max(inv_v)
    c1 = inv_u * (inv_v / inv_v_max) * (0.5 * PEER_P_SCALE)
    c2 = inv_u * (2.0 ** -0.5)
    post = jnp.broadcast_to(inv_v_max / PEER_P_SCALE, (1, LANES)).astype(jnp.float32)
    d, t = ht.shape
    n_heads, n_keys, _ = e0.shape
    assert n_keys == LANES
    n_experts = u.shape[0]
    eb = rows_per_step * LANES
    sel_spec = pl.BlockSpec((n_heads, n_keys, tm), lambda i, k: (0, 0, i))
    row_spec = pl.BlockSpec((n_heads, rows_per_step, tm), lambda i, k: (0, k, i))
    vec_spec = pl.BlockSpec((1, d), lambda i, k: (0, 0))
    return pl.pallas_call(
        functools.partial(_peer_dense_kernel, n_heads=n_heads, rows_per_step=rows_per_step,
                          alpha=alpha),
        grid=(t // tm, n_experts // eb),
        in_specs=[pl.BlockSpec((d, tm), lambda i, k: (0, i)),
                  row_spec, row_spec, sel_spec, sel_spec,
                  pl.BlockSpec((eb, d), lambda i, k: (k, 0)),
                  pl.BlockSpec((d, eb), lambda i, k: (0, k)),
                  pl.BlockSpec((eb, LANES), lambda i, k: (k, 0)),
                  pl.BlockSpec((eb, LANES), lambda i, k: (k, 0)),
                  pl.BlockSpec((1, LANES), lambda i, k: (0, 0)),
                  vec_spec, vec_spec],
        out_specs=pl.BlockSpec((tm, d), lambda i, k: (i, 0)),
        out_shape=jax.ShapeDtypeStruct((t, d), jnp.float32),
        scratch_shapes=[pltpu.VMEM((d, tm), FP8),
                        pltpu.VMEM((8, LANES), jnp.float32),
                        pltpu.VMEM((d, tm), jnp.float32),
                        pltpu.VMEM((eb, tm), jnp.bfloat16),
                        pltpu.VMEM((eb, tm), FP8)],
        compiler_params=pltpu.CompilerParams(
            dimension_semantics=("parallel", "arbitrary"),
            vmem_limit_bytes=LARGE_VMEM_LIMIT_BYTES),
        name="peer_dense",
    )(ht, e0, thr, e1, rank, u, vt, c1, c2, post, ln_g[None, :], ln_b[None, :])


def _fold_keys_kernel(keys_ref, wq_ref, o_ref):
    o_ref[...] = lax.dot_general(
        keys_ref[0], wq_ref[...], (((1,), (1,)), ((), ())),
        precision=lax.Precision.HIGHEST, preferred_element_type=jnp.float32)


def _fold_keys(keys, wq):
    n_blocks, n_keys, half_key = keys.shape
    d = wq.shape[0]
    return pl.pallas_call(
        _fold_keys_kernel,
        grid=(n_blocks,),
        in_specs=[pl.BlockSpec((1, n_keys, half_key), lambda b: (b, 0, 0)),
                  pl.BlockSpec((d, half_key), lambda b: (0, b))],
        out_specs=pl.BlockSpec((n_keys, d), lambda b: (b, 0)),
        out_shape=jax.ShapeDtypeStruct((n_blocks * n_keys, d), jnp.float32),
        compiler_params=pltpu.CompilerParams(dimension_semantics=("parallel",)),
        name="fold_keys",
    )(keys, wq)


def _batcher_pairs(n):
    pairs = []
    p = 1
    while p < n:
        k = p
        while k >= 1:
            for j in range(k % p, n - k, 2 * k):
                for i in range(min(k, n - j - k)):
                    if (i + j) // (2 * p) == (i + j + k) // (2 * p):
                        pairs.append((i + j, i + j + k))
            k //= 2
        p *= 2
    return pairs


_SORT_PAIRS = _batcher_pairs(TOPK)
_MERGE_PAIRS = [(i, i + d) for d in (8, 4, 2, 1) for i in range(TOPK) if not i & d]


def _compare_exchange(v, pairs):
    v = list(v)
    for i, j in pairs:
        v[i], v[j] = jnp.maximum(v[i], v[j]), jnp.minimum(v[i], v[j])
    return v


def _top16(tiles):
    v = _compare_exchange(tiles, _SORT_PAIRS)
    for shift in (4, 2, 1):
        other = [pltpu.roll(x, shift, 0) for x in v]
        v = [jnp.maximum(v[r], other[TOPK - 1 - r]) for r in range(TOPK)]
        v = _compare_exchange(v, _MERGE_PAIRS)
    return v


def _stack_rows(tiles, sub):
    out = tiles[0]
    for r in range(1, 8):
        out = jnp.where(sub == r, tiles[r], out)
    return out


def _peer_select_kernel(sc_ref, e0_ref, thr_ref, e1_ref, rank_ref, *, n_heads, n_keys):
    tl = sc_ref.shape[1]
    reps = n_keys // 8
    sub = lax.broadcasted_iota(jnp.int32, (8, tl), 0)
    spread = lambda tile: pltpu.repeat(tile, reps, axis=0)

    def head_body(h, carry):
        r0 = pl.multiple_of(h * 2 * n_keys, 2 * n_keys)
        s0 = sc_ref[pl.ds(r0, n_keys), :]
        s1 = sc_ref[pl.ds(r0 + n_keys, n_keys), :]
        a = _top16([s0[8 * r:8 * r + 8, :] for r in range(reps)])
        b = _top16([s1[8 * r:8 * r + 8, :] for r in range(reps)])
        b_lo, b_hi, a_hi = _stack_rows(b[:8], sub), _stack_rows(b[8:], sub), _stack_rows(a[8:], sub)
        cand = [a[0] + b_lo, a[0] + b_hi] + [a[p] + b_lo for p in range(1, 8)] + [a_hi + b[0]]
        cand += [jnp.full((8, tl), -jnp.inf, jnp.float32)] * (TOPK - len(cand))
        best = _top16(cand)
        tau = best[TOPK - 1]
        zsum = sum(jnp.exp(c - best[0]) for c in best)
        thr = jnp.zeros((n_keys, tl), jnp.float32)
        rank1 = jnp.full((n_keys, tl), 100.0, jnp.float32)
        for p in reversed(range(TOPK)):
            n_sel = (jnp.where(a[p] + b_lo >= tau, 1.0, 0.0)
                     + jnp.where(a[p] + b_hi >= tau, 1.0, 0.0))
            for shift in (4, 2, 1):
                n_sel = n_sel + pltpu.roll(n_sel, shift, 0)
            thr = jnp.where(s0 == spread(a[p]), spread(n_sel), thr)
            rank1 = jnp.where(s1 == spread(b[p]), float(p), rank1)
        e0_ref[h] = jnp.exp(s0 - spread(a[0])) / spread(zsum)
        thr_ref[h] = thr
        e1_ref[h] = jnp.exp(s1 - spread(b[0])).astype(e1_ref.dtype)
        rank_ref[h] = rank1.astype(rank_ref.dtype)
        return carry

    lax.fori_loop(0, n_heads, head_body, 0)


def _peer_select(scores_t, n_heads, n_keys, tl):
    rows, t = scores_t.shape
    assert rows == n_heads * 2 * n_keys and n_keys == 8 * TOPK
    out_spec = pl.BlockSpec((n_heads, n_keys, tl), lambda i: (0, 0, i))
    f32 = jax.ShapeDtypeStruct((n_heads, n_keys, t), jnp.float32)
    b16 = jax.ShapeDtypeStruct((n_heads, n_keys, t), jnp.bfloat16)
    return pl.pallas_call(
        functools.partial(_peer_select_kernel, n_heads=n_heads, n_keys=n_keys),
        grid=(t // tl,),
        in_specs=[pl.BlockSpec((rows, tl), lambda i: (0, i))],
        out_specs=[out_spec, out_spec, out_spec, out_spec],
        out_shape=[f32, f32, b16, b16],
        compiler_params=pltpu.CompilerParams(
            dimension_semantics=("parallel",), vmem_limit_bytes=VMEM_LIMIT_BYTES),
        name="peer_select",
    )(scores_t)


def _log_sigmoid(x):
    return jnp.minimum(x, 0.0) - jnp.log1p(jnp.exp(-jnp.abs(x)))


def _mlstm_kernel(q_ref, k_ref, v_ref, o_ref, g_ref, gt_ref,
                  qm_ref, km_ref, vm_ref, om_ref, gm_ref, gmt_ref,
                  bias_ref, biast_ref, nw_ref, y_ref, c_ref, m_ref,
                  *, n_heads, head_qk, head_v):
    c = pl.program_id(1)
    tc = q_ref.shape[0]
    f32, bf = jnp.float32, jnp.bfloat16
    first = c == 0

    @pl.when(first)
    def _():
        c_ref[...] = jnp.zeros_like(c_ref)
        m_ref[...] = jnp.zeros_like(m_ref)

    q = jnp.where(first, qm_ref[...], q_ref[...])
    k = jnp.where(first, km_ref[...], k_ref[...])
    v = jnp.where(first, vm_ref[...], v_ref[...])
    o = jnp.where(first, om_ref[...], o_ref[...])
    g = jnp.where(first, gm_ref[...], g_ref[...]) + bias_ref[...]
    gt = jnp.where(first, gmt_ref[...], gt_ref[...]) + biast_ref[...]

    row = lax.broadcasted_iota(jnp.int32, (tc, tc), 0)
    col = lax.broadcasted_iota(jnp.int32, (tc, tc), 1)
    causal = col <= row
    tri = jnp.where(causal, 1.0, 0.0).astype(f32)
    b_cols = jnp.dot(tri, _log_sigmoid(g), precision=lax.Precision.HIGHEST,
                     preferred_element_type=f32)
    b_rows = lax.dot_general(_log_sigmoid(gt), tri, (((1,), (1,)), ((), ())),
                             precision=lax.Precision.HIGHEST, preferred_element_type=f32)
    ln_scale = -0.5 * math.log(head_qk)
    ones_blk = jnp.where(lax.broadcasted_iota(jnp.int32, (tc, LANES), 1) == 0, 1.0, 0.0).astype(bf)

    for h in range(n_heads):
        qh = q[:, h * head_qk:(h + 1) * head_qk]
        kh = k[:, h * head_qk:(h + 1) * head_qk]
        v_ext = jnp.concatenate([v[:, h * head_v:(h + 1) * head_v], ones_blk], axis=1)
        li_col = g[:, h:h + 1]
        b_col = b_cols[:, n_heads + h:n_heads + h + 1]
        li_row = gt[h:h + 1, :]
        b_row = b_rows[n_heads + h:n_heads + h + 1, :]
        m_prev = m_ref[h:h + 1, 0:1]
        d = jnp.where(causal, b_col + (li_row - b_row), -jnp.inf)
        inter = b_col + m_prev
        m_t = jnp.maximum(inter, jnp.max(d, axis=1, keepdims=True))
        w = jnp.exp(d - (m_t - ln_scale))
        s = lax.dot_general(qh, kh, (((1,), (1,)), ((), ())), preferred_element_type=f32) * w
        sc_inter = jnp.exp(inter - (m_t - ln_scale))
        c_ext = c_ref[h]
        num_ext = (jnp.dot(s.astype(bf), v_ext, preferred_element_type=f32)
                   + sc_inter * jnp.dot(qh, c_ext.astype(bf), preferred_element_type=f32))
        den = num_ext[:, head_v:head_v + 1]
        hm = num_ext[:, :head_v] / jnp.maximum(jnp.abs(den), jnp.exp(-m_t))
        hm = hm * lax.rsqrt(jnp.mean(hm * hm, axis=1, keepdims=True) + LN_EPS)
        gate_o = jax.nn.sigmoid(o[:, h * head_v:(h + 1) * head_v].astype(f32))
        y_ref[:, h * head_v:(h + 1) * head_v] = (
            gate_o * (hm * nw_ref[:, h * head_v:(h + 1) * head_v])).astype(y_ref.dtype)
        b_last = b_col[tc - 1:tc, :]
        dl = b_last - b_col + li_col
        m_new = jnp.maximum(b_last + m_prev, jnp.max(dl, axis=0, keepdims=True))
        wv = (jnp.exp(dl - m_new) * v_ext.astype(f32)).astype(bf)
        c_ref[h] = (jnp.exp(b_last + m_prev - m_new) * c_ext
                    + lax.dot_general(kh, wv, (((0,), (0,)), ((), ())),
                                      preferred_element_type=f32))
        m_ref[h:h + 1, :] = jnp.broadcast_to(m_new, (1, LANES))


def _mlstm(proj_x, proj_m, gif_x, gt_x, gif_m, b_if, norm_w, *, bsz, seq, n_heads, head_qk,
           head_v, off_q, off_v):
    tc = min(MLSTM_CHUNK, seq)
    assert seq % tc == 0
    n_meta = proj_m.shape[0]
    d_mqk, d_mv = n_heads * head_qk, n_heads * head_v
    assert off_q % d_mqk == 0 and off_v % d_mv == 0 and n_meta <= tc
    nc_x = seq // tc
    bf = jnp.bfloat16
    lead = tc - n_meta
    pad_rows = lambda a: jnp.pad(a, ((lead, 0), (0, 0)))
    qm = pad_rows(proj_m[:, off_q:off_q + d_mqk])
    km = pad_rows(proj_m[:, off_q + d_mqk:off_q + 2 * d_mqk])
    vm = pad_rows(proj_m[:, off_v:off_v + d_mv])
    om = pad_rows(proj_m[:, off_v + d_mv:off_v + 2 * d_mv])
    pad_gate = jnp.concatenate([jnp.full((lead, n_heads), I_GATE_PAD, jnp.float32),
                                jnp.full((lead, n_heads), -I_GATE_PAD, jnp.float32),
                                jnp.zeros((lead, LANES - 2 * n_heads), jnp.float32)], axis=1)
    gm = jnp.concatenate([pad_gate, gif_m], axis=0)
    bias = jnp.pad(b_if, (0, LANES - 2 * n_heads))[None, :]
    biast = jnp.broadcast_to(b_if[:, None], (2 * n_heads, tc))
    gt_m = gm[:, :2 * n_heads].T

    xrow = lambda b, c: b * nc_x + jnp.maximum(c - 1, 0)
    const = lambda b, c: (0, 0)
    return pl.pallas_call(
        functools.partial(_mlstm_kernel, n_heads=n_heads, head_qk=head_qk, head_v=head_v),
        grid=(bsz, nc_x + 1),
        in_specs=[pl.BlockSpec((tc, d_mqk), lambda b, c: (xrow(b, c), off_q // d_mqk)),
                  pl.BlockSpec((tc, d_mqk), lambda b, c: (xrow(b, c), off_q // d_mqk + 1)),
                  pl.BlockSpec((tc, d_mv), lambda b, c: (xrow(b, c), off_v // d_mv)),
                  pl.BlockSpec((tc, d_mv), lambda b, c: (xrow(b, c), off_v // d_mv + 1)),
                  pl.BlockSpec((tc, LANES), lambda b, c: (xrow(b, c), 0)),
                  pl.BlockSpec((2 * n_heads, tc), lambda b, c: (0, xrow(b, c))),
                  pl.BlockSpec((tc, d_mqk), const), pl.BlockSpec((tc, d_mqk), const),
                  pl.BlockSpec((tc, d_mv), const), pl.BlockSpec((tc, d_mv), const),
                  pl.BlockSpec((tc, LANES), const), pl.BlockSpec((2 * n_heads, tc), const),
                  pl.BlockSpec((1, LANES), const), pl.BlockSpec((2 * n_heads, tc), const),
                  pl.BlockSpec((1, d_mv), const)],
        out_specs=pl.BlockSpec((tc, d_mv), lambda b, c: (xrow(b, c), 0)),
        out_shape=jax.ShapeDtypeStruct((bsz * seq, d_mv), bf),
        scratch_shapes=[pltpu.VMEM((n_heads, head_qk, head_v + LANES), jnp.float32),
                        pltpu.VMEM((n_heads, LANES), jnp.float32)],
        compiler_params=pltpu.CompilerParams(
            dimension_semantics=("parallel", "arbitrary"),
            vmem_limit_bytes=VMEM_LIMIT_BYTES),
        name="mlstm",
    )(proj_x, proj_x, proj_x, proj_x, gif_x, gt_x, qm, km, vm, om, gm, gt_m,
      bias, biast, norm_w[None, :])


def _mix_kernel(x_ref, cb_ref, cc_ref, cx_ref, cch_ref, cxh_ref, ccm_ref, cxm_ref,
                yb_ref, ga_ref, gb_ref, pa_ref, pb_ref, wo_ref, cw_ref,
                g0_ref, b0_ref, g1_ref, b1_ref, h1t_ref,
                *, tiles_per_batch, alpha):
    i = pl.program_id(0)
    f32, bf = jnp.float32, jnp.bfloat16
    tm = x_ref.shape[0]
    z = cc_ref[...].astype(f32) * cx_ref[...].astype(f32)
    batch_start = (i % tiles_per_batch) == 0
    halo = jnp.where(batch_start,
                     ccm_ref[...].astype(f32) * cxm_ref[...].astype(f32),
                     cch_ref[...].astype(f32) * cxh_ref[...].astype(f32))
    sub = lax.broadcasted_iota(jnp.int32, halo.shape, 0)
    n_halo = halo.shape[0]
    y = cw_ref[2:3, :] * z
    for shift in (1, 2):
        rolled = pltpu.roll(z, shift, 0)
        head = jnp.where(sub < shift, pltpu.roll(halo, shift, 0), rolled[:n_halo])
        y = y + cw_ref[2 - shift:3 - shift, :] * jnp.concatenate([head, rolled[n_halo:]], axis=0)
    y_a = (cb_ref[...].astype(f32) * y).astype(bf)
    za = jnp.dot(y_a, pa_ref[...], preferred_element_type=f32)
    zb = jnp.dot(yb_ref[...], pb_ref[...], preferred_element_type=f32)
    merged = (jax.nn.sigmoid(ga_ref[...].astype(f32)) * za
              + jax.nn.sigmoid(gb_ref[...].astype(f32)) * zb).astype(bf)
    mix = jnp.dot(merged, wo_ref[...], preferred_element_type=f32)
    h0 = _ln_rows(x_ref[...], g0_ref[...], b0_ref[...])
    h1t_ref[...] = _ln_rows(alpha * h0 + mix, g1_ref[...], b1_ref[...]).T


def _mix(x2, proj_x, proj_m, y_b, p_a, p_b, w_o, conv_w, ln0_g, ln0_b, ln1_g, ln1_b,
         *, seq, off_g, alpha, tm):
    t_x, d = x2.shape
    d_conv = conv_w.shape[-1]
    d_mv = y_b.shape[-1]
    n_meta = proj_m.shape[0]
    halo = 8
    assert seq % tm == 0 and tm % halo == 0 and n_meta % halo == 0 and off_g % d == 0
    assert conv_w.shape[0] == 3
    tile = lambda j: pl.BlockSpec((tm, d_conv), lambda i: (i, j))
    prev = lambda j: pl.BlockSpec((halo, d_conv),
                                  lambda i: (jnp.maximum(i * (tm // halo) - 1, 0), j))
    meta = lambda j: pl.BlockSpec((halo, d_conv), lambda i: (n_meta // halo - 1, j))
    const = lambda shape: pl.BlockSpec(shape, lambda i: (0, 0), pipeline_mode=pl.Buffered(1))
    vec = lambda a: a[None, :]
    return pl.pallas_call(
        functools.partial(_mix_kernel, tiles_per_batch=seq // tm, alpha=alpha),
        grid=(t_x // tm,),
        in_specs=[pl.BlockSpec((tm, d), lambda i: (i, 0)),
                  tile(0), tile(1), tile(2), prev(1), prev(2), meta(1), meta(2),
                  pl.BlockSpec((tm, d_mv), lambda i: (i, 0)),
                  pl.BlockSpec((tm, d), lambda i: (i, off_g // d)),
                  pl.BlockSpec((tm, d), lambda i: (i, off_g // d + 1)),
                  const(p_a.shape), const(p_b.shape), const(w_o.shape), const(conv_w.shape),
                  const((1, d)), const((1, d)), const((1, d)), const((1, d))],
        out_specs=pl.BlockSpec((d, tm), lambda i: (0, i)),
        out_shape=jax.ShapeDtypeStruct((d, t_x), jnp.float32),
        compiler_params=pltpu.CompilerParams(
            dimension_semantics=("parallel",), vmem_limit_bytes=VMEM_LIMIT_BYTES),
        name="mix",
    )(x2, proj_x, proj_x, proj_x, proj_x, proj_x, proj_m, proj_m, y_b, proj_x, proj_x,
      p_a, p_b, w_o, conv_w, vec(ln0_g), vec(ln0_b), vec(ln1_g), vec(ln1_b))


def kernel(x, meta_tokens, ln0_g, ln0_b, w_in, b_if, conv_w, mh_norm_w, p_a, p_b, w_o,
           ln1_g, ln1_b, peer_wq, peer_keys, peer_u, peer_v, ln2_g, ln2_b):
    depth = w_in.shape[0]
    assert depth == 1
    bsz, seq, d = x.shape
    n_meta = meta_tokens.shape[0]
    n_in = w_in.shape[-1]
    d_conv = conv_w.shape[-1]
    d_mv = mh_norm_w.shape[-1]
    n_mh = b_if.shape[-1] // 2
    d_mqk = (n_in - 3 * d_conv - 2 * d_mv - 2 * n_mh - 2 * d) // 2
    head_qk = d_mqk // n_mh
    head_v = d_mv // n_mh
    alpha = (2 * depth) ** 0.25
    bf = jnp.bfloat16
    t_x = bsz * seq

    x2 = x.reshape(t_x, d)
    gate_off = 3 * d_conv + 2 * d_mqk + 2 * d_mv
    w = w_in[0]
    w_tail = w[:, gate_off + 2 * n_mh:].astype(bf)
    w_gif = jnp.pad(w[:, gate_off:gate_off + 2 * n_mh], ((0, 0), (0, LANES - 2 * n_mh))).astype(bf)
    inproj_tn = min(INPROJ_TILE_N, w_tail.shape[1])
    proj_x, gif_x, gt_x = _inproj(x2, ln0_g, ln0_b, w, gate_off, w_tail, w_gif,
                                  INPROJ_TILE_M, inproj_tn)
    proj_m, gif_m, _ = _inproj(meta_tokens, ln0_g, ln0_b, w, gate_off, w_tail, w_gif,
                               INPROJ_TILE_M, inproj_tn)

    off_q = 3 * d_conv
    off_v = off_q + 2 * d_mqk
    off_g = off_v + 2 * d_mv
    y_b = _mlstm(proj_x, proj_m, gif_x, gt_x, gif_m, b_if[0], mh_norm_w[0], bsz=bsz, seq=seq,
                 n_heads=n_mh, head_qk=head_qk, head_v=head_v, off_q=off_q, off_v=off_v)
    h1_t = _mix(x2, proj_x, proj_m, y_b,
                p_a[0].astype(bf), p_b[0].astype(bf), w_o[0].astype(bf), conv_w[0],
                ln0_g, ln0_b, ln1_g[0], ln1_b[0], seq=seq, off_g=off_g, alpha=alpha,
                tm=MIX_TILE)

    n_ph, _, n_keys, half_key = peer_keys.shape[1:]
    w_fold = _fold_keys(peer_keys[0].reshape(n_ph * 2, n_keys, half_key), peer_wq[0])
    scores_t = _matmul(w_fold, h1_t, jnp.float32, 1024, 512)
    e0, thr, e1, rank = _peer_select(scores_t, n_ph, n_keys, LANES)
    out = _peer_dense(h1_t, e0, thr, e1, rank, peer_u[0], peer_v[0],
                      ln2_g[0], ln2_b[0], alpha=alpha, tm=PEER_TOKEN_TILE,
                      rows_per_step=PEER_ROWS_PER_STEP)
    return out.reshape(bsz, seq, d)
```

```python
import functools
import math

import jax
import jax.numpy as jnp
from jax import lax
from jax.experimental import pallas as pl
from jax.experimental.pallas import tpu as pltpu

TOPK = 16
LN_EPS = 1e-5
I_GATE_PAD = -1e30
MLSTM_CHUNK = 256
MIX_TILE = 256
INPROJ_TILE_M = 1024
INPROJ_TILE_N = 1024
PEER_TOKEN_TILE = 512
PEER_ROWS_PER_STEP = 8
QUANT_ROWS = 512
LANES = 128
BF16_SUBLANES = 16
VMEM_LIMIT_BYTES = 56 * 1024 * 1024
LARGE_VMEM_LIMIT_BYTES = 58 * 1024 * 1024
FP8 = jnp.float8_e4m3fn
FP8_CLIP = 448.0
FP8_TARGET = 256.0
PEER_P_SCALE = 64.0


def _gelu_exact(x):
    return 0.5 * x * (1.0 + lax.erf(x * (2.0 ** -0.5)))


def _ln_rows(x, g, b):
    mu = jnp.mean(x, axis=1, keepdims=True)
    xc = x - mu
    var = jnp.mean(xc * xc, axis=1, keepdims=True)
    return xc * lax.rsqrt(var + LN_EPS) * g + b


def _inproj_kernel(x_ref, g_ref, b_ref, w_ref, wg_ref, o_ref, og_ref, ogt_ref, h_ref):
    @pl.when(pl.program_id(1) == 0)
    def _():
        h = _ln_rows(x_ref[...], g_ref[...], b_ref[...]).astype(h_ref.dtype)
        h_ref[...] = h
        gates = jnp.dot(h, wg_ref[...], preferred_element_type=jnp.float32)
        og_ref[...] = gates
        ogt_ref[...] = gates.T

    o_ref[...] = jnp.dot(h_ref[...], w_ref[...],
                         preferred_element_type=jnp.float32).astype(o_ref.dtype)


def _inproj(x2, ln_g, ln_b, w_main, w_gate, tm, tn):
    m, d = x2.shape
    n = w_main.shape[1]
    tm = min(tm, m)
    assert m % tm == 0 and n % tn == 0 and w_gate.shape[1] == LANES
    const = lambda shape: pl.BlockSpec(shape, lambda i, j: (0, 0))
    return pl.pallas_call(
        _inproj_kernel,
        grid=(m // tm, n // tn),
        in_specs=[pl.BlockSpec((tm, d), lambda i, j: (i, 0)),
                  const((1, d)), const((1, d)),
                  pl.BlockSpec((d, tn), lambda i, j: (0, j)),
                  const((d, LANES))],
        out_specs=[pl.BlockSpec((tm, tn), lambda i, j: (i, j)),
                   pl.BlockSpec((tm, LANES), lambda i, j: (i, 0)),
                   pl.BlockSpec((LANES, tm), lambda i, j: (0, i))],
        out_shape=[jax.ShapeDtypeStruct((m, n), jnp.bfloat16),
                   jax.ShapeDtypeStruct((m, LANES), jnp.float32),
                   jax.ShapeDtypeStruct((LANES, m), jnp.float32)],
        scratch_shapes=[pltpu.VMEM((tm, d), jnp.bfloat16)],
        compiler_params=pltpu.CompilerParams(
            dimension_semantics=("parallel", "arbitrary"),
            vmem_limit_bytes=VMEM_LIMIT_BYTES),
        name="inproj",
    )(x2, ln_g[None, :], ln_b[None, :], w_main, w_gate)


def _mm_kernel(a_ref, b_ref, o_ref):
    bf = jnp.bfloat16
    o_ref[...] = jnp.dot(a_ref[...].astype(bf), b_ref[...].astype(bf),
                         preferred_element_type=jnp.float32).astype(o_ref.dtype)


def _matmul(a, b, out_dtype, tm, tn):
    m, k = a.shape
    _, n = b.shape
    tm = min(tm, m)
    tn = min(tn, n)
    assert m % tm == 0 and n % tn == 0
    return pl.pallas_call(
        _mm_kernel,
        grid=(m // tm, n // tn),
        in_specs=[pl.BlockSpec((tm, k), lambda i, j: (i, 0)),
                  pl.BlockSpec((k, tn), lambda i, j: (0, j))],
        out_specs=pl.BlockSpec((tm, tn), lambda i, j: (i, j)),
        out_shape=jax.ShapeDtypeStruct((m, n), out_dtype),
        compiler_params=pltpu.CompilerParams(
            dimension_semantics=("parallel", "parallel"),
            vmem_limit_bytes=VMEM_LIMIT_BYTES),
        name="matmul",
    )(a, b)


def _pow2_scale(amax):
    safe = jnp.where(amax > 0, amax, FP8_TARGET)
    return jnp.exp2(jnp.floor(jnp.log2(FP8_TARGET / safe)))


def _quantize_rows_kernel(x_ref, q_ref, inv_ref, *, transpose):
    x = x_ref[...]
    scale = _pow2_scale(jnp.max(jnp.abs(x), axis=1, keepdims=True))
    xs = x * scale
    q_ref[...] = (xs.T if transpose else xs).astype(q_ref.dtype)
    inv_ref[...] = jnp.broadcast_to(1.0 / scale, inv_ref.shape)


def _quantize_rows(x, rows, transpose):
    n, d = x.shape
    rows = min(rows, n)
    assert n % rows == 0
    q_shape, q_spec = (((d, n), pl.BlockSpec((d, rows), lambda i: (0, i))) if transpose
                       else ((n, d), pl.BlockSpec((rows, d), lambda i: (i, 0))))
    return pl.pallas_call(
        functools.partial(_quantize_rows_kernel, transpose=transpose),
        grid=(n // rows,),
        in_specs=[pl.BlockSpec((rows, d), lambda i: (i, 0))],
        out_specs=[q_spec, pl.BlockSpec((rows, LANES), lambda i: (i, 0))],
        out_shape=[jax.ShapeDtypeStruct(q_shape, FP8),
                   jax.ShapeDtypeStruct((n, LANES), jnp.float32)],
        compiler_params=pltpu.CompilerParams(
            dimension_semantics=("parallel",), vmem_limit_bytes=VMEM_LIMIT_BYTES),
        name="quantize_rows",
    )(x)


def _peer_dense_kernel(ht_ref, e0_ref, thr_ref, e1_ref, rank_ref, u_ref, vt_ref, c1_ref, c2_ref,
                       post_ref, g_ref, b_ref, o_ref, hb_ref, sh_ref, acc_ref, w_ref, p_ref,
                       *, n_heads, rows_per_step, alpha):
    k = pl.program_id(1)
    bf, f8 = jnp.bfloat16, FP8
    tm = ht_ref.shape[1]
    zero = jnp.zeros((), bf)

    def row_tile(ref, h, ii):
        row = jnp.broadcast_to(ref[h, ii:ii + 1, :], (BF16_SUBLANES, tm)).astype(bf)
        return pltpu.repeat(row, LANES // BF16_SUBLANES, axis=0)

    @pl.when(k == 0)
    def _():
        acc_ref[...] = jnp.zeros_like(acc_ref)
        h = ht_ref[...]
        amax = jnp.max(jnp.max(jnp.abs(h), axis=0, keepdims=True), axis=1, keepdims=True)
        s_h = _pow2_scale(amax)
        hb_ref[...] = (h * s_h).astype(f8)
        sh_ref[...] = jnp.broadcast_to(1.0 / s_h, sh_ref.shape)

    for ii in range(rows_per_step):
        w = jnp.zeros((LANES, tm), bf)
        for h in range(n_heads):
            sel = rank_ref[h] < row_tile(thr_ref, h, ii)
            w = w + jnp.where(sel, e1_ref[h], zero) * row_tile(e0_ref, h, ii)
        w_ref[ii * LANES:(ii + 1) * LANES, :] = w
    a = jnp.dot(u_ref[...], hb_ref[...], preferred_element_type=jnp.float32)
    inv_sh = sh_ref[0:1, 0:1]
    c1 = pltpu.repeat((c1_ref[...] * inv_sh).astype(bf), tm // LANES, axis=1)
    c2 = pltpu.repeat((c2_ref[...] * inv_sh).astype(bf), tm // LANES, axis=1)
    ab = a.astype(bf)
    act = (ab * c1) * (1.0 + lax.erf(ab * c2))
    p = jnp.clip(w_ref[...] * act, -FP8_CLIP, FP8_CLIP)
    p_ref[...] = p.astype(f8)
    acc_ref[...] += jnp.dot(vt_ref[...], p_ref[...], preferred_element_type=jnp.float32)

    @pl.when(k == pl.num_programs(1) - 1)
    def _():
        y = alpha * ht_ref[...] + acc_ref[...] * post_ref[0:1, 0:1]
        mu = jnp.mean(y, axis=0, keepdims=True)
        yc = y - mu
        var = jnp.mean(yc * yc, axis=0, keepdims=True)
        o_ref[...] = (yc * lax.rsqrt(var + LN_EPS)).T * g_ref[...] + b_ref[...]


def _peer_dense(ht, e0, thr, e1, rank, peer_u, peer_v, ln_g, ln_b, *, alpha, tm, rows_per_step):
    u, inv_u = _quantize_rows(peer_u, QUANT_ROWS, transpose=False)
    vt, inv_v = _quantize_rows(peer_v, QUANT_ROWS, transpose=True)
    inv_v_max = jnp.max(inv_v)
    c1 = inv_u * (inv_v / inv_v_max) * (0.5 * PEER_P_SCALE)
    c2 = inv_u * (2.0 ** -0.5)
    post = jnp.broadcast_to(inv_v_max / PEER_P_SCALE, (1, LANES)).astype(jnp.float32)
    d, t = ht.shape
    n_heads, n_keys, _ = e0.shape
    assert n_keys == LANES
    n_experts = u.shape[0]
    eb = rows_per_step * LANES
    sel_spec = pl.BlockSpec((n_heads, n_keys, tm), lambda i, k: (0, 0, i))
    row_spec = pl.BlockSpec((n_heads, rows_per_step, tm), lambda i, k: (0, k, i))
    vec_spec = pl.BlockSpec((1, d), lambda i, k: (0, 0))
    return pl.pallas_call(
        functools.partial(_peer_dense_kernel, n_heads=n_heads, rows_per_step=rows_per_step,
                          alpha=alpha),
        grid=(t // tm, n_experts // eb),
        in_specs=[pl.BlockSpec((d, tm), lambda i, k: (0, i)),
                  row_spec, row_spec, sel_spec, sel_spec,
                  pl.BlockSpec((eb, d), lambda i, k: (k, 0)),
                  pl.BlockSpec((d, eb), lambda i, k: (0, k)),
                  pl.BlockSpec((eb, LANES), lambda i, k: (k, 0)),
                  pl.BlockSpec((eb, LANES), lambda i, k: (k, 0)),
                  pl.BlockSpec((1, LANES), lambda i, k: (0, 0)),
                  vec_spec, vec_spec],
        out_specs=pl.BlockSpec((tm, d), lambda i, k: (i, 0)),
        out_shape=jax.ShapeDtypeStruct((t, d), jnp.float32),
        scratch_shapes=[pltpu.VMEM((d, tm), FP8),
                        pltpu.VMEM((8, LANES), jnp.float32),
                        pltpu.VMEM((d, tm), jnp.float32),
                        pltpu.VMEM((eb, tm), jnp.bfloat16),
                        pltpu.VMEM((eb, tm), FP8)],
        compiler_params=pltpu.CompilerParams(
            dimension_semantics=("parallel", "arbitrary"),
            vmem_limit_bytes=LARGE_VMEM_LIMIT_BYTES),
        name="peer_dense",
    )(ht, e0, thr, e1, rank, u, vt, c1, c2, post, ln_g[None, :], ln_b[None, :])


def _fold_keys_kernel(keys_ref, wq_ref, o_ref):
    o_ref[...] = lax.dot_general(
        keys_ref[0], wq_ref[...], (((1,), (1,)), ((), ())),
        precision=lax.Precision.HIGHEST, preferred_element_type=jnp.float32)


def _fold_keys(keys, wq):
    n_blocks, n_keys, half_key = keys.shape
    d = wq.shape[0]
    return pl.pallas_call(
        _fold_keys_kernel,
        grid=(n_blocks,),
        in_specs=[pl.BlockSpec((1, n_keys, half_key), lambda b: (b, 0, 0)),
                  pl.BlockSpec((d, half_key), lambda b: (0, b))],
        out_specs=pl.BlockSpec((n_keys, d), lambda b: (b, 0)),
        out_shape=jax.ShapeDtypeStruct((n_blocks * n_keys, d), jnp.float32),
        compiler_params=pltpu.CompilerParams(dimension_semantics=("parallel",)),
        name="fold_keys",
    )(keys, wq)


def _batcher_pairs(n):
    pairs = []
    p = 1
    while p < n:
        k = p
        while k >= 1:
            for j in range(k % p, n - k, 2 * k):
                for i in range(min(k, n - j - k)):
                    if (i + j) // (2 * p) == (i + j + k) // (2 * p):
                        pairs.append((i + j, i + j + k))
            k //= 2
        p *= 2
    return pairs


_SORT_PAIRS = _batcher_pairs(TOPK)
_MERGE_PAIRS = [(i, i + d) for d in (8, 4, 2, 1) for i in range(TOPK) if not i & d]


def _compare_exchange(v, pairs):
    v = list(v)
    for i, j in pairs:
        v[i], v[j] = jnp.maximum(v[i], v[j]), jnp.minimum(v[i], v[j])
    return v


def _top16(tiles):
    v = _compare_exchange(tiles, _SORT_PAIRS)
    for shift in (4, 2, 1):
        other = [pltpu.roll(x, shift, 0) for x in v]
        v = [jnp.maximum(v[r], other[TOPK - 1 - r]) for r in range(TOPK)]
        v = _compare_exchange(v, _MERGE_PAIRS)
    return v


def _stack_rows(tiles, sub):
    out = tiles[0]
    for r in range(1, 8):
        out = jnp.where(sub == r, tiles[r], out)
    return out


def _peer_select_kernel(sc_ref, e0_ref, thr_ref, e1_ref, rank_ref, *, n_heads, n_keys):
    tl = sc_ref.shape[1]
    reps = n_keys // 8
    sub = lax.broadcasted_iota(jnp.int32, (8, tl), 0)
    spread = lambda tile: pltpu.repeat(tile, reps, axis=0)

    def head_body(h, carry):
        r0 = pl.multiple_of(h * 2 * n_keys, 2 * n_keys)
        s0 = sc_ref[pl.ds(r0, n_keys), :]
        s1 = sc_ref[pl.ds(r0 + n_keys, n_keys), :]
        a = _top16([s0[8 * r:8 * r + 8, :] for r in range(reps)])
        b = _top16([s1[8 * r:8 * r + 8, :] for r in range(reps)])
        b_lo, b_hi, a_hi = _stack_rows(b[:8], sub), _stack_rows(b[8:], sub), _stack_rows(a[8:], sub)
        cand = [a[0] + b_lo, a[0] + b_hi] + [a[p] + b_lo for p in range(1, 8)] + [a_hi + b[0]]
        cand += [jnp.full((8, tl), -jnp.inf, jnp.float32)] * (TOPK - len(cand))
        best = _top16(cand)
        tau = best[TOPK - 1]
        zsum = sum(jnp.exp(c - best[0]) for c in best)
        thr = jnp.zeros((n_keys, tl), jnp.float32)
        rank1 = jnp.full((n_keys, tl), 100.0, jnp.float32)
        for p in reversed(range(TOPK)):
            n_sel = (jnp.where(a[p] + b_lo >= tau, 1.0, 0.0)
                     + jnp.where(a[p] + b_hi >= tau, 1.0, 0.0))
            for shift in (4, 2, 1):
                n_sel = n_sel + pltpu.roll(n_sel, shift, 0)
            thr = jnp.where(s0 == spread(a[p]), spread(n_sel), thr)
            rank1 = jnp.where(s1 == spread(b[p]), float(p), rank1)
        e0_ref[h] = jnp.exp(s0 - spread(a[0])) / spread(zsum)
        thr_ref[h] = thr
        e1_ref[h] = jnp.exp(s1 - spread(b[0])).astype(e1_ref.dtype)
        rank_ref[h] = rank1.astype(rank_ref.dtype)
        return carry

    lax.fori_loop(0, n_heads, head_body, 0)


def _peer_select(scores_t, n_heads, n_keys, tl):
    rows, t = scores_t.shape
    assert rows == n_heads * 2 * n_keys and n_keys == 8 * TOPK
    out_spec = pl.BlockSpec((n_heads, n_keys, tl), lambda i: (0, 0, i))
    f32 = jax.ShapeDtypeStruct((n_heads, n_keys, t), jnp.float32)
    b16 = jax.ShapeDtypeStruct((n_heads, n_keys, t), jnp.bfloat16)
    return pl.pallas_call(
        functools.partial(_peer_select_kernel, n_heads=n_heads, n_keys=n_keys),
        grid=(t // tl,),
        in_specs=[pl.BlockSpec((rows, tl), lambda i: (0, i))],
        out_specs=[out_spec, out_spec, out_spec, out_spec],
        out_shape=[f32, f32, b16, b16],
        compiler_params=pltpu.CompilerParams(
            dimension_semantics=("parallel",), vmem_limit_bytes=VMEM_LIMIT_BYTES),
        name="peer_select",
    )(scores_t)


def _log_sigmoid(x):
    return jnp.minimum(x, 0.0) - jnp.log1p(jnp.exp(-jnp.abs(x)))


def _mlstm_kernel(q_ref, k_ref, v_ref, o_ref, g_ref, gt_ref,
                  qm_ref, km_ref, vm_ref, om_ref, gm_ref, gmt_ref,
                  bias_ref, biast_ref, nw_ref, y_ref, c_ref, m_ref,
                  *, n_heads, head_qk, head_v):
    c = pl.program_id(1)
    tc = q_ref.shape[0]
    f32, bf = jnp.float32, jnp.bfloat16
    first = c == 0

    @pl.when(first)
    def _():
        c_ref[...] = jnp.zeros_like(c_ref)
        m_ref[...] = jnp.zeros_like(m_ref)

    q = jnp.where(first, qm_ref[...], q_ref[...])
    k = jnp.where(first, km_ref[...], k_ref[...])
    v = jnp.where(first, vm_ref[...], v_ref[...])
    o = jnp.where(first, om_ref[...], o_ref[...])
    g = jnp.where(first, gm_ref[...], g_ref[...]) + bias_ref[...]
    gt = jnp.where(first, gmt_ref[...], gt_ref[...]) + biast_ref[...]

    row = lax.broadcasted_iota(jnp.int32, (tc, tc), 0)
    col = lax.broadcasted_iota(jnp.int32, (tc, tc), 1)
    causal = col <= row
    tri = jnp.where(causal, 1.0, 0.0).astype(f32)
    b_cols = jnp.dot(tri, _log_sigmoid(g), precision=lax.Precision.HIGHEST,
                     preferred_element_type=f32)
    b_rows = lax.dot_general(_log_sigmoid(gt), tri, (((1,), (1,)), ((), ())),
                             precision=lax.Precision.HIGHEST, preferred_element_type=f32)
    ln_scale = -0.5 * math.log(head_qk)
    ones_blk = jnp.where(lax.broadcasted_iota(jnp.int32, (tc, LANES), 1) == 0, 1.0, 0.0).astype(bf)

    for h in range(n_heads):
        qh = q[:, h * head_qk:(h + 1) * head_qk]
        kh = k[:, h * head_qk:(h + 1) * head_qk]
        v_ext = jnp.concatenate([v[:, h * head_v:(h + 1) * head_v], ones_blk], axis=1)
        li_col = g[:, h:h + 1]
        b_col = b_cols[:, n_heads + h:n_heads + h + 1]
        li_row = gt[h:h + 1, :]
        b_row = b_rows[n_heads + h:n_heads + h + 1, :]
        m_prev = m_ref[h:h + 1, 0:1]
        d = jnp.where(causal, b_col + (li_row - b_row), -jnp.inf)
        inter = b_col + m_prev
        m_t = jnp.maximum(inter, jnp.max(d, axis=1, keepdims=True))
        w = jnp.exp(d - (m_t - ln_scale))
        s = lax.dot_general(qh, kh, (((1,), (1,)), ((), ())), preferred_element_type=f32) * w
        sc_inter = jnp.exp(inter - (m_t - ln_scale))
        c_ext = c_ref[h]
        num_ext = (jnp.dot(s.astype(bf), v_ext, preferred_element_type=f32)
                   + sc_inter * jnp.dot(qh, c_ext.astype(bf), preferred_element_type=f32))
        den = num_ext[:, head_v:head_v + 1]
        hm = num_ext[:, :head_v] / jnp.maximum(jnp.abs(den), jnp.exp(-m_t))
        hm = hm * lax.rsqrt(jnp.mean(hm * hm, axis=1, keepdims=True) + LN_EPS)
        gate_o = jax.nn.sigmoid(o[:, h * head_v:(h + 1) * head_v].astype(f32))
        y_ref[:, h * head_v:(h + 1) * head_v] = (
            gate_o * (hm * nw_ref[:, h * head_v:(h + 1) * head_v])).astype(y_ref.dtype)
        b_last = b_col[tc - 1:tc, :]
        dl = b_last - b_col + li_col
        m_new = jnp.maximum(b_last + m_prev, jnp.max(dl, axis=0, keepdims=True))
        wv = (jnp.exp(dl - m_new) * v_ext.astype(f32)).astype(bf)
        c_ref[h] = (jnp.exp(b_last + m_prev - m_new) * c_ext
                    + lax.dot_general(kh, wv, (((0,), (0,)), ((), ())),
                                      preferred_element_type=f32))
        m_ref[h:h + 1, :] = jnp.broadcast_to(m_new, (1, LANES))


def _mlstm(proj_x, proj_m, gif_x, gt_x, gif_m, b_if, norm_w, *, bsz, seq, n_heads, head_qk,
           head_v, off_q, off_v):
    tc = min(MLSTM_CHUNK, seq)
    assert seq % tc == 0
    n_meta = proj_m.shape[0]
    d_mqk, d_mv = n_heads * head_qk, n_heads * head_v
    assert off_q % d_mqk == 0 and off_v % d_mv == 0 and n_meta <= tc
    nc_x = seq // tc
    bf = jnp.bfloat16
    lead = tc - n_meta
    pad_rows = lambda a: jnp.pad(a, ((lead, 0), (0, 0)))
    qm = pad_rows(proj_m[:, off_q:off_q + d_mqk])
    km = pad_rows(proj_m[:, off_q + d_mqk:off_q + 2 * d_mqk])
    vm = pad_rows(proj_m[:, off_v:off_v + d_mv])
    om = pad_rows(proj_m[:, off_v + d_mv:off_v + 2 * d_mv])
    pad_gate = jnp.concatenate([jnp.full((lead, n_heads), I_GATE_PAD, jnp.float32),
                                jnp.full((lead, n_heads), -I_GATE_PAD, jnp.float32),
                                jnp.zeros((lead, LANES - 2 * n_heads), jnp.float32)], axis=1)
    gm = jnp.concatenate([pad_gate, gif_m], axis=0)
    bias = jnp.pad(b_if, (0, LANES - 2 * n_heads))[None, :]
    biast = jnp.broadcast_to(b_if[:, None], (2 * n_heads, tc))
    gt_m = gm[:, :2 * n_heads].T

    xrow = lambda b, c: b * nc_x + jnp.maximum(c - 1, 0)
    const = lambda b, c: (0, 0)
    return pl.pallas_call(
        functools.partial(_mlstm_kernel, n_heads=n_heads, head_qk=head_qk, head_v=head_v),
        grid=(bsz, nc_x + 1),
        in_specs=[pl.BlockSpec((tc, d_mqk), lambda b, c: (xrow(b, c), off_q // d_mqk)),
                  pl.BlockSpec((tc, d_mqk), lambda b, c: (xrow(b, c), off_q // d_mqk + 1)),
                  pl.BlockSpec((tc, d_mv), lambda b, c: (xrow(b, c), off_v // d_mv)),
                  pl.BlockSpec((tc, d_mv), lambda b, c: (xrow(b, c), off_v // d_mv + 1)),
                  pl.BlockSpec((tc, LANES), lambda b, c: (xrow(b, c), 0)),
                  pl.BlockSpec((2 * n_heads, tc), lambda b, c: (0, xrow(b, c))),
                  pl.BlockSpec((tc, d_mqk), const), pl.BlockSpec((tc, d_mqk), const),
                  pl.BlockSpec((tc, d_mv), const), pl.BlockSpec((tc, d_mv), const),
                  pl.BlockSpec((tc, LANES), const), pl.BlockSpec((2 * n_heads, tc), const),
                  pl.BlockSpec((1, LANES), const), pl.BlockSpec((2 * n_heads, tc), const),
                  pl.BlockSpec((1, d_mv), const)],
        out_specs=pl.BlockSpec((tc, d_mv), lambda b, c: (xrow(b, c), 0)),
        out_shape=jax.ShapeDtypeStruct((bsz * seq, d_mv), bf),
        scratch_shapes=[pltpu.VMEM((n_heads, head_qk, head_v + LANES), jnp.float32),
                        pltpu.VMEM((n_heads, LANES), jnp.float32)],
        compiler_params=pltpu.CompilerParams(
            dimension_semantics=("parallel", "arbitrary"),
            vmem_limit_bytes=VMEM_LIMIT_BYTES),
        name="mlstm",
    )(proj_x, proj_x, proj_x, proj_x, gif_x, gt_x, qm, km, vm, om, gm, gt_m,
      bias, biast, norm_w[None, :])


def _mix_kernel(x_ref, cb_ref, cc_ref, cx_ref, cch_ref, cxh_ref, ccm_ref, cxm_ref,
                yb_ref, ga_ref, gb_ref, pa_ref, pb_ref, wo_ref, cw_ref,
                g0_ref, b0_ref, g1_ref, b1_ref, h1t_ref,
                *, tiles_per_batch, alpha):
    i = pl.program_id(0)
    f32, bf = jnp.float32, jnp.bfloat16
    tm = x_ref.shape[0]
    z = cc_ref[...].astype(f32) * cx_ref[...].astype(f32)
    batch_start = (i % tiles_per_batch) == 0
    halo = jnp.where(batch_start,
                     ccm_ref[...].astype(f32) * cxm_ref[...].astype(f32),
                     cch_ref[...].astype(f32) * cxh_ref[...].astype(f32))
    sub = lax.broadcasted_iota(jnp.int32, halo.shape, 0)
    n_halo = halo.shape[0]
    y = cw_ref[2:3, :] * z
    for shift in (1, 2):
        rolled = pltpu.roll(z, shift, 0)
        head = jnp.where(sub < shift, pltpu.roll(halo, shift, 0), rolled[:n_halo])
        y = y + cw_ref[2 - shift:3 - shift, :] * jnp.concatenate([head, rolled[n_halo:]], axis=0)
    y_a = (cb_ref[...].astype(f32) * y).astype(bf)
    za = jnp.dot(y_a, pa_ref[...], preferred_element_type=f32)
    zb = jnp.dot(yb_ref[...], pb_ref[...], preferred_element_type=f32)
    merged = (jax.nn.sigmoid(ga_ref[...].astype(f32)) * za
              + jax.nn.sigmoid(gb_ref[...].astype(f32)) * zb).astype(bf)
    mix = jnp.dot(merged, wo_ref[...], preferred_element_type=f32)
    h0 = _ln_rows(x_ref[...], g0_ref[...], b0_ref[...])
    h1t_ref[...] = _ln_rows(alpha * h0 + mix, g1_ref[...], b1_ref[...]).T


def _mix(x2, proj_x, proj_m, y_b, p_a, p_b, w_o, conv_w, ln0_g, ln0_b, ln1_g, ln1_b,
         *, seq, off_g, alpha, tm):
    t_x, d = x2.shape
    d_conv = conv_w.shape[-1]
    d_mv = y_b.shape[-1]
    n_meta = proj_m.shape[0]
    halo = 8
    assert seq % tm == 0 and tm % halo == 0 and n_meta % halo == 0 and off_g % d == 0
    assert conv_w.shape[0] == 3
    tile = lambda j: pl.BlockSpec((tm, d_conv), lambda i: (i, j))
    prev = lambda j: pl.BlockSpec((halo, d_conv),
                                  lambda i: (jnp.maximum(i * (tm // halo) - 1, 0), j))
    meta = lambda j: pl.BlockSpec((halo, d_conv), lambda i: (n_meta // halo - 1, j))
    const = lambda shape: pl.BlockSpec(shape, lambda i: (0, 0), pipeline_mode=pl.Buffered(1))
    vec = lambda a: a[None, :]
    return pl.pallas_call(
        functools.partial(_mix_kernel, tiles_per_batch=seq // tm, alpha=alpha),
        grid=(t_x // tm,),
        in_specs=[pl.BlockSpec((tm, d), lambda i: (i, 0)),
                  tile(0), tile(1), tile(2), prev(1), prev(2), meta(1), meta(2),
                  pl.BlockSpec((tm, d_mv), lambda i: (i, 0)),
                  pl.BlockSpec((tm, d), lambda i: (i, off_g // d)),
                  pl.BlockSpec((tm, d), lambda i: (i, off_g // d + 1)),
                  const(p_a.shape), const(p_b.shape), const(w_o.shape), const(conv_w.shape),
                  const((1, d)), const((1, d)), const((1, d)), const((1, d))],
        out_specs=pl.BlockSpec((d, tm), lambda i: (0, i)),
        out_shape=jax.ShapeDtypeStruct((d, t_x), jnp.float32),
        compiler_params=pltpu.CompilerParams(
            dimension_semantics=("parallel",), vmem_limit_bytes=VMEM_LIMIT_BYTES),
        name="mix",
    )(x2, proj_x, proj_x, proj_x, proj_x, proj_x, proj_m, proj_m, y_b, proj_x, proj_x,
      p_a, p_b, w_o, conv_w, vec(ln0_g), vec(ln0_b), vec(ln1_g), vec(ln1_b))


def kernel(x, meta_tokens, ln0_g, ln0_b, w_in, b_if, conv_w, mh_norm_w, p_a, p_b, w_o,
           ln1_g, ln1_b, peer_wq, peer_keys, peer_u, peer_v, ln2_g, ln2_b):
    depth = w_in.shape[0]
    assert depth == 1
    bsz, seq, d = x.shape
    n_meta = meta_tokens.shape[0]
    n_in = w_in.shape[-1]
    d_conv = conv_w.shape[-1]
    d_mv = mh_norm_w.shape[-1]
    n_mh = b_if.shape[-1] // 2
    d_mqk = (n_in - 3 * d_conv - 2 * d_mv - 2 * n_mh - 2 * d) // 2
    head_qk = d_mqk // n_mh
    head_v = d_mv // n_mh
    alpha = (2 * depth) ** 0.25
    bf = jnp.bfloat16
    t_x = bsz * seq

    x2 = x.reshape(t_x, d)
    gate_off = 3 * d_conv + 2 * d_mqk + 2 * d_mv
    w = w_in[0]
    w_bf = w.astype(bf)
    w_main = jnp.concatenate([w_bf[:, :gate_off], w_bf[:, gate_off + 2 * n_mh:]], axis=1)
    w_gif = jnp.pad(w_bf[:, gate_off:gate_off + 2 * n_mh], ((0, 0), (0, LANES - 2 * n_mh)))
    inproj_tn = min(INPROJ_TILE_N, w_main.shape[1])
    proj_x, gif_x, gt_x = _inproj(x2, ln0_g, ln0_b, w_main, w_gif, INPROJ_TILE_M, inproj_tn)
    proj_m, gif_m, _ = _inproj(meta_tokens, ln0_g, ln0_b, w_main, w_gif, INPROJ_TILE_M,
                               inproj_tn)

    off_q = 3 * d_conv
    off_v = off_q + 2 * d_mqk
    off_g = off_v + 2 * d_mv
    y_b = _mlstm(proj_x, proj_m, gif_x, gt_x, gif_m, b_if[0], mh_norm_w[0], bsz=bsz, seq=seq,
                 n_heads=n_mh, head_qk=head_qk, head_v=head_v, off_q=off_q, off_v=off_v)
    h1_t = _mix(x2, proj_x, proj_m, y_b,
                p_a[0].astype(bf), p_b[0].astype(bf), w_o[0].astype(bf), conv_w[0],
                ln0_g, ln0_b, ln1_g[0], ln1_b[0], seq=seq, off_g=off_g, alpha=alpha,
                tm=MIX_TILE)

    n_ph, _, n_keys, half_key = peer_keys.shape[1:]
    w_fold = _fold_keys(peer_keys[0].reshape(n_ph * 2, n_keys, half_key), peer_wq[0])
    scores_t = _matmul(w_fold, h1_t, jnp.float32, 1024, 512)
    e0, thr, e1, rank = _peer_select(scores_t, n_ph, n_keys, LANES)
    out = _peer_dense(h1_t, e0, thr, e1, rank, peer_u[0], peer_v[0],
                      ln2_g[0], ln2_b[0], alpha=alpha, tm=PEER_TOKEN_TILE,
                      rows_per_step=PEER_ROWS_PER_STEP)
    return out.reshape(bsz, seq, d)
```

```python
import functools
import math

import jax
import jax.numpy as jnp
from jax import lax
from jax.experimental import pallas as pl
from jax.experimental.pallas import tpu as pltpu

LANES = 128
SUBLANES = 8
BF16_SUBLANES = 16
VMEM_LIMIT_BYTES = 56 * 1024 * 1024
LARGE_VMEM_LIMIT_BYTES = 58 * 1024 * 1024

TOPK = 16
LN_EPS = 1e-5
I_GATE_PAD = -1e30
MLSTM_CHUNK = 256
MIX_TILE = 256
INPROJ_TILE_M = 1024
INPROJ_TILE_N = 1024
PEER_TOKEN_TILE = 512
PEER_ROWS_PER_STEP = 8
QUANT_ROWS = 512
FP8 = jnp.float8_e4m3fn
FP8_CLIP = 448.0
FP8_TARGET = 256.0
PEER_P_SCALE = 64.0


def _ln_rows(x, g, b):
    mu = jnp.mean(x, axis=1, keepdims=True)
    xc = x - mu
    var = jnp.mean(xc * xc, axis=1, keepdims=True)
    return xc * lax.rsqrt(var + LN_EPS) * g + b


def _inproj_kernel(x_ref, g_ref, b_ref, w_ref, wg_ref, o_ref, og_ref, ogt_ref, h_ref):
    @pl.when(pl.program_id(1) == 0)
    def _():
        h = _ln_rows(x_ref[...], g_ref[...], b_ref[...]).astype(h_ref.dtype)
        h_ref[...] = h
        gates = jnp.dot(h, wg_ref[...], preferred_element_type=jnp.float32)
        og_ref[...] = gates
        ogt_ref[...] = gates.T

    o_ref[...] = jnp.dot(h_ref[...], w_ref[...],
                         preferred_element_type=jnp.float32).astype(o_ref.dtype)


def _inproj(x2, ln_g, ln_b, w_main, w_gate, tm, tn):
    m, d = x2.shape
    n = w_main.shape[1]
    tm = min(tm, m)
    assert m % tm == 0 and n % tn == 0 and w_gate.shape[1] == LANES
    const = lambda shape: pl.BlockSpec(shape, lambda i, j: (0, 0))
    return pl.pallas_call(
        _inproj_kernel,
        grid=(m // tm, n // tn),
        in_specs=[pl.BlockSpec((tm, d), lambda i, j: (i, 0)),
                  const((1, d)), const((1, d)),
                  pl.BlockSpec((d, tn), lambda i, j: (0, j)),
                  const((d, LANES))],
        out_specs=[pl.BlockSpec((tm, tn), lambda i, j: (i, j)),
                   pl.BlockSpec((tm, LANES), lambda i, j: (i, 0)),
                   pl.BlockSpec((LANES, tm), lambda i, j: (0, i))],
        out_shape=[jax.ShapeDtypeStruct((m, n), jnp.bfloat16),
                   jax.ShapeDtypeStruct((m, LANES), jnp.float32),
                   jax.ShapeDtypeStruct((LANES, m), jnp.float32)],
        scratch_shapes=[pltpu.VMEM((tm, d), jnp.bfloat16)],
        compiler_params=pltpu.CompilerParams(
            dimension_semantics=("parallel", "arbitrary"),
            vmem_limit_bytes=VMEM_LIMIT_BYTES),
        name="inproj",
    )(x2, ln_g[None, :], ln_b[None, :], w_main, w_gate)


def _mm_kernel(a_ref, b_ref, o_ref):
    bf = jnp.bfloat16
    o_ref[...] = jnp.dot(a_ref[...].astype(bf), b_ref[...].astype(bf),
                         preferred_element_type=jnp.float32).astype(o_ref.dtype)


def _matmul(a, b, out_dtype, tm, tn):
    m, k = a.shape
    _, n = b.shape
    tm = min(tm, m)
    tn = min(tn, n)
    assert m % tm == 0 and n % tn == 0
    return pl.pallas_call(
        _mm_kernel,
        grid=(m // tm, n // tn),
        in_specs=[pl.BlockSpec((tm, k), lambda i, j: (i, 0)),
                  pl.BlockSpec((k, tn), lambda i, j: (0, j))],
        out_specs=pl.BlockSpec((tm, tn), lambda i, j: (i, j)),
        out_shape=jax.ShapeDtypeStruct((m, n), out_dtype),
        compiler_params=pltpu.CompilerParams(
            dimension_semantics=("parallel", "parallel"),
            vmem_limit_bytes=VMEM_LIMIT_BYTES),
        name="matmul",
    )(a, b)


def _pow2_scale(amax):
    safe = jnp.where(amax > 0, amax, FP8_TARGET)
    return jnp.exp2(jnp.floor(jnp.log2(FP8_TARGET / safe)))


def _quantize_rows_kernel(x_ref, q_ref, inv_ref, *, transpose):
    x = x_ref[...]
    scale = _pow2_scale(jnp.max(jnp.abs(x), axis=1, keepdims=True))
    xs = x * scale
    q_ref[...] = (xs.T if transpose else xs).astype(q_ref.dtype)
    inv_ref[...] = jnp.broadcast_to(1.0 / scale, inv_ref.shape)


def _quantize_rows(x, rows, transpose):
    n, d = x.shape
    rows = min(rows, n)
    assert n % rows == 0
    q_shape, q_spec = (((d, n), pl.BlockSpec((d, rows), lambda i: (0, i))) if transpose
                       else ((n, d), pl.BlockSpec((rows, d), lambda i: (i, 0))))
    return pl.pallas_call(
        functools.partial(_quantize_rows_kernel, transpose=transpose),
        grid=(n // rows,),
        in_specs=[pl.BlockSpec((rows, d), lambda i: (i, 0))],
        out_specs=[q_spec, pl.BlockSpec((rows, LANES), lambda i: (i, 0))],
        out_shape=[jax.ShapeDtypeStruct(q_shape, FP8),
                   jax.ShapeDtypeStruct((n, LANES), jnp.float32)],
        compiler_params=pltpu.CompilerParams(
            dimension_semantics=("parallel",), vmem_limit_bytes=VMEM_LIMIT_BYTES),
        name="quantize_rows",
    )(x)


def _peer_dense_kernel(ht_ref, e0_ref, thr_ref, e1_ref, rank_ref, u_ref, vt_ref, c1_ref, c2_ref,
                       post_ref, g_ref, b_ref, o_ref, hb_ref, sh_ref, acc_ref, w_ref, p_ref,
                       *, n_heads, rows_per_step, alpha):
    k = pl.program_id(1)
    bf, f8 = jnp.bfloat16, FP8
    tm = ht_ref.shape[1]
    zero = jnp.zeros((), bf)

    def row_tile(ref, h, ii):
        row = jnp.broadcast_to(ref[h, ii:ii + 1, :], (BF16_SUBLANES, tm)).astype(bf)
        return pltpu.repeat(row, LANES // BF16_SUBLANES, axis=0)

    @pl.when(k == 0)
    def _():
        acc_ref[...] = jnp.zeros_like(acc_ref)
        h = ht_ref[...]
        amax = jnp.max(jnp.max(jnp.abs(h), axis=0, keepdims=True), axis=1, keepdims=True)
        s_h = _pow2_scale(amax)
        hb_ref[...] = (h * s_h).astype(f8)
        sh_ref[...] = jnp.broadcast_to(1.0 / s_h, sh_ref.shape)

    for ii in range(rows_per_step):
        w = jnp.zeros((LANES, tm), bf)
        for h in range(n_heads):
            sel = rank_ref[h] < row_tile(thr_ref, h, ii)
            w = w + jnp.where(sel, e1_ref[h], zero) * row_tile(e0_ref, h, ii)
        w_ref[ii * LANES:(ii + 1) * LANES, :] = w
    a = jnp.dot(u_ref[...], hb_ref[...], preferred_element_type=jnp.float32)
    inv_sh = sh_ref[0:1, 0:1]
    c1 = pltpu.repeat((c1_ref[...] * inv_sh).astype(bf), tm // LANES, axis=1)
    c2 = pltpu.repeat((c2_ref[...] * inv_sh).astype(bf), tm // LANES, axis=1)
    ab = a.astype(bf)
    act = (ab * c1) * (1.0 + lax.erf(ab * c2))
    p = jnp.clip(w_ref[...] * act, -FP8_CLIP, FP8_CLIP)
    p_ref[...] = p.astype(f8)
    acc_ref[...] += jnp.dot(vt_ref[...], p_ref[...], preferred_element_type=jnp.float32)

    @pl.when(k == pl.num_programs(1) - 1)
    def _():
        y = alpha * ht_ref[...] + acc_ref[...] * post_ref[0:1, 0:1]
        mu = jnp.mean(y, axis=0, keepdims=True)
        yc = y - mu
        var = jnp.mean(yc * yc, axis=0, keepdims=True)
        o_ref[...] = (yc * lax.rsqrt(var + LN_EPS)).T * g_ref[...] + b_ref[...]


def _peer_dense(ht, e0, thr, e1, rank, peer_u, peer_v, ln_g, ln_b, *, alpha, tm, rows_per_step):
    u, inv_u = _quantize_rows(peer_u, QUANT_ROWS, transpose=False)
    vt, inv_v = _quantize_rows(peer_v, QUANT_ROWS, transpose=True)
    inv_v_max = jnp.max(inv_v)
    c1 = inv_u * (inv_v / inv_v_max) * (0.5 * PEER_P_SCALE)
    c2 = inv_u * (2.0 ** -0.5)
    post = jnp.broadcast_to(inv_v_max / PEER_P_SCALE, (1, LANES)).astype(jnp.float32)
    d, t = ht.shape
    n_heads, n_keys, _ = e0.shape
    assert n_keys == LANES
    n_experts = u.shape[0]
    eb = rows_per_step * LANES
    sel_spec = pl.BlockSpec((n_heads, n_keys, tm), lambda i, k: (0, 0, i))
    row_spec = pl.BlockSpec((n_heads, rows_per_step, tm), lambda i, k: (0, k, i))
    vec_spec = pl.BlockSpec((1, d), lambda i, k: (0, 0))
    return pl.pallas_call(
        functools.partial(_peer_dense_kernel, n_heads=n_heads, rows_per_step=rows_per_step,
                          alpha=alpha),
        grid=(t // tm, n_experts // eb),
        in_specs=[pl.BlockSpec((d, tm), lambda i, k: (0, i)),
                  row_spec, row_spec, sel_spec, sel_spec,
                  pl.BlockSpec((eb, d), lambda i, k: (k, 0)),
                  pl.BlockSpec((d, eb), lambda i, k: (0, k)),
                  pl.BlockSpec((eb, LANES), lambda i, k: (k, 0)),
                  pl.BlockSpec((eb, LANES), lambda i, k: (k, 0)),
                  pl.BlockSpec((1, LANES), lambda i, k: (0, 0)),
                  vec_spec, vec_spec],
        out_specs=pl.BlockSpec((tm, d), lambda i, k: (i, 0)),
        out_shape=jax.ShapeDtypeStruct((t, d), jnp.float32),
        scratch_shapes=[pltpu.VMEM((d, tm), FP8),
                        pltpu.VMEM((SUBLANES, LANES), jnp.float32),
                        pltpu.VMEM((d, tm), jnp.float32),
                        pltpu.VMEM((eb, tm), jnp.bfloat16),
                        pltpu.VMEM((eb, tm), FP8)],
        compiler_params=pltpu.CompilerParams(
            dimension_semantics=("parallel", "arbitrary"),
            vmem_limit_bytes=LARGE_VMEM_LIMIT_BYTES),
        name="peer_dense",
    )(ht, e0, thr, e1, rank, u, vt, c1, c2, post, ln_g[None, :], ln_b[None, :])


def _fold_keys_kernel(keys_ref, wq_ref, o_ref):
    o_ref[...] = lax.dot_general(
        keys_ref[0], wq_ref[...], (((1,), (1,)), ((), ())),
        precision=lax.Precision.HIGHEST, preferred_element_type=jnp.float32)


def _fold_keys(keys, wq):
    n_blocks, n_keys, half_key = keys.shape
    d = wq.shape[0]
    return pl.pallas_call(
        _fold_keys_kernel,
        grid=(n_blocks,),
        in_specs=[pl.BlockSpec((1, n_keys, half_key), lambda b: (b, 0, 0)),
                  pl.BlockSpec((d, half_key), lambda b: (0, b))],
        out_specs=pl.BlockSpec((n_keys, d), lambda b: (b, 0)),
        out_shape=jax.ShapeDtypeStruct((n_blocks * n_keys, d), jnp.float32),
        compiler_params=pltpu.CompilerParams(dimension_semantics=("parallel",)),
        name="fold_keys",
    )(keys, wq)


def _batcher_pairs(n):
    pairs = []
    p = 1
    while p < n:
        k = p
        while k >= 1:
            for j in range(k % p, n - k, 2 * k):
                for i in range(min(k, n - j - k)):
                    if (i + j) // (2 * p) == (i + j + k) // (2 * p):
                        pairs.append((i + j, i + j + k))
            k //= 2
        p *= 2
    return pairs


_SORT_PAIRS = _batcher_pairs(TOPK)
_MERGE_PAIRS = [(i, i + d) for d in (8, 4, 2, 1) for i in range(TOPK) if not i & d]
_SUBLANE_SHIFTS = (4, 2, 1)
UNRANKED = 100.0


def _compare_exchange(v, pairs):
    v = list(v)
    for i, j in pairs:
        v[i], v[j] = jnp.maximum(v[i], v[j]), jnp.minimum(v[i], v[j])
    return v


def _top16(tiles):
    v = _compare_exchange(tiles, _SORT_PAIRS)
    for shift in _SUBLANE_SHIFTS:
        other = [pltpu.roll(x, shift, 0) for x in v]
        v = [jnp.maximum(v[r], other[TOPK - 1 - r]) for r in range(TOPK)]
        v = _compare_exchange(v, _MERGE_PAIRS)
    return v


def _stack_rows(tiles, sub):
    out = tiles[0]
    for r in range(1, SUBLANES):
        out = jnp.where(sub == r, tiles[r], out)
    return out


def _peer_select_kernel(sc_ref, e0_ref, thr_ref, e1_ref, rank_ref, *, n_heads, n_keys):
    tl = sc_ref.shape[1]
    reps = n_keys // SUBLANES
    half = TOPK // 2
    sub = lax.broadcasted_iota(jnp.int32, (SUBLANES, tl), 0)
    spread = lambda tile: pltpu.repeat(tile, reps, axis=0)

    def head_body(h, carry):
        r0 = pl.multiple_of(h * 2 * n_keys, 2 * n_keys)
        s0 = sc_ref[pl.ds(r0, n_keys), :]
        s1 = sc_ref[pl.ds(r0 + n_keys, n_keys), :]
        a = _top16([s0[SUBLANES * r:SUBLANES * (r + 1), :] for r in range(reps)])
        b = _top16([s1[SUBLANES * r:SUBLANES * (r + 1), :] for r in range(reps)])
        b_lo, b_hi = _stack_rows(b[:half], sub), _stack_rows(b[half:], sub)
        a_hi = _stack_rows(a[half:], sub)
        cand = ([a[0] + b_lo, a[0] + b_hi] + [a[p] + b_lo for p in range(1, half)]
                + [a_hi + b[0]])
        cand += [jnp.full((SUBLANES, tl), -jnp.inf, jnp.float32)] * (TOPK - len(cand))
        best = _top16(cand)
        tau = best[TOPK - 1]
        zsum = sum(jnp.exp(c - best[0]) for c in best)
        thr = jnp.zeros((n_keys, tl), jnp.float32)
        rank1 = jnp.full((n_keys, tl), UNRANKED, jnp.float32)
        for p in reversed(range(TOPK)):
            n_sel = (jnp.where(a[p] + b_lo >= tau, 1.0, 0.0)
                     + jnp.where(a[p] + b_hi >= tau, 1.0, 0.0))
            for shift in _SUBLANE_SHIFTS:
                n_sel = n_sel + pltpu.roll(n_sel, shift, 0)
            thr = jnp.where(s0 == spread(a[p]), spread(n_sel), thr)
            rank1 = jnp.where(s1 == spread(b[p]), float(p), rank1)
        e0_ref[h] = jnp.exp(s0 - spread(a[0])) / spread(zsum)
        thr_ref[h] = thr
        e1_ref[h] = jnp.exp(s1 - spread(b[0])).astype(e1_ref.dtype)
        rank_ref[h] = rank1.astype(rank_ref.dtype)
        return carry

    lax.fori_loop(0, n_heads, head_body, 0)


def _peer_select(scores_t, n_heads, n_keys, tl):
    rows, t = scores_t.shape
    assert rows == n_heads * 2 * n_keys and n_keys == SUBLANES * TOPK and TOPK == 2 * SUBLANES
    out_spec = pl.BlockSpec((n_heads, n_keys, tl), lambda i: (0, 0, i))
    f32 = jax.ShapeDtypeStruct((n_heads, n_keys, t), jnp.float32)
    b16 = jax.ShapeDtypeStruct((n_heads, n_keys, t), jnp.bfloat16)
    return pl.pallas_call(
        functools.partial(_peer_select_kernel, n_heads=n_heads, n_keys=n_keys),
        grid=(t // tl,),
        in_specs=[pl.BlockSpec((rows, tl), lambda i: (0, i))],
        out_specs=[out_spec, out_spec, out_spec, out_spec],
        out_shape=[f32, f32, b16, b16],
        compiler_params=pltpu.CompilerParams(
            dimension_semantics=("parallel",), vmem_limit_bytes=VMEM_LIMIT_BYTES),
        name="peer_select",
    )(scores_t)


def _log_sigmoid(x):
    return jnp.minimum(x, 0.0) - jnp.log1p(jnp.exp(-jnp.abs(x)))


def _mlstm_kernel(q_ref, k_ref, v_ref, o_ref, g_ref, gt_ref,
                  qm_ref, km_ref, vm_ref, om_ref, gm_ref, gmt_ref,
                  bias_ref, biast_ref, nw_ref, y_ref, c_ref, m_ref,
                  *, n_heads, head_qk, head_v):
    c = pl.program_id(1)
    tc = q_ref.shape[0]
    f32, bf = jnp.float32, jnp.bfloat16
    first = c == 0

    @pl.when(first)
    def _():
        c_ref[...] = jnp.zeros_like(c_ref)
        m_ref[...] = jnp.zeros_like(m_ref)

    q = jnp.where(first, qm_ref[...], q_ref[...])
    k = jnp.where(first, km_ref[...], k_ref[...])
    v = jnp.where(first, vm_ref[...], v_ref[...])
    o = jnp.where(first, om_ref[...], o_ref[...])
    g = jnp.where(first, gm_ref[...], g_ref[...]) + bias_ref[...]
    gt = jnp.where(first, gmt_ref[...], gt_ref[...]) + biast_ref[...]

    row = lax.broadcasted_iota(jnp.int32, (tc, tc), 0)
    col = lax.broadcasted_iota(jnp.int32, (tc, tc), 1)
    causal = col <= row
    tri = jnp.where(causal, 1.0, 0.0).astype(f32)
    b_cols = jnp.dot(tri, _log_sigmoid(g), precision=lax.Precision.HIGHEST,
                     preferred_element_type=f32)
    b_rows = lax.dot_general(_log_sigmoid(gt), tri, (((1,), (1,)), ((), ())),
                             precision=lax.Precision.HIGHEST, preferred_element_type=f32)
    ln_scale = -0.5 * math.log(head_qk)
    ones_blk = jnp.where(lax.broadcasted_iota(jnp.int32, (tc, LANES), 1) == 0, 1.0, 0.0).astype(bf)

    for h in range(n_heads):
        qh = q[:, h * head_qk:(h + 1) * head_qk]
        kh = k[:, h * head_qk:(h + 1) * head_qk]
        v_ext = jnp.concatenate([v[:, h * head_v:(h + 1) * head_v], ones_blk], axis=1)
        li_col = g[:, h:h + 1]
        b_col = b_cols[:, n_heads + h:n_heads + h + 1]
        li_row = gt[h:h + 1, :]
        b_row = b_rows[n_heads + h:n_heads + h + 1, :]
        m_prev = m_ref[h:h + 1, 0:1]
        d = jnp.where(causal, b_col + (li_row - b_row), -jnp.inf)
        inter = b_col + m_prev
        m_t = jnp.maximum(inter, jnp.max(d, axis=1, keepdims=True))
        w = jnp.exp(d - (m_t - ln_scale))
        s = lax.dot_general(qh, kh, (((1,), (1,)), ((), ())), preferred_element_type=f32) * w
        sc_inter = jnp.exp(inter - (m_t - ln_scale))
        c_ext = c_ref[h]
        num_ext = (jnp.dot(s.astype(bf), v_ext, preferred_element_type=f32)
                   + sc_inter * jnp.dot(qh, c_ext.astype(bf), preferred_element_type=f32))
        den = num_ext[:, head_v:head_v + 1]
        hm = num_ext[:, :head_v] / jnp.maximum(jnp.abs(den), jnp.exp(-m_t))
        hm = hm * lax.rsqrt(jnp.mean(hm * hm, axis=1, keepdims=True) + LN_EPS)
        gate_o = jax.nn.sigmoid(o[:, h * head_v:(h + 1) * head_v].astype(f32))
        y_ref[:, h * head_v:(h + 1) * head_v] = (
            gate_o * (hm * nw_ref[:, h * head_v:(h + 1) * head_v])).astype(y_ref.dtype)
        b_last = b_col[tc - 1:tc, :]
        dl = b_last - b_col + li_col
        m_new = jnp.maximum(b_last + m_prev, jnp.max(dl, axis=0, keepdims=True))
        wv = (jnp.exp(dl - m_new) * v_ext.astype(f32)).astype(bf)
        c_ref[h] = (jnp.exp(b_last + m_prev - m_new) * c_ext
                    + lax.dot_general(kh, wv, (((0,), (0,)), ((), ())),
                                      preferred_element_type=f32))
        m_ref[h:h + 1, :] = jnp.broadcast_to(m_new, (1, LANES))


def _mlstm(proj_x, proj_m, gif_x, gt_x, gif_m, b_if, norm_w, *, bsz, seq, n_heads, head_qk,
           head_v, off_q, off_v):
    tc = min(MLSTM_CHUNK, seq)
    assert seq % tc == 0
    n_meta = proj_m.shape[0]
    d_mqk, d_mv = n_heads * head_qk, n_heads * head_v
    assert off_q % d_mqk == 0 and off_v % d_mv == 0 and n_meta <= tc
    nc_x = seq // tc
    bf = jnp.bfloat16
    lead = tc - n_meta
    pad_rows = lambda a: jnp.pad(a, ((lead, 0), (0, 0)))
    qm = pad_rows(proj_m[:, off_q:off_q + d_mqk])
    km = pad_rows(proj_m[:, off_q + d_mqk:off_q + 2 * d_mqk])
    vm = pad_rows(proj_m[:, off_v:off_v + d_mv])
    om = pad_rows(proj_m[:, off_v + d_mv:off_v + 2 * d_mv])
    pad_gate = jnp.concatenate([jnp.full((lead, n_heads), I_GATE_PAD, jnp.float32),
                                jnp.full((lead, n_heads), -I_GATE_PAD, jnp.float32),
                                jnp.zeros((lead, LANES - 2 * n_heads), jnp.float32)], axis=1)
    gm = jnp.concatenate([pad_gate, gif_m], axis=0)
    bias = jnp.pad(b_if, (0, LANES - 2 * n_heads))[None, :]
    biast = jnp.broadcast_to(b_if[:, None], (2 * n_heads, tc))
    gt_m = gm[:, :2 * n_heads].T

    xrow = lambda b, c: b * nc_x + jnp.maximum(c - 1, 0)
    const = lambda b, c: (0, 0)
    return pl.pallas_call(
        functools.partial(_mlstm_kernel, n_heads=n_heads, head_qk=head_qk, head_v=head_v),
        grid=(bsz, nc_x + 1),
        in_specs=[pl.BlockSpec((tc, d_mqk), lambda b, c: (xrow(b, c), off_q // d_mqk)),
                  pl.BlockSpec((tc, d_mqk), lambda b, c: (xrow(b, c), off_q // d_mqk + 1)),
                  pl.BlockSpec((tc, d_mv), lambda b, c: (xrow(b, c), off_v // d_mv)),
                  pl.BlockSpec((tc, d_mv), lambda b, c: (xrow(b, c), off_v // d_mv + 1)),
                  pl.BlockSpec((tc, LANES), lambda b, c: (xrow(b, c), 0)),
                  pl.BlockSpec((2 * n_heads, tc), lambda b, c: (0, xrow(b, c))),
                  pl.BlockSpec((tc, d_mqk), const), pl.BlockSpec((tc, d_mqk), const),
                  pl.BlockSpec((tc, d_mv), const), pl.BlockSpec((tc, d_mv), const),
                  pl.BlockSpec((tc, LANES), const), pl.BlockSpec((2 * n_heads, tc), const),
                  pl.BlockSpec((1, LANES), const), pl.BlockSpec((2 * n_heads, tc), const),
                  pl.BlockSpec((1, d_mv), const)],
        out_specs=pl.BlockSpec((tc, d_mv), lambda b, c: (xrow(b, c), 0)),
        out_shape=jax.ShapeDtypeStruct((bsz * seq, d_mv), bf),
        scratch_shapes=[pltpu.VMEM((n_heads, head_qk, head_v + LANES), jnp.float32),
                        pltpu.VMEM((n_heads, LANES), jnp.float32)],
        compiler_params=pltpu.CompilerParams(
            dimension_semantics=("parallel", "arbitrary"),
            vmem_limit_bytes=VMEM_LIMIT_BYTES),
        name="mlstm",
    )(proj_x, proj_x, proj_x, proj_x, gif_x, gt_x, qm, km, vm, om, gm, gt_m,
      bias, biast, norm_w[None, :])


def _mix_kernel(x_ref, cb_ref, cc_ref, cx_ref, cch_ref, cxh_ref, ccm_ref, cxm_ref,
                yb_ref, ga_ref, gb_ref, pa_ref, pb_ref, wo_ref, cw_ref,
                g0_ref, b0_ref, g1_ref, b1_ref, h1t_ref,
                *, tiles_per_batch, alpha):
    i = pl.program_id(0)
    f32, bf = jnp.float32, jnp.bfloat16
    tm = x_ref.shape[0]
    z = cc_ref[...].astype(f32) * cx_ref[...].astype(f32)
    batch_start = (i % tiles_per_batch) == 0
    halo = jnp.where(batch_start,
                     ccm_ref[...].astype(f32) * cxm_ref[...].astype(f32),
                     cch_ref[...].astype(f32) * cxh_ref[...].astype(f32))
    sub = lax.broadcasted_iota(jnp.int32, halo.shape, 0)
    n_halo = halo.shape[0]
    y = cw_ref[2:3, :] * z
    for shift in (1, 2):
        rolled = pltpu.roll(z, shift, 0)
        head = jnp.where(sub < shift, pltpu.roll(halo, shift, 0), rolled[:n_halo])
        y = y + cw_ref[2 - shift:3 - shift, :] * jnp.concatenate([head, rolled[n_halo:]], axis=0)
    y_a = (cb_ref[...].astype(f32) * y).astype(bf)
    za = jnp.dot(y_a, pa_ref[...], preferred_element_type=f32)
    zb = jnp.dot(yb_ref[...], pb_ref[...], preferred_element_type=f32)
    merged = (jax.nn.sigmoid(ga_ref[...].astype(f32)) * za
              + jax.nn.sigmoid(gb_ref[...].astype(f32)) * zb).astype(bf)
    mix = jnp.dot(merged, wo_ref[...], preferred_element_type=f32)
    h0 = _ln_rows(x_ref[...], g0_ref[...], b0_ref[...])
    h1t_ref[...] = _ln_rows(alpha * h0 + mix, g1_ref[...], b1_ref[...]).T


def _mix(x2, proj_x, proj_m, y_b, p_a, p_b, w_o, conv_w, ln0_g, ln0_b, ln1_g, ln1_b,
         *, seq, off_g, alpha, tm):
    t_x, d = x2.shape
    d_conv = conv_w.shape[-1]
    d_mv = y_b.shape[-1]
    n_meta = proj_m.shape[0]
    halo = SUBLANES
    assert seq % tm == 0 and tm % halo == 0 and n_meta % halo == 0 and off_g % d == 0
    assert conv_w.shape[0] == 3
    tile = lambda j: pl.BlockSpec((tm, d_conv), lambda i: (i, j))
    prev = lambda j: pl.BlockSpec((halo, d_conv),
                                  lambda i: (jnp.maximum(i * (tm // halo) - 1, 0), j))
    meta = lambda j: pl.BlockSpec((halo, d_conv), lambda i: (n_meta // halo - 1, j))
    const = lambda shape: pl.BlockSpec(shape, lambda i: (0, 0), pipeline_mode=pl.Buffered(1))
    vec = lambda a: a[None, :]
    return pl.pallas_call(
        functools.partial(_mix_kernel, tiles_per_batch=seq // tm, alpha=alpha),
        grid=(t_x // tm,),
        in_specs=[pl.BlockSpec((tm, d), lambda i: (i, 0)),
                  tile(0), tile(1), tile(2), prev(1), prev(2), meta(1), meta(2),
                  pl.BlockSpec((tm, d_mv), lambda i: (i, 0)),
                  pl.BlockSpec((tm, d), lambda i: (i, off_g // d)),
                  pl.BlockSpec((tm, d), lambda i: (i, off_g // d + 1)),
                  const(p_a.shape), const(p_b.shape), const(w_o.shape), const(conv_w.shape),
                  const((1, d)), const((1, d)), const((1, d)), const((1, d))],
        out_specs=pl.BlockSpec((d, tm), lambda i: (0, i)),
        out_shape=jax.ShapeDtypeStruct((d, t_x), jnp.float32),
        compiler_params=pltpu.CompilerParams(
            dimension_semantics=("parallel",), vmem_limit_bytes=VMEM_LIMIT_BYTES),
        name="mix",
    )(x2, proj_x, proj_x, proj_x, proj_x, proj_x, proj_m, proj_m, y_b, proj_x, proj_x,
      p_a, p_b, w_o, conv_w, vec(ln0_g), vec(ln0_b), vec(ln1_g), vec(ln1_b))


def kernel(x, meta_tokens, ln0_g, ln0_b, w_in, b_if, conv_w, mh_norm_w, p_a, p_b, w_o,
           ln1_g, ln1_b, peer_wq, peer_keys, peer_u, peer_v, ln2_g, ln2_b):
    depth = w_in.shape[0]
    assert depth == 1
    bsz, seq, d = x.shape
    n_meta = meta_tokens.shape[0]
    n_in = w_in.shape[-1]
    d_conv = conv_w.shape[-1]
    d_mv = mh_norm_w.shape[-1]
    n_mh = b_if.shape[-1] // 2
    d_mqk = (n_in - 3 * d_conv - 2 * d_mv - 2 * n_mh - 2 * d) // 2
    head_qk = d_mqk // n_mh
    head_v = d_mv // n_mh
    alpha = (2 * depth) ** 0.25
    bf = jnp.bfloat16
    t_x = bsz * seq

    x2 = x.reshape(t_x, d)
    gate_off = 3 * d_conv + 2 * d_mqk + 2 * d_mv
    w = w_in[0]
    w_bf = w.astype(bf)
    w_main = jnp.concatenate([w_bf[:, :gate_off], w_bf[:, gate_off + 2 * n_mh:]], axis=1)
    w_gif = jnp.pad(w_bf[:, gate_off:gate_off + 2 * n_mh], ((0, 0), (0, LANES - 2 * n_mh)))
    inproj_tn = min(INPROJ_TILE_N, w_main.shape[1])
    proj_x, gif_x, gt_x = _inproj(x2, ln0_g, ln0_b, w_main, w_gif, INPROJ_TILE_M, inproj_tn)
    proj_m, gif_m, _ = _inproj(meta_tokens, ln0_g, ln0_b, w_main, w_gif, INPROJ_TILE_M,
                               inproj_tn)

    off_q = 3 * d_conv
    off_v = off_q + 2 * d_mqk
    off_g = off_v + 2 * d_mv
    y_b = _mlstm(proj_x, proj_m, gif_x, gt_x, gif_m, b_if[0], mh_norm_w[0], bsz=bsz, seq=seq,
                 n_heads=n_mh, head_qk=head_qk, head_v=head_v, off_q=off_q, off_v=off_v)
    h1_t = _mix(x2, proj_x, proj_m, y_b,
                p_a[0].astype(bf), p_b[0].astype(bf), w_o[0].astype(bf), conv_w[0],
                ln0_g, ln0_b, ln1_g[0], ln1_b[0], seq=seq, off_g=off_g, alpha=alpha,
                tm=MIX_TILE)

    n_ph, _, n_keys, half_key = peer_keys.shape[1:]
    w_fold = _fold_keys(peer_keys[0].reshape(n_ph * 2, n_keys, half_key), peer_wq[0])
    scores_t = _matmul(w_fold, h1_t, jnp.float32, 1024, 512)
    e0, thr, e1, rank = _peer_select(scores_t, n_ph, n_keys, LANES)
    out = _peer_dense(h1_t, e0, thr, e1, rank, peer_u[0], peer_v[0],
                      ln2_g[0], ln2_b[0], alpha=alpha, tm=PEER_TOKEN_TILE,
                      rows_per_step=PEER_ROWS_PER_STEP)
    return out.reshape(bsz, seq, d)
```

```python
import functools
import math

import jax
import jax.numpy as jnp
from jax import lax
from jax.experimental import pallas as pl
from jax.experimental.pallas import tpu as pltpu

LANES = 128
SUBLANES = 8
BF16_SUBLANES = 16
VMEM_LIMIT_BYTES = 56 * 1024 * 1024
LARGE_VMEM_LIMIT_BYTES = 58 * 1024 * 1024

TOPK = 16
LN_EPS = 1e-5
I_GATE_PAD = -1e30
MLSTM_CHUNK = 256
MIX_TILE = 256
INPROJ_TILE_M = 1024
INPROJ_TILE_N = 1024
PEER_TOKEN_TILE = 512
PEER_ROWS_PER_STEP = 8
QUANT_ROWS = 512
FP8 = jnp.float8_e4m3fn
FP8_CLIP = 448.0
FP8_TARGET = 256.0
PEER_P_SCALE = 64.0


def _ln_rows(x, g, b):
    mu = jnp.mean(x, axis=1, keepdims=True)
    xc = x - mu
    var = jnp.mean(xc * xc, axis=1, keepdims=True)
    return xc * lax.rsqrt(var + LN_EPS) * g + b


def _inproj_kernel(x_ref, g_ref, b_ref, wh_ref, wt_ref, wg_ref, o_ref, og_ref, ogt_ref, h_ref,
                   *, n_head_blocks):
    j = pl.program_id(1)

    @pl.when(j == 0)
    def _():
        h = _ln_rows(x_ref[...], g_ref[...], b_ref[...]).astype(h_ref.dtype)
        h_ref[...] = h
        gates = jnp.dot(h, wg_ref[...], preferred_element_type=jnp.float32)
        og_ref[...] = gates
        ogt_ref[...] = gates.T

    @pl.when(j < n_head_blocks)
    def _():
        o_ref[...] = jnp.dot(h_ref[...], wh_ref[...],
                             preferred_element_type=jnp.float32).astype(o_ref.dtype)

    @pl.when(j >= n_head_blocks)
    def _():
        o_ref[...] = jnp.dot(h_ref[...], wt_ref[...],
                             preferred_element_type=jnp.float32).astype(o_ref.dtype)


def _inproj(x2, ln_g, ln_b, w_head, w_tail, w_gate, tm, tn):
    m, d = x2.shape
    n = w_head.shape[1] + w_tail.shape[1]
    tm = min(tm, m)
    assert m % tm == 0 and w_head.shape[1] % tn == 0 and w_tail.shape[1] % tn == 0
    assert w_gate.shape[1] == LANES
    nh = w_head.shape[1] // tn
    const = lambda shape: pl.BlockSpec(shape, lambda i, j: (0, 0))
    return pl.pallas_call(
        functools.partial(_inproj_kernel, n_head_blocks=nh),
        grid=(m // tm, n // tn),
        in_specs=[pl.BlockSpec((tm, d), lambda i, j: (i, 0)),
                  const((1, d)), const((1, d)),
                  pl.BlockSpec((d, tn), lambda i, j: (0, jnp.minimum(j, nh - 1))),
                  pl.BlockSpec((d, tn), lambda i, j: (0, jnp.maximum(j - nh, 0))),
                  const((d, LANES))],
        out_specs=[pl.BlockSpec((tm, tn), lambda i, j: (i, j)),
                   pl.BlockSpec((tm, LANES), lambda i, j: (i, 0)),
                   pl.BlockSpec((LANES, tm), lambda i, j: (0, i))],
        out_shape=[jax.ShapeDtypeStruct((m, n), jnp.bfloat16),
                   jax.ShapeDtypeStruct((m, LANES), jnp.float32),
                   jax.ShapeDtypeStruct((LANES, m), jnp.float32)],
        scratch_shapes=[pltpu.VMEM((tm, d), jnp.bfloat16)],
        compiler_params=pltpu.CompilerParams(
            dimension_semantics=("parallel", "arbitrary"),
            vmem_limit_bytes=VMEM_LIMIT_BYTES),
        name="inproj",
    )(x2, ln_g[None, :], ln_b[None, :], w_head, w_tail, w_gate)


def _mm_kernel(a_ref, b_ref, o_ref):
    bf = jnp.bfloat16
    o_ref[...] = jnp.dot(a_ref[...].astype(bf), b_ref[...].astype(bf),
                         preferred_element_type=jnp.float32).astype(o_ref.dtype)


def _matmul(a, b, out_dtype, tm, tn):
    m, k = a.shape
    _, n = b.shape
    tm = min(tm, m)
    tn = min(tn, n)
    assert m % tm == 0 and n % tn == 0
    return pl.pallas_call(
        _mm_kernel,
        grid=(m // tm, n // tn),
        in_specs=[pl.BlockSpec((tm, k), lambda i, j: (i, 0)),
                  pl.BlockSpec((k, tn), lambda i, j: (0, j))],
        out_specs=pl.BlockSpec((tm, tn), lambda i, j: (i, j)),
        out_shape=jax.ShapeDtypeStruct((m, n), out_dtype),
        compiler_params=pltpu.CompilerParams(
            dimension_semantics=("parallel", "parallel"),
            vmem_limit_bytes=VMEM_LIMIT_BYTES),
        name="matmul",
    )(a, b)


def _pow2_scale(amax):
    safe = jnp.where(amax > 0, amax, FP8_TARGET)
    return jnp.exp2(jnp.floor(jnp.log2(FP8_TARGET / safe)))


def _quantize_rows_kernel(x_ref, q_ref, inv_ref, *, transpose):
    x = x_ref[...]
    scale = _pow2_scale(jnp.max(jnp.abs(x), axis=1, keepdims=True))
    xs = x * scale
    q_ref[...] = (xs.T if transpose else xs).astype(q_ref.dtype)
    inv_ref[...] = jnp.broadcast_to(1.0 / scale, inv_ref.shape)


def _quantize_rows(x, rows, transpose):
    n, d = x.shape
    rows = min(rows, n)
    assert n % rows == 0
    q_shape, q_spec = (((d, n), pl.BlockSpec((d, rows), lambda i: (0, i))) if transpose
                       else ((n, d), pl.BlockSpec((rows, d), lambda i: (i, 0))))
    return pl.pallas_call(
        functools.partial(_quantize_rows_kernel, transpose=transpose),
        grid=(n // rows,),
        in_specs=[pl.BlockSpec((rows, d), lambda i: (i, 0))],
        out_specs=[q_spec, pl.BlockSpec((rows, LANES), lambda i: (i, 0))],
        out_shape=[jax.ShapeDtypeStruct(q_shape, FP8),
                   jax.ShapeDtypeStruct((n, LANES), jnp.float32)],
        compiler_params=pltpu.CompilerParams(
            dimension_semantics=("parallel",), vmem_limit_bytes=VMEM_LIMIT_BYTES),
        name="quantize_rows",
    )(x)


def _peer_dense_kernel(ht_ref, e0_ref, thr_ref, e1_ref, rank_ref, u_ref, vt_ref, c1_ref, c2_ref,
                       post_ref, g_ref, b_ref, o_ref, hb_ref, sh_ref, acc_ref, w_ref, p_ref,
                       *, n_heads, rows_per_step, alpha):
    k = pl.program_id(1)
    bf, f8 = jnp.bfloat16, FP8
    tm = ht_ref.shape[1]
    zero = jnp.zeros((), bf)

    def row_tile(ref, h, ii):
        row = jnp.broadcast_to(ref[h, ii:ii + 1, :], (BF16_SUBLANES, tm)).astype(bf)
        return pltpu.repeat(row, LANES // BF16_SUBLANES, axis=0)

    @pl.when(k == 0)
    def _():
        acc_ref[...] = jnp.zeros_like(acc_ref)
        h = ht_ref[...]
        amax = jnp.max(jnp.max(jnp.abs(h), axis=0, keepdims=True), axis=1, keepdims=True)
        s_h = _pow2_scale(amax)
        hb_ref[...] = (h * s_h).astype(f8)
        sh_ref[...] = jnp.broadcast_to(1.0 / s_h, sh_ref.shape)

    for ii in range(rows_per_step):
        w = jnp.zeros((LANES, tm), bf)
        for h in range(n_heads):
            sel = rank_ref[h] < row_tile(thr_ref, h, ii)
            w = w + jnp.where(sel, e1_ref[h], zero) * row_tile(e0_ref, h, ii)
        w_ref[ii * LANES:(ii + 1) * LANES, :] = w
    a = jnp.dot(u_ref[...], hb_ref[...], preferred_element_type=jnp.float32)
    inv_sh = sh_ref[0:1, 0:1]
    c1 = pltpu.repeat((c1_ref[...] * inv_sh).astype(bf), tm // LANES, axis=1)
    c2 = pltpu.repeat((c2_ref[...] * inv_sh).astype(bf), tm // LANES, axis=1)
    ab = a.astype(bf)
    act = (ab * c1) * (1.0 + lax.erf(ab * c2))
    p = jnp.clip(w_ref[...] * act, -FP8_CLIP, FP8_CLIP)
    p_ref[...] = p.astype(f8)
    acc_ref[...] += jnp.dot(vt_ref[...], p_ref[...], preferred_element_type=jnp.float32)

    @pl.when(k == pl.num_programs(1) - 1)
    def _():
        y = alpha * ht_ref[...] + acc_ref[...] * post_ref[0:1, 0:1]
        mu = jnp.mean(y, axis=0, keepdims=True)
        yc = y - mu
        var = jnp.mean(yc * yc, axis=0, keepdims=True)
        o_ref[...] = (yc * lax.rsqrt(var + LN_EPS)).T * g_ref[...] + b_ref[...]


def _peer_dense(ht, e0, thr, e1, rank, peer_u, peer_v, ln_g, ln_b, *, alpha, tm, rows_per_step):
    u, inv_u = _quantize_rows(peer_u, QUANT_ROWS, transpose=False)
    vt, inv_v = _quantize_rows(peer_v, QUANT_ROWS, transpose=True)
    inv_v_max = jnp.max(inv_v)
    c1 = inv_u * (inv_v / inv_v_max) * (0.5 * PEER_P_SCALE)
    c2 = inv_u * (2.0 ** -0.5)
    post = jnp.broadcast_to(inv_v_max / PEER_P_SCALE, (1, LANES)).astype(jnp.float32)
    d, t = ht.shape
    n_heads, n_keys, _ = e0.shape
    assert n_keys == LANES
    n_experts = u.shape[0]
    eb = rows_per_step * LANES
    sel_spec = pl.BlockSpec((n_heads, n_keys, tm), lambda i, k: (0, 0, i))
    row_spec = pl.BlockSpec((n_heads, rows_per_step, tm), lambda i, k: (0, k, i))
    vec_spec = pl.BlockSpec((1, d), lambda i, k: (0, 0))
    return pl.pallas_call(
        functools.partial(_peer_dense_kernel, n_heads=n_heads, rows_per_step=rows_per_step,
                          alpha=alpha),
        grid=(t // tm, n_experts // eb),
        in_specs=[pl.BlockSpec((d, tm), lambda i, k: (0, i)),
                  row_spec, row_spec, sel_spec, sel_spec,
                  pl.BlockSpec((eb, d), lambda i, k: (k, 0)),
                  pl.BlockSpec((d, eb), lambda i, k: (0, k)),
                  pl.BlockSpec((eb, LANES), lambda i, k: (k, 0)),
                  pl.BlockSpec((eb, LANES), lambda i, k: (k, 0)),
                  pl.BlockSpec((1, LANES), lambda i, k: (0, 0)),
                  vec_spec, vec_spec],
        out_specs=pl.BlockSpec((tm, d), lambda i, k: (i, 0)),
        out_shape=jax.ShapeDtypeStruct((t, d), jnp.float32),
        scratch_shapes=[pltpu.VMEM((d, tm), FP8),
                        pltpu.VMEM((SUBLANES, LANES), jnp.float32),
                        pltpu.VMEM((d, tm), jnp.float32),
                        pltpu.VMEM((eb, tm), jnp.bfloat16),
                        pltpu.VMEM((eb, tm), FP8)],
        compiler_params=pltpu.CompilerParams(
            dimension_semantics=("parallel", "arbitrary"),
            vmem_limit_bytes=LARGE_VMEM_LIMIT_BYTES),
        name="peer_dense",
    )(ht, e0, thr, e1, rank, u, vt, c1, c2, post, ln_g[None, :], ln_b[None, :])


def _fold_keys_kernel(keys_ref, wq_ref, o_ref):
    o_ref[...] = lax.dot_general(
        keys_ref[0], wq_ref[...], (((1,), (1,)), ((), ())),
        precision=lax.Precision.HIGHEST, preferred_element_type=jnp.float32)


def _fold_keys(keys, wq):
    n_blocks, n_keys, half_key = keys.shape
    d = wq.shape[0]
    return pl.pallas_call(
        _fold_keys_kernel,
        grid=(n_blocks,),
        in_specs=[pl.BlockSpec((1, n_keys, half_key), lambda b: (b, 0, 0)),
                  pl.BlockSpec((d, half_key), lambda b: (0, b))],
        out_specs=pl.BlockSpec((n_keys, d), lambda b: (b, 0)),
        out_shape=jax.ShapeDtypeStruct((n_blocks * n_keys, d), jnp.float32),
        compiler_params=pltpu.CompilerParams(dimension_semantics=("parallel",)),
        name="fold_keys",
    )(keys, wq)


def _batcher_pairs(n):
    pairs = []
    p = 1
    while p < n:
        k = p
        while k >= 1:
            for j in range(k % p, n - k, 2 * k):
                for i in range(min(k, n - j - k)):
                    if (i + j) // (2 * p) == (i + j + k) // (2 * p):
                        pairs.append((i + j, i + j + k))
            k //= 2
        p *= 2
    return pairs


_SORT_PAIRS = _batcher_pairs(TOPK)
_MERGE_PAIRS = [(i, i + d) for d in (8, 4, 2, 1) for i in range(TOPK) if not i & d]
_SUBLANE_SHIFTS = (4, 2, 1)
UNRANKED = 100.0


def _compare_exchange(v, pairs):
    v = list(v)
    for i, j in pairs:
        v[i], v[j] = jnp.maximum(v[i], v[j]), jnp.minimum(v[i], v[j])
    return v


def _top16(tiles):
    v = _compare_exchange(tiles, _SORT_PAIRS)
    for shift in _SUBLANE_SHIFTS:
        other = [pltpu.roll(x, shift, 0) for x in v]
        v = [jnp.maximum(v[r], other[TOPK - 1 - r]) for r in range(TOPK)]
        v = _compare_exchange(v, _MERGE_PAIRS)
    return v


def _stack_rows(tiles, sub):
    out = tiles[0]
    for r in range(1, SUBLANES):
        out = jnp.where(sub == r, tiles[r], out)
    return out


def _peer_select_kernel(sc_ref, e0_ref, thr_ref, e1_ref, rank_ref, *, n_heads, n_keys):
    tl = sc_ref.shape[1]
    reps = n_keys // SUBLANES
    half = TOPK // 2
    sub = lax.broadcasted_iota(jnp.int32, (SUBLANES, tl), 0)
    spread = lambda tile: pltpu.repeat(tile, reps, axis=0)

    def head_body(h, carry):
        r0 = pl.multiple_of(h * 2 * n_keys, 2 * n_keys)
        s0 = sc_ref[pl.ds(r0, n_keys), :]
        s1 = sc_ref[pl.ds(r0 + n_keys, n_keys), :]
        a = _top16([s0[SUBLANES * r:SUBLANES * (r + 1), :] for r in range(reps)])
        b = _top16([s1[SUBLANES * r:SUBLANES * (r + 1), :] for r in range(reps)])
        b_lo, b_hi = _stack_rows(b[:half], sub), _stack_rows(b[half:], sub)
        a_hi = _stack_rows(a[half:], sub)
        cand = ([a[0] + b_lo, a[0] + b_hi] + [a[p] + b_lo for p in range(1, half)]
                + [a_hi + b[0]])
        cand += [jnp.full((SUBLANES, tl), -jnp.inf, jnp.float32)] * (TOPK - len(cand))
        best = _top16(cand)
        tau = best[TOPK - 1]
        zsum = sum(jnp.exp(c - best[0]) for c in best)
        thr = jnp.zeros((n_keys, tl), jnp.float32)
        rank1 = jnp.full((n_keys, tl), UNRANKED, jnp.float32)
        for p in reversed(range(TOPK)):
            n_sel = (jnp.where(a[p] + b_lo >= tau, 1.0, 0.0)
                     + jnp.where(a[p] + b_hi >= tau, 1.0, 0.0))
            for shift in _SUBLANE_SHIFTS:
                n_sel = n_sel + pltpu.roll(n_sel, shift, 0)
            thr = jnp.where(s0 == spread(a[p]), spread(n_sel), thr)
            rank1 = jnp.where(s1 == spread(b[p]), float(p), rank1)
        e0_ref[h] = jnp.exp(s0 - spread(a[0])) / spread(zsum)
        thr_ref[h] = thr
        e1_ref[h] = jnp.exp(s1 - spread(b[0])).astype(e1_ref.dtype)
        rank_ref[h] = rank1.astype(rank_ref.dtype)
        return carry

    lax.fori_loop(0, n_heads, head_body, 0)


def _peer_select(scores_t, n_heads, n_keys, tl):
    rows, t = scores_t.shape
    assert rows == n_heads * 2 * n_keys and n_keys == SUBLANES * TOPK and TOPK == 2 * SUBLANES
    out_spec = pl.BlockSpec((n_heads, n_keys, tl), lambda i: (0, 0, i))
    f32 = jax.ShapeDtypeStruct((n_heads, n_keys, t), jnp.float32)
    b16 = jax.ShapeDtypeStruct((n_heads, n_keys, t), jnp.bfloat16)
    return pl.pallas_call(
        functools.partial(_peer_select_kernel, n_heads=n_heads, n_keys=n_keys),
        grid=(t // tl,),
        in_specs=[pl.BlockSpec((rows, tl), lambda i: (0, i))],
        out_specs=[out_spec, out_spec, out_spec, out_spec],
        out_shape=[f32, f32, b16, b16],
        compiler_params=pltpu.CompilerParams(
            dimension_semantics=("parallel",), vmem_limit_bytes=VMEM_LIMIT_BYTES),
        name="peer_select",
    )(scores_t)


def _log_sigmoid(x):
    return jnp.minimum(x, 0.0) - jnp.log1p(jnp.exp(-jnp.abs(x)))


def _mlstm_kernel(q_ref, k_ref, v_ref, o_ref, g_ref, gt_ref,
                  qm_ref, km_ref, vm_ref, om_ref, gm_ref, gmt_ref,
                  bias_ref, biast_ref, nw_ref, y_ref, c_ref, m_ref,
                  *, n_heads, head_qk, head_v):
    c = pl.program_id(1)
    tc = q_ref.shape[0]
    f32, bf = jnp.float32, jnp.bfloat16
    first = c == 0

    @pl.when(first)
    def _():
        c_ref[...] = jnp.zeros_like(c_ref)
        m_ref[...] = jnp.zeros_like(m_ref)

    q = jnp.where(first, qm_ref[...], q_ref[...])
    k = jnp.where(first, km_ref[...], k_ref[...])
    v = jnp.where(first, vm_ref[...], v_ref[...])
    o = jnp.where(first, om_ref[...], o_ref[...])
    g = jnp.where(first, gm_ref[...], g_ref[...]) + bias_ref[...]
    gt = jnp.where(first, gmt_ref[...], gt_ref[...]) + biast_ref[...]

    row = lax.broadcasted_iota(jnp.int32, (tc, tc), 0)
    col = lax.broadcasted_iota(jnp.int32, (tc, tc), 1)
    causal = col <= row
    tri = jnp.where(causal, 1.0, 0.0).astype(f32)
    b_cols = jnp.dot(tri, _log_sigmoid(g), precision=lax.Precision.HIGHEST,
                     preferred_element_type=f32)
    b_rows = lax.dot_general(_log_sigmoid(gt), tri, (((1,), (1,)), ((), ())),
                             precision=lax.Precision.HIGHEST, preferred_element_type=f32)
    ln_scale = -0.5 * math.log(head_qk)
    ones_blk = jnp.where(lax.broadcasted_iota(jnp.int32, (tc, LANES), 1) == 0, 1.0, 0.0).astype(bf)

    for h in range(n_heads):
        qh = q[:, h * head_qk:(h + 1) * head_qk]
        kh = k[:, h * head_qk:(h + 1) * head_qk]
        v_ext = jnp.concatenate([v[:, h * head_v:(h + 1) * head_v], ones_blk], axis=1)
        li_col = g[:, h:h + 1]
        b_col = b_cols[:, n_heads + h:n_heads + h + 1]
        li_row = gt[h:h + 1, :]
        b_row = b_rows[n_heads + h:n_heads + h + 1, :]
        m_prev = m_ref[h:h + 1, 0:1]
        d = jnp.where(causal, b_col + (li_row - b_row), -jnp.inf)
        inter = b_col + m_prev
        m_t = jnp.maximum(inter, jnp.max(d, axis=1, keepdims=True))
        w = jnp.exp(d - (m_t - ln_scale))
        s = lax.dot_general(qh, kh, (((1,), (1,)), ((), ())), preferred_element_type=f32) * w
        sc_inter = jnp.exp(inter - (m_t - ln_scale))
        c_ext = c_ref[h]
        num_ext = (jnp.dot(s.astype(bf), v_ext, preferred_element_type=f32)
                   + sc_inter * jnp.dot(qh, c_ext.astype(bf), preferred_element_type=f32))
        den = num_ext[:, head_v:head_v + 1]
        hm = num_ext[:, :head_v] / jnp.maximum(jnp.abs(den), jnp.exp(-m_t))
        hm = hm * lax.rsqrt(jnp.mean(hm * hm, axis=1, keepdims=True) + LN_EPS)
        gate_o = jax.nn.sigmoid(o[:, h * head_v:(h + 1) * head_v].astype(f32))
        y_ref[:, h * head_v:(h + 1) * head_v] = (
            gate_o * (hm * nw_ref[:, h * head_v:(h + 1) * head_v])).astype(y_ref.dtype)
        b_last = b_col[tc - 1:tc, :]
        dl = b_last - b_col + li_col
        m_new = jnp.maximum(b_last + m_prev, jnp.max(dl, axis=0, keepdims=True))
        wv = (jnp.exp(dl - m_new) * v_ext.astype(f32)).astype(bf)
        c_ref[h] = (jnp.exp(b_last + m_prev - m_new) * c_ext
                    + lax.dot_general(kh, wv, (((0,), (0,)), ((), ())),
                                      preferred_element_type=f32))
        m_ref[h:h + 1, :] = jnp.broadcast_to(m_new, (1, LANES))


def _mlstm(proj_x, proj_m, gif_x, gt_x, gif_m, b_if, norm_w, *, bsz, seq, n_heads, head_qk,
           head_v, off_q, off_v):
    tc = min(MLSTM_CHUNK, seq)
    assert seq % tc == 0
    n_meta = proj_m.shape[0]
    d_mqk, d_mv = n_heads * head_qk, n_heads * head_v
    assert off_q % d_mqk == 0 and off_v % d_mv == 0 and n_meta <= tc
    nc_x = seq // tc
    bf = jnp.bfloat16
    lead = tc - n_meta
    pad_rows = lambda a: jnp.pad(a, ((lead, 0), (0, 0)))
    qm = pad_rows(proj_m[:, off_q:off_q + d_mqk])
    km = pad_rows(proj_m[:, off_q + d_mqk:off_q + 2 * d_mqk])
    vm = pad_rows(proj_m[:, off_v:off_v + d_mv])
    om = pad_rows(proj_m[:, off_v + d_mv:off_v + 2 * d_mv])
    pad_gate = jnp.concatenate([jnp.full((lead, n_heads), I_GATE_PAD, jnp.float32),
                                jnp.full((lead, n_heads), -I_GATE_PAD, jnp.float32),
                                jnp.zeros((lead, LANES - 2 * n_heads), jnp.float32)], axis=1)
    gm = jnp.concatenate([pad_gate, gif_m], axis=0)
    bias = jnp.pad(b_if, (0, LANES - 2 * n_heads))[None, :]
    biast = jnp.broadcast_to(b_if[:, None], (2 * n_heads, tc))
    gt_m = gm[:, :2 * n_heads].T

    xrow = lambda b, c: b * nc_x + jnp.maximum(c - 1, 0)
    const = lambda b, c: (0, 0)
    return pl.pallas_call(
        functools.partial(_mlstm_kernel, n_heads=n_heads, head_qk=head_qk, head_v=head_v),
        grid=(bsz, nc_x + 1),
        in_specs=[pl.BlockSpec((tc, d_mqk), lambda b, c: (xrow(b, c), off_q // d_mqk)),
                  pl.BlockSpec((tc, d_mqk), lambda b, c: (xrow(b, c), off_q // d_mqk + 1)),
                  pl.BlockSpec((tc, d_mv), lambda b, c: (xrow(b, c), off_v // d_mv)),
                  pl.BlockSpec((tc, d_mv), lambda b, c: (xrow(b, c), off_v // d_mv + 1)),
                  pl.BlockSpec((tc, LANES), lambda b, c: (xrow(b, c), 0)),
                  pl.BlockSpec((2 * n_heads, tc), lambda b, c: (0, xrow(b, c))),
                  pl.BlockSpec((tc, d_mqk), const), pl.BlockSpec((tc, d_mqk), const),
                  pl.BlockSpec((tc, d_mv), const), pl.BlockSpec((tc, d_mv), const),
                  pl.BlockSpec((tc, LANES), const), pl.BlockSpec((2 * n_heads, tc), const),
                  pl.BlockSpec((1, LANES), const), pl.BlockSpec((2 * n_heads, tc), const),
                  pl.BlockSpec((1, d_mv), const)],
        out_specs=pl.BlockSpec((tc, d_mv), lambda b, c: (xrow(b, c), 0)),
        out_shape=jax.ShapeDtypeStruct((bsz * seq, d_mv), bf),
        scratch_shapes=[pltpu.VMEM((n_heads, head_qk, head_v + LANES), jnp.float32),
                        pltpu.VMEM((n_heads, LANES), jnp.float32)],
        compiler_params=pltpu.CompilerParams(
            dimension_semantics=("parallel", "arbitrary"),
            vmem_limit_bytes=VMEM_LIMIT_BYTES),
        name="mlstm",
    )(proj_x, proj_x, proj_x, proj_x, gif_x, gt_x, qm, km, vm, om, gm, gt_m,
      bias, biast, norm_w[None, :])


def _mix_kernel(x_ref, cb_ref, cc_ref, cx_ref, cch_ref, cxh_ref, ccm_ref, cxm_ref,
                yb_ref, ga_ref, gb_ref, pa_ref, pb_ref, wo_ref, cw_ref,
                g0_ref, b0_ref, g1_ref, b1_ref, h1t_ref,
                *, tiles_per_batch, alpha):
    i = pl.program_id(0)
    f32, bf = jnp.float32, jnp.bfloat16
    tm = x_ref.shape[0]
    z = cc_ref[...].astype(f32) * cx_ref[...].astype(f32)
    batch_start = (i % tiles_per_batch) == 0
    halo = jnp.where(batch_start,
                     ccm_ref[...].astype(f32) * cxm_ref[...].astype(f32),
                     cch_ref[...].astype(f32) * cxh_ref[...].astype(f32))
    sub = lax.broadcasted_iota(jnp.int32, halo.shape, 0)
    n_halo = halo.shape[0]
    y = cw_ref[2:3, :] * z
    for shift in (1, 2):
        rolled = pltpu.roll(z, shift, 0)
        head = jnp.where(sub < shift, pltpu.roll(halo, shift, 0), rolled[:n_halo])
        y = y + cw_ref[2 - shift:3 - shift, :] * jnp.concatenate([head, rolled[n_halo:]], axis=0)
    y_a = (cb_ref[...].astype(f32) * y).astype(bf)
    za = jnp.dot(y_a, pa_ref[...], preferred_element_type=f32)
    zb = jnp.dot(yb_ref[...], pb_ref[...], preferred_element_type=f32)
    merged = (jax.nn.sigmoid(ga_ref[...].astype(f32)) * za
              + jax.nn.sigmoid(gb_ref[...].astype(f32)) * zb).astype(bf)
    mix = jnp.dot(merged, wo_ref[...], preferred_element_type=f32)
    h0 = _ln_rows(x_ref[...], g0_ref[...], b0_ref[...])
    h1t_ref[...] = _ln_rows(alpha * h0 + mix, g1_ref[...], b1_ref[...]).T


def _mix(x2, proj_x, proj_m, y_b, p_a, p_b, w_o, conv_w, ln0_g, ln0_b, ln1_g, ln1_b,
         *, seq, off_g, alpha, tm):
    t_x, d = x2.shape
    d_conv = conv_w.shape[-1]
    d_mv = y_b.shape[-1]
    n_meta = proj_m.shape[0]
    halo = SUBLANES
    assert seq % tm == 0 and tm % halo == 0 and n_meta % halo == 0 and off_g % d == 0
    assert conv_w.shape[0] == 3
    tile = lambda j: pl.BlockSpec((tm, d_conv), lambda i: (i, j))
    prev = lambda j: pl.BlockSpec((halo, d_conv),
                                  lambda i: (jnp.maximum(i * (tm // halo) - 1, 0), j))
    meta = lambda j: pl.BlockSpec((halo, d_conv), lambda i: (n_meta // halo - 1, j))
    const = lambda shape: pl.BlockSpec(shape, lambda i: (0, 0), pipeline_mode=pl.Buffered(1))
    vec = lambda a: a[None, :]
    return pl.pallas_call(
        functools.partial(_mix_kernel, tiles_per_batch=seq // tm, alpha=alpha),
        grid=(t_x // tm,),
        in_specs=[pl.BlockSpec((tm, d), lambda i: (i, 0)),
                  tile(0), tile(1), tile(2), prev(1), prev(2), meta(1), meta(2),
                  pl.BlockSpec((tm, d_mv), lambda i: (i, 0)),
                  pl.BlockSpec((tm, d), lambda i: (i, off_g // d)),
                  pl.BlockSpec((tm, d), lambda i: (i, off_g // d + 1)),
                  const(p_a.shape), const(p_b.shape), const(w_o.shape), const(conv_w.shape),
                  const((1, d)), const((1, d)), const((1, d)), const((1, d))],
        out_specs=pl.BlockSpec((d, tm), lambda i: (0, i)),
        out_shape=jax.ShapeDtypeStruct((d, t_x), jnp.float32),
        compiler_params=pltpu.CompilerParams(
            dimension_semantics=("parallel",), vmem_limit_bytes=VMEM_LIMIT_BYTES),
        name="mix",
    )(x2, proj_x, proj_x, proj_x, proj_x, proj_x, proj_m, proj_m, y_b, proj_x, proj_x,
      p_a, p_b, w_o, conv_w, vec(ln0_g), vec(ln0_b), vec(ln1_g), vec(ln1_b))


def kernel(x, meta_tokens, ln0_g, ln0_b, w_in, b_if, conv_w, mh_norm_w, p_a, p_b, w_o,
           ln1_g, ln1_b, peer_wq, peer_keys, peer_u, peer_v, ln2_g, ln2_b):
    depth = w_in.shape[0]
    assert depth == 1
    bsz, seq, d = x.shape
    n_meta = meta_tokens.shape[0]
    n_in = w_in.shape[-1]
    d_conv = conv_w.shape[-1]
    d_mv = mh_norm_w.shape[-1]
    n_mh = b_if.shape[-1] // 2
    d_mqk = (n_in - 3 * d_conv - 2 * d_mv - 2 * n_mh - 2 * d) // 2
    head_qk = d_mqk // n_mh
    head_v = d_mv // n_mh
    alpha = (2 * depth) ** 0.25
    bf = jnp.bfloat16
    t_x = bsz * seq

    x2 = x.reshape(t_x, d)
    gate_off = 3 * d_conv + 2 * d_mqk + 2 * d_mv
    w = w_in[0]
    w_head = w[:, :gate_off].astype(bf)
    w_tail = w[:, gate_off + 2 * n_mh:].astype(bf)
    w_gif = w[:, gate_off:gate_off + LANES].astype(bf)
    inproj_tn = min(INPROJ_TILE_N, w_tail.shape[1])
    proj_x, gif_x, gt_x = _inproj(x2, ln0_g, ln0_b, w_head, w_tail, w_gif,
                                  INPROJ_TILE_M, inproj_tn)
    proj_m, gif_m, _ = _inproj(meta_tokens, ln0_g, ln0_b, w_head, w_tail, w_gif,
                               INPROJ_TILE_M, inproj_tn)

    off_q = 3 * d_conv
    off_v = off_q + 2 * d_mqk
    off_g = off_v + 2 * d_mv
    y_b = _mlstm(proj_x, proj_m, gif_x, gt_x, gif_m, b_if[0], mh_norm_w[0], bsz=bsz, seq=seq,
                 n_heads=n_mh, head_qk=head_qk, head_v=head_v, off_q=off_q, off_v=off_v)
    h1_t = _mix(x2, proj_x, proj_m, y_b,
                p_a[0].astype(bf), p_b[0].astype(bf), w_o[0].astype(bf), conv_w[0],
                ln0_g, ln0_b, ln1_g[0], ln1_b[0], seq=seq, off_g=off_g, alpha=alpha,
                tm=MIX_TILE)

    n_ph, _, n_keys, half_key = peer_keys.shape[1:]
    w_fold = _fold_keys(peer_keys[0].reshape(n_ph * 2, n_keys, half_key), peer_wq[0])
    scores_t = _matmul(w_fold, h1_t, jnp.float32, 1024, 512)
    e0, thr, e1, rank = _peer_select(scores_t, n_ph, n_keys, LANES)
    out = _peer_dense(h1_t, e0, thr, e1, rank, peer_u[0], peer_v[0],
                      ln2_g[0], ln2_b[0], alpha=alpha, tm=PEER_TOKEN_TILE,
                      rows_per_step=PEER_ROWS_PER_STEP)
    return out.reshape(bsz, seq, d)
```

```python
import functools
import math

import jax
import jax.numpy as jnp
from jax import lax
from jax.experimental import pallas as pl
from jax.experimental.pallas import tpu as pltpu

LANES = 128
SUBLANES = 8
BF16_SUBLANES = 16
VMEM_LIMIT_BYTES = 56 * 1024 * 1024
LARGE_VMEM_LIMIT_BYTES = 58 * 1024 * 1024

TOPK = 16
LN_EPS = 1e-5
I_GATE_PAD = -1e30
MLSTM_CHUNK = 256
MIX_TILE = 256
INPROJ_TILE_M = 1024
INPROJ_TILE_N = 1024
PEER_TOKEN_TILE = 512
PEER_ROWS_PER_STEP = 8
QUANT_ROWS = 512
FP8 = jnp.float8_e4m3fn
FP8_CLIP = 448.0
FP8_TARGET = 256.0
PEER_P_SCALE = 64.0


def _ln_rows(x, g, b):
    mu = jnp.mean(x, axis=1, keepdims=True)
    xc = x - mu
    var = jnp.mean(xc * xc, axis=1, keepdims=True)
    return xc * lax.rsqrt(var + LN_EPS) * g + b


def _inproj_kernel(x_ref, g_ref, b_ref, w_ref, wg_ref, o_ref, og_ref, ogt_ref, h_ref):
    @pl.when(pl.program_id(1) == 0)
    def _():
        h = _ln_rows(x_ref[...], g_ref[...], b_ref[...]).astype(h_ref.dtype)
        h_ref[...] = h
        gates = jnp.dot(h, wg_ref[...], preferred_element_type=jnp.float32)
        og_ref[...] = gates
        ogt_ref[...] = gates.T

    o_ref[...] = jnp.dot(h_ref[...], w_ref[...],
                         preferred_element_type=jnp.float32).astype(o_ref.dtype)


def _inproj(x2, ln_g, ln_b, w_main, w_gate, tm, tn):
    m, d = x2.shape
    n = w_main.shape[1]
    tm = min(tm, m)
    assert m % tm == 0 and n % tn == 0 and w_gate.shape[1] == LANES
    const = lambda shape: pl.BlockSpec(shape, lambda i, j: (0, 0))
    return pl.pallas_call(
        _inproj_kernel,
        grid=(m // tm, n // tn),
        in_specs=[pl.BlockSpec((tm, d), lambda i, j: (i, 0)),
                  const((1, d)), const((1, d)),
                  pl.BlockSpec((d, tn), lambda i, j: (0, j)),
                  const((d, LANES))],
        out_specs=[pl.BlockSpec((tm, tn), lambda i, j: (i, j)),
                   pl.BlockSpec((tm, LANES), lambda i, j: (i, 0)),
                   pl.BlockSpec((LANES, tm), lambda i, j: (0, i))],
        out_shape=[jax.ShapeDtypeStruct((m, n), jnp.bfloat16),
                   jax.ShapeDtypeStruct((m, LANES), jnp.float32),
                   jax.ShapeDtypeStruct((LANES, m), jnp.float32)],
        scratch_shapes=[pltpu.VMEM((tm, d), jnp.bfloat16)],
        compiler_params=pltpu.CompilerParams(
            dimension_semantics=("parallel", "arbitrary"),
            vmem_limit_bytes=VMEM_LIMIT_BYTES),
        name="inproj",
    )(x2, ln_g[None, :], ln_b[None, :], w_main, w_gate)


def _mm_kernel(a_ref, b_ref, o_ref):
    bf = jnp.bfloat16
    o_ref[...] = jnp.dot(a_ref[...].astype(bf), b_ref[...].astype(bf),
                         preferred_element_type=jnp.float32).astype(o_ref.dtype)


def _matmul(a, b, out_dtype, tm, tn):
    m, k = a.shape
    _, n = b.shape
    tm = min(tm, m)
    tn = min(tn, n)
    assert m % tm == 0 and n % tn == 0
    return pl.pallas_call(
        _mm_kernel,
        grid=(m // tm, n // tn),
        in_specs=[pl.BlockSpec((tm, k), lambda i, j: (i, 0)),
                  pl.BlockSpec((k, tn), lambda i, j: (0, j))],
        out_specs=pl.BlockSpec((tm, tn), lambda i, j: (i, j)),
        out_shape=jax.ShapeDtypeStruct((m, n), out_dtype),
        compiler_params=pltpu.CompilerParams(
            dimension_semantics=("parallel", "parallel"),
            vmem_limit_bytes=VMEM_LIMIT_BYTES),
        name="matmul",
    )(a, b)


def _pow2_scale(amax):
    safe = jnp.where(amax > 0, amax, FP8_TARGET)
    return jnp.exp2(jnp.floor(jnp.log2(FP8_TARGET / safe)))


def _quantize_rows_kernel(x_ref, q_ref, inv_ref, *, transpose):
    x = x_ref[...]
    scale = _pow2_scale(jnp.max(jnp.abs(x), axis=1, keepdims=True))
    xs = x * scale
    q_ref[...] = (xs.T if transpose else xs).astype(q_ref.dtype)
    inv_ref[...] = jnp.broadcast_to(1.0 / scale, inv_ref.shape)


def _quantize_rows(x, rows, transpose):
    n, d = x.shape
    rows = min(rows, n)
    assert n % rows == 0
    q_shape, q_spec = (((d, n), pl.BlockSpec((d, rows), lambda i: (0, i))) if transpose
                       else ((n, d), pl.BlockSpec((rows, d), lambda i: (i, 0))))
    return pl.pallas_call(
        functools.partial(_quantize_rows_kernel, transpose=transpose),
        grid=(n // rows,),
        in_specs=[pl.BlockSpec((rows, d), lambda i: (i, 0))],
        out_specs=[q_spec, pl.BlockSpec((rows, LANES), lambda i: (i, 0))],
        out_shape=[jax.ShapeDtypeStruct(q_shape, FP8),
                   jax.ShapeDtypeStruct((n, LANES), jnp.float32)],
        compiler_params=pltpu.CompilerParams(
            dimension_semantics=("parallel",), vmem_limit_bytes=VMEM_LIMIT_BYTES),
        name="quantize_rows",
    )(x)


def _peer_dense_kernel(ht_ref, e0_ref, thr_ref, e1_ref, rank_ref, u_ref, vt_ref, c1_ref, c2_ref,
                       post_ref, g_ref, b_ref, o_ref, hb_ref, sh_ref, acc_ref, w_ref, p_ref,
                       *, n_heads, rows_per_step, alpha):
    k = pl.program_id(1)
    bf, f8 = jnp.bfloat16, FP8
    tm = ht_ref.shape[1]
    zero = jnp.zeros((), bf)

    def row_tile(ref, h, ii):
        row = jnp.broadcast_to(ref[h, ii:ii + 1, :], (BF16_SUBLANES, tm)).astype(bf)
        return pltpu.repeat(row, LANES // BF16_SUBLANES, axis=0)

    @pl.when(k == 0)
    def _():
        acc_ref[...] = jnp.zeros_like(acc_ref)
        h = ht_ref[...]
        amax = jnp.max(jnp.max(jnp.abs(h), axis=0, keepdims=True), axis=1, keepdims=True)
        s_h = _pow2_scale(amax)
        hb_ref[...] = (h * s_h).astype(f8)
        sh_ref[...] = jnp.broadcast_to(1.0 / s_h, sh_ref.shape)

    for ii in range(rows_per_step):
        w = jnp.zeros((LANES, tm), bf)
        for h in range(n_heads):
            sel = rank_ref[h] < row_tile(thr_ref, h, ii)
            w = w + jnp.where(sel, e1_ref[h], zero) * row_tile(e0_ref, h, ii)
        w_ref[ii * LANES:(ii + 1) * LANES, :] = w
    a = jnp.dot(u_ref[...], hb_ref[...], preferred_element_type=jnp.float32)
    inv_sh = sh_ref[0:1, 0:1]
    c1 = pltpu.repeat((c1_ref[...] * inv_sh).astype(bf), tm // LANES, axis=1)
    c2 = pltpu.repeat((c2_ref[...] * inv_sh).astype(bf), tm // LANES, axis=1)
    ab = a.astype(bf)
    act = (ab * c1) * (1.0 + lax.erf(ab * c2))
    p = jnp.clip(w_ref[...] * act, -FP8_CLIP, FP8_CLIP)
    p_ref[...] = p.astype(f8)
    acc_ref[...] += jnp.dot(vt_ref[...], p_ref[...], preferred_element_type=jnp.float32)

    @pl.when(k == pl.num_programs(1) - 1)
    def _():
        y = alpha * ht_ref[...] + acc_ref[...] * post_ref[0:1, 0:1]
        mu = jnp.mean(y, axis=0, keepdims=True)
        yc = y - mu
        var = jnp.mean(yc * yc, axis=0, keepdims=True)
        o_ref[...] = (yc * lax.rsqrt(var + LN_EPS)).T * g_ref[...] + b_ref[...]


def _peer_dense(ht, e0, thr, e1, rank, peer_u, peer_v, ln_g, ln_b, *, alpha, tm, rows_per_step):
    u, inv_u = _quantize_rows(peer_u, QUANT_ROWS, transpose=False)
    vt, inv_v = _quantize_rows(peer_v, QUANT_ROWS, transpose=True)
    inv_v_max = jnp.max(inv_v)
    c1 = inv_u * (inv_v / inv_v_max) * (0.5 * PEER_P_SCALE)
    c2 = inv_u * (2.0 ** -0.5)
    post = jnp.broadcast_to(inv_v_max / PEER_P_SCALE, (1, LANES)).astype(jnp.float32)
    d, t = ht.shape
    n_heads, n_keys, _ = e0.shape
    assert n_keys == LANES
    n_experts = u.shape[0]
    eb = rows_per_step * LANES
    sel_spec = pl.BlockSpec((n_heads, n_keys, tm), lambda i, k: (0, 0, i))
    row_spec = pl.BlockSpec((n_heads, rows_per_step, tm), lambda i, k: (0, k, i))
    vec_spec = pl.BlockSpec((1, d), lambda i, k: (0, 0))
    return pl.pallas_call(
        functools.partial(_peer_dense_kernel, n_heads=n_heads, rows_per_step=rows_per_step,
                          alpha=alpha),
        grid=(t // tm, n_experts // eb),
        in_specs=[pl.BlockSpec((d, tm), lambda i, k: (0, i)),
                  row_spec, row_spec, sel_spec, sel_spec,
                  pl.BlockSpec((eb, d), lambda i, k: (k, 0)),
                  pl.BlockSpec((d, eb), lambda i, k: (0, k)),
                  pl.BlockSpec((eb, LANES), lambda i, k: (k, 0)),
                  pl.BlockSpec((eb, LANES), lambda i, k: (k, 0)),
                  pl.BlockSpec((1, LANES), lambda i, k: (0, 0)),
                  vec_spec, vec_spec],
        out_specs=pl.BlockSpec((tm, d), lambda i, k: (i, 0)),
        out_shape=jax.ShapeDtypeStruct((t, d), jnp.float32),
        scratch_shapes=[pltpu.VMEM((d, tm), FP8),
                        pltpu.VMEM((SUBLANES, LANES), jnp.float32),
                        pltpu.VMEM((d, tm), jnp.float32),
                        pltpu.VMEM((eb, tm), jnp.bfloat16),
                        pltpu.VMEM((eb, tm), FP8)],
        compiler_params=pltpu.CompilerParams(
            dimension_semantics=("parallel", "arbitrary"),
            vmem_limit_bytes=LARGE_VMEM_LIMIT_BYTES),
        name="peer_dense",
    )(ht, e0, thr, e1, rank, u, vt, c1, c2, post, ln_g[None, :], ln_b[None, :])


def _fold_keys_kernel(keys_ref, wq_ref, o_ref):
    o_ref[...] = lax.dot_general(
        keys_ref[0], wq_ref[...], (((1,), (1,)), ((), ())),
        precision=lax.Precision.HIGHEST, preferred_element_type=jnp.float32)


def _fold_keys(keys, wq):
    n_blocks, n_keys, half_key = keys.shape
    d = wq.shape[0]
    return pl.pallas_call(
        _fold_keys_kernel,
        grid=(n_blocks,),
        in_specs=[pl.BlockSpec((1, n_keys, half_key), lambda b: (b, 0, 0)),
                  pl.BlockSpec((d, half_key), lambda b: (0, b))],
        out_specs=pl.BlockSpec((n_keys, d), lambda b: (b, 0)),
        out_shape=jax.ShapeDtypeStruct((n_blocks * n_keys, d), jnp.float32),
        compiler_params=pltpu.CompilerParams(dimension_semantics=("parallel",)),
        name="fold_keys",
    )(keys, wq)


def _batcher_pairs(n):
    pairs = []
    p = 1
    while p < n:
        k = p
        while k >= 1:
            for j in range(k % p, n - k, 2 * k):
                for i in range(min(k, n - j - k)):
                    if (i + j) // (2 * p) == (i + j + k) // (2 * p):
                        pairs.append((i + j, i + j + k))
            k //= 2
        p *= 2
    return pairs


_SORT_PAIRS = _batcher_pairs(TOPK)
_MERGE_PAIRS = [(i, i + d) for d in (8, 4, 2, 1) for i in range(TOPK) if not i & d]
_SUBLANE_SHIFTS = (4, 2, 1)
UNRANKED = 100.0


def _compare_exchange(v, pairs):
    v = list(v)
    for i, j in pairs:
        v[i], v[j] = jnp.maximum(v[i], v[j]), jnp.minimum(v[i], v[j])
    return v


def _top16(tiles):
    v = _compare_exchange(tiles, _SORT_PAIRS)
    for shift in _SUBLANE_SHIFTS:
        other = [pltpu.roll(x, shift, 0) for x in v]
        v = [jnp.maximum(v[r], other[TOPK - 1 - r]) for r in range(TOPK)]
        v = _compare_exchange(v, _MERGE_PAIRS)
    return v


def _stack_rows(tiles, sub):
    out = tiles[0]
    for r in range(1, SUBLANES):
        out = jnp.where(sub == r, tiles[r], out)
    return out


def _peer_select_kernel(sc_ref, e0_ref, thr_ref, e1_ref, rank_ref, *, n_heads, n_keys):
    tl = sc_ref.shape[1]
    reps = n_keys // SUBLANES
    half = TOPK // 2
    sub = lax.broadcasted_iota(jnp.int32, (SUBLANES, tl), 0)
    spread = lambda tile: pltpu.repeat(tile, reps, axis=0)

    def head_body(h, carry):
        r0 = pl.multiple_of(h * 2 * n_keys, 2 * n_keys)
        s0 = sc_ref[pl.ds(r0, n_keys), :]
        s1 = sc_ref[pl.ds(r0 + n_keys, n_keys), :]
        a = _top16([s0[SUBLANES * r:SUBLANES * (r + 1), :] for r in range(reps)])
        b = _top16([s1[SUBLANES * r:SUBLANES * (r + 1), :] for r in range(reps)])
        b_lo, b_hi = _stack_rows(b[:half], sub), _stack_rows(b[half:], sub)
        a_hi = _stack_rows(a[half:], sub)
        cand = ([a[0] + b_lo, a[0] + b_hi] + [a[p] + b_lo for p in range(1, half)]
                + [a_hi + b[0]])
        cand += [jnp.full((SUBLANES, tl), -jnp.inf, jnp.float32)] * (TOPK - len(cand))
        best = _top16(cand)
        tau = best[TOPK - 1]
        zsum = sum(jnp.exp(c - best[0]) for c in best)
        thr = jnp.zeros((n_keys, tl), jnp.float32)
        rank1 = jnp.full((n_keys, tl), UNRANKED, jnp.float32)
        for p in reversed(range(TOPK)):
            n_sel = (jnp.where(a[p] + b_lo >= tau, 1.0, 0.0)
                     + jnp.where(a[p] + b_hi >= tau, 1.0, 0.0))
            for shift in _SUBLANE_SHIFTS:
                n_sel = n_sel + pltpu.roll(n_sel, shift, 0)
            thr = jnp.where(s0 == spread(a[p]), spread(n_sel), thr)
            rank1 = jnp.where(s1 == spread(b[p]), float(p), rank1)
        e0_ref[h] = jnp.exp(s0 - spread(a[0])) / spread(zsum)
        thr_ref[h] = thr
        e1_ref[h] = jnp.exp(s1 - spread(b[0])).astype(e1_ref.dtype)
        rank_ref[h] = rank1.astype(rank_ref.dtype)
        return carry

    lax.fori_loop(0, n_heads, head_body, 0)


def _peer_select(scores_t, n_heads, n_keys, tl):
    rows, t = scores_t.shape
    assert rows == n_heads * 2 * n_keys and n_keys == SUBLANES * TOPK and TOPK == 2 * SUBLANES
    out_spec = pl.BlockSpec((n_heads, n_keys, tl), lambda i: (0, 0, i))
    f32 = jax.ShapeDtypeStruct((n_heads, n_keys, t), jnp.float32)
    b16 = jax.ShapeDtypeStruct((n_heads, n_keys, t), jnp.bfloat16)
    return pl.pallas_call(
        functools.partial(_peer_select_kernel, n_heads=n_heads, n_keys=n_keys),
        grid=(t // tl,),
        in_specs=[pl.BlockSpec((rows, tl), lambda i: (0, i))],
        out_specs=[out_spec, out_spec, out_spec, out_spec],
        out_shape=[f32, f32, b16, b16],
        compiler_params=pltpu.CompilerParams(
            dimension_semantics=("parallel",), vmem_limit_bytes=VMEM_LIMIT_BYTES),
        name="peer_select",
    )(scores_t)


def _log_sigmoid(x):
    return jnp.minimum(x, 0.0) - jnp.log1p(jnp.exp(-jnp.abs(x)))


def _mlstm_kernel(q_ref, k_ref, v_ref, o_ref, g_ref, gt_ref,
                  qm_ref, km_ref, vm_ref, om_ref, gm_ref, gmt_ref,
                  bias_ref, biast_ref, nw_ref, y_ref, c_ref, m_ref,
                  *, n_heads, head_qk, head_v):
    c = pl.program_id(0)
    n_batch, tc = q_ref.shape[0], q_ref.shape[1]
    f32, bf = jnp.float32, jnp.bfloat16
    first = c == 0

    @pl.when(first)
    def _():
        c_ref[...] = jnp.zeros_like(c_ref)
        m_ref[...] = jnp.zeros_like(m_ref)

    row = lax.broadcasted_iota(jnp.int32, (tc, tc), 0)
    col = lax.broadcasted_iota(jnp.int32, (tc, tc), 1)
    causal = col <= row
    tri = jnp.where(causal, 1.0, 0.0).astype(f32)
    ln_scale = -0.5 * math.log(head_qk)
    ones_blk = jnp.where(lax.broadcasted_iota(jnp.int32, (tc, LANES), 1) == 0, 1.0, 0.0).astype(bf)

    for b in range(n_batch):
        _mlstm_chunk(b, first, causal, tri, ln_scale, ones_blk,
                     q_ref, k_ref, v_ref, o_ref, g_ref, gt_ref,
                     qm_ref, km_ref, vm_ref, om_ref, gm_ref, gmt_ref,
                     bias_ref, biast_ref, nw_ref, y_ref, c_ref, m_ref,
                     n_heads=n_heads, head_qk=head_qk, head_v=head_v)


def _mlstm_chunk(b, first, causal, tri, ln_scale, ones_blk,
                 q_ref, k_ref, v_ref, o_ref, g_ref, gt_ref,
                 qm_ref, km_ref, vm_ref, om_ref, gm_ref, gmt_ref,
                 bias_ref, biast_ref, nw_ref, y_ref, c_ref, m_ref,
                 *, n_heads, head_qk, head_v):
    tc = q_ref.shape[1]
    f32, bf = jnp.float32, jnp.bfloat16
    q = jnp.where(first, qm_ref[...], q_ref[b])
    k = jnp.where(first, km_ref[...], k_ref[b])
    v = jnp.where(first, vm_ref[...], v_ref[b])
    o = jnp.where(first, om_ref[...], o_ref[b])
    g = jnp.where(first, gm_ref[...], g_ref[b]) + bias_ref[...]
    gt = jnp.where(first, gmt_ref[...], gt_ref[b]) + biast_ref[...]
    b_cols = jnp.dot(tri, _log_sigmoid(g), precision=lax.Precision.HIGHEST,
                     preferred_element_type=f32)
    b_rows = lax.dot_general(_log_sigmoid(gt), tri, (((1,), (1,)), ((), ())),
                             precision=lax.Precision.HIGHEST, preferred_element_type=f32)

    for h in range(n_heads):
        qh = q[:, h * head_qk:(h + 1) * head_qk]
        kh = k[:, h * head_qk:(h + 1) * head_qk]
        v_ext = jnp.concatenate([v[:, h * head_v:(h + 1) * head_v], ones_blk], axis=1)
        li_col = g[:, h:h + 1]
        b_col = b_cols[:, n_heads + h:n_heads + h + 1]
        li_row = gt[h:h + 1, :]
        b_row = b_rows[n_heads + h:n_heads + h + 1, :]
        m_prev = m_ref[b, h:h + 1, 0:1]
        d = jnp.where(causal, b_col + (li_row - b_row), -jnp.inf)
        inter = b_col + m_prev
        m_t = jnp.maximum(inter, jnp.max(d, axis=1, keepdims=True))
        w = jnp.exp(d - (m_t - ln_scale))
        s = lax.dot_general(qh, kh, (((1,), (1,)), ((), ())), preferred_element_type=f32) * w
        sc_inter = jnp.exp(inter - (m_t - ln_scale))
        c_ext = c_ref[b, h]
        num_ext = (jnp.dot(s.astype(bf), v_ext, preferred_element_type=f32)
                   + sc_inter * jnp.dot(qh, c_ext.astype(bf), preferred_element_type=f32))
        den = num_ext[:, head_v:head_v + 1]
        hm = num_ext[:, :head_v] / jnp.maximum(jnp.abs(den), jnp.exp(-m_t))
        hm = hm * lax.rsqrt(jnp.mean(hm * hm, axis=1, keepdims=True) + LN_EPS)
        gate_o = jax.nn.sigmoid(o[:, h * head_v:(h + 1) * head_v].astype(f32))
        y_ref[b, :, h * head_v:(h + 1) * head_v] = (
            gate_o * (hm * nw_ref[:, h * head_v:(h + 1) * head_v])).astype(y_ref.dtype)
        b_last = b_col[tc - 1:tc, :]
        dl = b_last - b_col + li_col
        m_new = jnp.maximum(b_last + m_prev, jnp.max(dl, axis=0, keepdims=True))
        wv = (jnp.exp(dl - m_new) * v_ext.astype(f32)).astype(bf)
        c_ref[b, h] = (jnp.exp(b_last + m_prev - m_new) * c_ext
                       + lax.dot_general(kh, wv, (((0,), (0,)), ((), ())),
                                         preferred_element_type=f32))
        m_ref[b, h:h + 1, :] = jnp.broadcast_to(m_new, (1, LANES))


def _mlstm(proj_x, proj_m, gif_x, gt_x, gif_m, b_if, norm_w, *, bsz, seq, n_heads, head_qk,
           head_v, off_q, off_v):
    tc = min(MLSTM_CHUNK, seq)
    assert seq % tc == 0
    n_meta = proj_m.shape[0]
    d_mqk, d_mv = n_heads * head_qk, n_heads * head_v
    assert off_q % d_mqk == 0 and off_v % d_mv == 0 and n_meta <= tc
    nc_x = seq // tc
    bf = jnp.bfloat16
    lead = tc - n_meta
    pad_rows = lambda a: jnp.pad(a, ((lead, 0), (0, 0)))
    qm = pad_rows(proj_m[:, off_q:off_q + d_mqk])
    km = pad_rows(proj_m[:, off_q + d_mqk:off_q + 2 * d_mqk])
    vm = pad_rows(proj_m[:, off_v:off_v + d_mv])
    om = pad_rows(proj_m[:, off_v + d_mv:off_v + 2 * d_mv])
    pad_gate = jnp.concatenate([jnp.full((lead, n_heads), I_GATE_PAD, jnp.float32),
                                jnp.full((lead, n_heads), -I_GATE_PAD, jnp.float32),
                                jnp.zeros((lead, LANES - 2 * n_heads), jnp.float32)], axis=1)
    gm = jnp.concatenate([pad_gate, gif_m], axis=0)
    bias = jnp.pad(b_if, (0, LANES - 2 * n_heads))[None, :]
    biast = jnp.broadcast_to(b_if[:, None], (2 * n_heads, tc))
    gt_m = gm[:, :2 * n_heads].T

    proj3 = proj_x.reshape(bsz, seq, proj_x.shape[1])
    gif3 = gif_x.reshape(bsz, seq, LANES)
    gt3 = gt_x.reshape(LANES, bsz, seq).transpose(1, 0, 2)
    xchunk = lambda c: jnp.maximum(c - 1, 0)
    const = lambda c: (0, 0)
    y = pl.pallas_call(
        functools.partial(_mlstm_kernel, n_heads=n_heads, head_qk=head_qk, head_v=head_v),
        grid=(nc_x + 1,),
        in_specs=[pl.BlockSpec((bsz, tc, d_mqk), lambda c: (0, xchunk(c), off_q // d_mqk)),
                  pl.BlockSpec((bsz, tc, d_mqk), lambda c: (0, xchunk(c), off_q // d_mqk + 1)),
                  pl.BlockSpec((bsz, tc, d_mv), lambda c: (0, xchunk(c), off_v // d_mv)),
                  pl.BlockSpec((bsz, tc, d_mv), lambda c: (0, xchunk(c), off_v // d_mv + 1)),
                  pl.BlockSpec((bsz, tc, LANES), lambda c: (0, xchunk(c), 0)),
                  pl.BlockSpec((bsz, 2 * n_heads, tc), lambda c: (0, 0, xchunk(c))),
                  pl.BlockSpec((tc, d_mqk), const), pl.BlockSpec((tc, d_mqk), const),
                  pl.BlockSpec((tc, d_mv), const), pl.BlockSpec((tc, d_mv), const),
                  pl.BlockSpec((tc, LANES), const), pl.BlockSpec((2 * n_heads, tc), const),
                  pl.BlockSpec((1, LANES), const), pl.BlockSpec((2 * n_heads, tc), const),
                  pl.BlockSpec((1, d_mv), const)],
        out_specs=pl.BlockSpec((bsz, tc, d_mv), lambda c: (0, xchunk(c), 0)),
        out_shape=jax.ShapeDtypeStruct((bsz, seq, d_mv), bf),
        scratch_shapes=[pltpu.VMEM((bsz, n_heads, head_qk, head_v + LANES), jnp.float32),
                        pltpu.VMEM((bsz, n_heads, LANES), jnp.float32)],
        compiler_params=pltpu.CompilerParams(
            dimension_semantics=("arbitrary",), vmem_limit_bytes=VMEM_LIMIT_BYTES),
        name="mlstm",
    )(proj3, proj3, proj3, proj3, gif3, gt3, qm, km, vm, om, gm, gt_m,
      bias, biast, norm_w[None, :])
    return y.reshape(bsz * seq, d_mv)


def _mix_kernel(x_ref, cb_ref, cc_ref, cx_ref, cch_ref, cxh_ref, ccm_ref, cxm_ref,
                yb_ref, ga_ref, gb_ref, pa_ref, pb_ref, wo_ref, cw_ref,
                g0_ref, b0_ref, g1_ref, b1_ref, h1t_ref,
                *, tiles_per_batch, alpha):
    i = pl.program_id(0)
    f32, bf = jnp.float32, jnp.bfloat16
    tm = x_ref.shape[0]
    z = cc_ref[...].astype(f32) * cx_ref[...].astype(f32)
    batch_start = (i % tiles_per_batch) == 0
    halo = jnp.where(batch_start,
                     ccm_ref[...].astype(f32) * cxm_ref[...].astype(f32),
                     cch_ref[...].astype(f32) * cxh_ref[...].astype(f32))
    sub = lax.broadcasted_iota(jnp.int32, halo.shape, 0)
    n_halo = halo.shape[0]
    y = cw_ref[2:3, :] * z
    for shift in (1, 2):
        rolled = pltpu.roll(z, shift, 0)
        head = jnp.where(sub < shift, pltpu.roll(halo, shift, 0), rolled[:n_halo])
        y = y + cw_ref[2 - shift:3 - shift, :] * jnp.concatenate([head, rolled[n_halo:]], axis=0)
    y_a = (cb_ref[...].astype(f32) * y).astype(bf)
    za = jnp.dot(y_a, pa_ref[...], preferred_element_type=f32)
    zb = jnp.dot(yb_ref[...], pb_ref[...], preferred_element_type=f32)
    merged = (jax.nn.sigmoid(ga_ref[...].astype(f32)) * za
              + jax.nn.sigmoid(gb_ref[...].astype(f32)) * zb).astype(bf)
    mix = jnp.dot(merged, wo_ref[...], preferred_element_type=f32)
    h0 = _ln_rows(x_ref[...], g0_ref[...], b0_ref[...])
    h1t_ref[...] = _ln_rows(alpha * h0 + mix, g1_ref[...], b1_ref[...]).T


def _mix(x2, proj_x, proj_m, y_b, p_a, p_b, w_o, conv_w, ln0_g, ln0_b, ln1_g, ln1_b,
         *, seq, off_g, alpha, tm):
    t_x, d = x2.shape
    d_conv = conv_w.shape[-1]
    d_mv = y_b.shape[-1]
    n_meta = proj_m.shape[0]
    halo = SUBLANES
    assert seq % tm == 0 and tm % halo == 0 and n_meta % halo == 0 and off_g % d == 0
    assert conv_w.shape[0] == 3
    tile = lambda j: pl.BlockSpec((tm, d_conv), lambda i: (i, j))
    prev = lambda j: pl.BlockSpec((halo, d_conv),
                                  lambda i: (jnp.maximum(i * (tm // halo) - 1, 0), j))
    meta = lambda j: pl.BlockSpec((halo, d_conv), lambda i: (n_meta // halo - 1, j))
    const = lambda shape: pl.BlockSpec(shape, lambda i: (0, 0), pipeline_mode=pl.Buffered(1))
    vec = lambda a: a[None, :]
    return pl.pallas_call(
        functools.partial(_mix_kernel, tiles_per_batch=seq // tm, alpha=alpha),
        grid=(t_x // tm,),
        in_specs=[pl.BlockSpec((tm, d), lambda i: (i, 0)),
                  tile(0), tile(1), tile(2), prev(1), prev(2), meta(1), meta(2),
                  pl.BlockSpec((tm, d_mv), lambda i: (i, 0)),
                  pl.BlockSpec((tm, d), lambda i: (i, off_g // d)),
                  pl.BlockSpec((tm, d), lambda i: (i, off_g // d + 1)),
                  const(p_a.shape), const(p_b.shape), const(w_o.shape), const(conv_w.shape),
                  const((1, d)), const((1, d)), const((1, d)), const((1, d))],
        out_specs=pl.BlockSpec((d, tm), lambda i: (0, i)),
        out_shape=jax.ShapeDtypeStruct((d, t_x), jnp.float32),
        compiler_params=pltpu.CompilerParams(
            dimension_semantics=("parallel",), vmem_limit_bytes=VMEM_LIMIT_BYTES),
        name="mix",
    )(x2, proj_x, proj_x, proj_x, proj_x, proj_x, proj_m, proj_m, y_b, proj_x, proj_x,
      p_a, p_b, w_o, conv_w, vec(ln0_g), vec(ln0_b), vec(ln1_g), vec(ln1_b))


def kernel(x, meta_tokens, ln0_g, ln0_b, w_in, b_if, conv_w, mh_norm_w, p_a, p_b, w_o,
           ln1_g, ln1_b, peer_wq, peer_keys, peer_u, peer_v, ln2_g, ln2_b):
    depth = w_in.shape[0]
    assert depth == 1
    bsz, seq, d = x.shape
    n_meta = meta_tokens.shape[0]
    n_in = w_in.shape[-1]
    d_conv = conv_w.shape[-1]
    d_mv = mh_norm_w.shape[-1]
    n_mh = b_if.shape[-1] // 2
    d_mqk = (n_in - 3 * d_conv - 2 * d_mv - 2 * n_mh - 2 * d) // 2
    head_qk = d_mqk // n_mh
    head_v = d_mv // n_mh
    alpha = (2 * depth) ** 0.25
    bf = jnp.bfloat16
    t_x = bsz * seq

    x2 = x.reshape(t_x, d)
    gate_off = 3 * d_conv + 2 * d_mqk + 2 * d_mv
    w = w_in[0]
    w_bf = w.astype(bf)
    w_main = jnp.concatenate([w_bf[:, :gate_off], w_bf[:, gate_off + 2 * n_mh:]], axis=1)
    w_gif = jnp.pad(w_bf[:, gate_off:gate_off + 2 * n_mh], ((0, 0), (0, LANES - 2 * n_mh)))
    inproj_tn = min(INPROJ_TILE_N, w_main.shape[1])
    proj_x, gif_x, gt_x = _inproj(x2, ln0_g, ln0_b, w_main, w_gif, INPROJ_TILE_M, inproj_tn)
    proj_m, gif_m, _ = _inproj(meta_tokens, ln0_g, ln0_b, w_main, w_gif, INPROJ_TILE_M,
                               inproj_tn)

    off_q = 3 * d_conv
    off_v = off_q + 2 * d_mqk
    off_g = off_v + 2 * d_mv
    y_b = _mlstm(proj_x, proj_m, gif_x, gt_x, gif_m, b_if[0], mh_norm_w[0], bsz=bsz, seq=seq,
                 n_heads=n_mh, head_qk=head_qk, head_v=head_v, off_q=off_q, off_v=off_v)
    h1_t = _mix(x2, proj_x, proj_m, y_b,
                p_a[0].astype(bf), p_b[0].astype(bf), w_o[0].astype(bf), conv_w[0],
                ln0_g, ln0_b, ln1_g[0], ln1_b[0], seq=seq, off_g=off_g, alpha=alpha,
                tm=MIX_TILE)

    n_ph, _, n_keys, half_key = peer_keys.shape[1:]
    w_fold = _fold_keys(peer_keys[0].reshape(n_ph * 2, n_keys, half_key), peer_wq[0])
    scores_t = _matmul(w_fold, h1_t, jnp.float32, 1024, 512)
    e0, thr, e1, rank = _peer_select(scores_t, n_ph, n_keys, LANES)
    out = _peer_dense(h1_t, e0, thr, e1, rank, peer_u[0], peer_v[0],
                      ln2_g[0], ln2_b[0], alpha=alpha, tm=PEER_TOKEN_TILE,
                      rows_per_step=PEER_ROWS_PER_STEP)
    return out.reshape(bsz, seq, d)
```

```python
import functools
import math

import jax
import jax.numpy as jnp
from jax import lax
from jax.experimental import pallas as pl
from jax.experimental.pallas import tpu as pltpu

LANES = 128
SUBLANES = 8
BF16_SUBLANES = 16
VMEM_LIMIT_BYTES = 56 * 1024 * 1024
LARGE_VMEM_LIMIT_BYTES = 58 * 1024 * 1024

TOPK = 16
LN_EPS = 1e-5
I_GATE_PAD = -1e30
MLSTM_CHUNK = 256
MIX_TILE = 256
INPROJ_TILE_M = 1024
INPROJ_TILE_N = 1024
PEER_TOKEN_TILE = 512
PEER_ROWS_PER_STEP = 8
QUANT_ROWS = 512
FP8 = jnp.float8_e4m3fn
FP8_CLIP = 448.0
FP8_TARGET = 256.0
PEER_P_SCALE = 64.0


def _ln_rows(x, g, b):
    mu = jnp.mean(x, axis=1, keepdims=True)
    xc = x - mu
    var = jnp.mean(xc * xc, axis=1, keepdims=True)
    return xc * lax.rsqrt(var + LN_EPS) * g + b


def _inproj_kernel(x_ref, g_ref, b_ref, w_ref, wg_ref, o_ref, og_ref, ogt_ref, h_ref):
    @pl.when(pl.program_id(1) == 0)
    def _():
        h = _ln_rows(x_ref[...], g_ref[...], b_ref[...]).astype(h_ref.dtype)
        h_ref[...] = h
        gates = jnp.dot(h, wg_ref[...], preferred_element_type=jnp.float32)
        og_ref[...] = gates
        ogt_ref[...] = gates.T

    o_ref[...] = jnp.dot(h_ref[...], w_ref[...],
                         preferred_element_type=jnp.float32).astype(o_ref.dtype)


def _inproj(x2, ln_g, ln_b, w_main, w_gate, tm, tn):
    m, d = x2.shape
    n = w_main.shape[1]
    tm = min(tm, m)
    assert m % tm == 0 and n % tn == 0 and w_gate.shape[1] == LANES
    const = lambda shape: pl.BlockSpec(shape, lambda i, j: (0, 0))
    return pl.pallas_call(
        _inproj_kernel,
        grid=(m // tm, n // tn),
        in_specs=[pl.BlockSpec((tm, d), lambda i, j: (i, 0)),
                  const((1, d)), const((1, d)),
                  pl.BlockSpec((d, tn), lambda i, j: (0, j)),
                  const((d, LANES))],
        out_specs=[pl.BlockSpec((tm, tn), lambda i, j: (i, j)),
                   pl.BlockSpec((tm, LANES), lambda i, j: (i, 0)),
                   pl.BlockSpec((LANES, tm), lambda i, j: (0, i))],
        out_shape=[jax.ShapeDtypeStruct((m, n), jnp.bfloat16),
                   jax.ShapeDtypeStruct((m, LANES), jnp.float32),
                   jax.ShapeDtypeStruct((LANES, m), jnp.float32)],
        scratch_shapes=[pltpu.VMEM((tm, d), jnp.bfloat16)],
        compiler_params=pltpu.CompilerParams(
            dimension_semantics=("parallel", "arbitrary"),
            vmem_limit_bytes=VMEM_LIMIT_BYTES),
        name="inproj",
    )(x2, ln_g[None, :], ln_b[None, :], w_main, w_gate)


def _mm_kernel(a_ref, b_ref, o_ref):
    bf = jnp.bfloat16
    o_ref[...] = jnp.dot(a_ref[...].astype(bf), b_ref[...].astype(bf),
                         preferred_element_type=jnp.float32).astype(o_ref.dtype)


def _matmul(a, b, out_dtype, tm, tn):
    m, k = a.shape
    _, n = b.shape
    tm = min(tm, m)
    tn = min(tn, n)
    assert m % tm == 0 and n % tn == 0
    return pl.pallas_call(
        _mm_kernel,
        grid=(m // tm, n // tn),
        in_specs=[pl.BlockSpec((tm, k), lambda i, j: (i, 0)),
                  pl.BlockSpec((k, tn), lambda i, j: (0, j))],
        out_specs=pl.BlockSpec((tm, tn), lambda i, j: (i, j)),
        out_shape=jax.ShapeDtypeStruct((m, n), out_dtype),
        compiler_params=pltpu.CompilerParams(
            dimension_semantics=("parallel", "parallel"),
            vmem_limit_bytes=VMEM_LIMIT_BYTES),
        name="matmul",
    )(a, b)


def _pow2_scale(amax):
    safe = jnp.where(amax > 0, amax, FP8_TARGET)
    return jnp.exp2(jnp.floor(jnp.log2(FP8_TARGET / safe)))


def _quantize_rows_kernel(x_ref, q_ref, inv_ref, *, transpose):
    x = x_ref[...]
    scale = _pow2_scale(jnp.max(jnp.abs(x), axis=1, keepdims=True))
    xs = x * scale
    q_ref[...] = (xs.T if transpose else xs).astype(q_ref.dtype)
    inv_ref[...] = jnp.broadcast_to(1.0 / scale, inv_ref.shape)


def _quantize_rows(x, rows, transpose):
    n, d = x.shape
    rows = min(rows, n)
    assert n % rows == 0
    q_shape, q_spec = (((d, n), pl.BlockSpec((d, rows), lambda i: (0, i))) if transpose
                       else ((n, d), pl.BlockSpec((rows, d), lambda i: (i, 0))))
    return pl.pallas_call(
        functools.partial(_quantize_rows_kernel, transpose=transpose),
        grid=(n // rows,),
        in_specs=[pl.BlockSpec((rows, d), lambda i: (i, 0))],
        out_specs=[q_spec, pl.BlockSpec((rows, LANES), lambda i: (i, 0))],
        out_shape=[jax.ShapeDtypeStruct(q_shape, FP8),
                   jax.ShapeDtypeStruct((n, LANES), jnp.float32)],
        compiler_params=pltpu.CompilerParams(
            dimension_semantics=("parallel",), vmem_limit_bytes=VMEM_LIMIT_BYTES),
        name="quantize_rows",
    )(x)


def _peer_dense_kernel(ht_ref, e0_ref, thr_ref, e1_ref, rank_ref, u_ref, vt_ref, c1_ref, c2_ref,
                       post_ref, g_ref, b_ref, o_ref, hb_ref, sh_ref, acc_ref, w_ref, p_ref,
                       *, n_heads, rows_per_step, alpha):
    k = pl.program_id(1)
    bf, f8 = jnp.bfloat16, FP8
    tm = ht_ref.shape[1]
    zero = jnp.zeros((), bf)

    def row_tile(ref, h, ii):
        row = jnp.broadcast_to(ref[h, ii:ii + 1, :], (BF16_SUBLANES, tm)).astype(bf)
        return pltpu.repeat(row, LANES // BF16_SUBLANES, axis=0)

    @pl.when(k == 0)
    def _():
        acc_ref[...] = jnp.zeros_like(acc_ref)
        h = ht_ref[...]
        amax = jnp.max(jnp.max(jnp.abs(h), axis=0, keepdims=True), axis=1, keepdims=True)
        s_h = _pow2_scale(amax)
        hb_ref[...] = (h * s_h).astype(f8)
        sh_ref[...] = jnp.broadcast_to(1.0 / s_h, sh_ref.shape)

    for ii in range(rows_per_step):
        w = jnp.zeros((LANES, tm), bf)
        for h in range(n_heads):
            sel = rank_ref[h] < row_tile(thr_ref, h, ii)
            w = w + jnp.where(sel, e1_ref[h], zero) * row_tile(e0_ref, h, ii)
        w_ref[ii * LANES:(ii + 1) * LANES, :] = w
    a = jnp.dot(u_ref[...], hb_ref[...], preferred_element_type=jnp.float32)
    inv_sh = sh_ref[0:1, 0:1]
    c1 = pltpu.repeat((c1_ref[...] * inv_sh).astype(bf), tm // LANES, axis=1)
    c2 = pltpu.repeat((c2_ref[...] * inv_sh).astype(bf), tm // LANES, axis=1)
    ab = a.astype(bf)
    act = (ab * c1) * (1.0 + lax.erf(ab * c2))
    p = jnp.clip(w_ref[...] * act, -FP8_CLIP, FP8_CLIP)
    p_ref[...] = p.astype(f8)
    acc_ref[...] += jnp.dot(vt_ref[...], p_ref[...], preferred_element_type=jnp.float32)

    @pl.when(k == pl.num_programs(1) - 1)
    def _():
        y = alpha * ht_ref[...] + acc_ref[...] * post_ref[0:1, 0:1]
        mu = jnp.mean(y, axis=0, keepdims=True)
        yc = y - mu
        var = jnp.mean(yc * yc, axis=0, keepdims=True)
        o_ref[...] = (yc * lax.rsqrt(var + LN_EPS)).T * g_ref[...] + b_ref[...]


def _peer_dense(ht, e0, thr, e1, rank, peer_u, peer_v, ln_g, ln_b, *, alpha, tm, rows_per_step):
    u, inv_u = _quantize_rows(peer_u, QUANT_ROWS, transpose=False)
    vt, inv_v = _quantize_rows(peer_v, QUANT_ROWS, transpose=True)
    inv_v_max = jnp.max(inv_v)
    c1 = inv_u * (inv_v / inv_v_max) * (0.5 * PEER_P_SCALE)
    c2 = inv_u * (2.0 ** -0.5)
    post = jnp.broadcast_to(inv_v_max / PEER_P_SCALE, (1, LANES)).astype(jnp.float32)
    d, t = ht.shape
    n_heads, n_keys, _ = e0.shape
    assert n_keys == LANES
    n_experts = u.shape[0]
    eb = rows_per_step * LANES
    sel_spec = pl.BlockSpec((n_heads, n_keys, tm), lambda i, k: (0, 0, i))
    row_spec = pl.BlockSpec((n_heads, rows_per_step, tm), lambda i, k: (0, k, i))
    vec_spec = pl.BlockSpec((1, d), lambda i, k: (0, 0))
    return pl.pallas_call(
        functools.partial(_peer_dense_kernel, n_heads=n_heads, rows_per_step=rows_per_step,
                          alpha=alpha),
        grid=(t // tm, n_experts // eb),
        in_specs=[pl.BlockSpec((d, tm), lambda i, k: (0, i)),
                  row_spec, row_spec, sel_spec, sel_spec,
                  pl.BlockSpec((eb, d), lambda i, k: (k, 0)),
                  pl.BlockSpec((d, eb), lambda i, k: (0, k)),
                  pl.BlockSpec((eb, LANES), lambda i, k: (k, 0)),
                  pl.BlockSpec((eb, LANES), lambda i, k: (k, 0)),
                  pl.BlockSpec((1, LANES), lambda i, k: (0, 0)),
                  vec_spec, vec_spec],
        out_specs=pl.BlockSpec((tm, d), lambda i, k: (i, 0)),
        out_shape=jax.ShapeDtypeStruct((t, d), jnp.float32),
        scratch_shapes=[pltpu.VMEM((d, tm), FP8),
                        pltpu.VMEM((SUBLANES, LANES), jnp.float32),
                        pltpu.VMEM((d, tm), jnp.float32),
                        pltpu.VMEM((eb, tm), jnp.bfloat16),
                        pltpu.VMEM((eb, tm), FP8)],
        compiler_params=pltpu.CompilerParams(
            dimension_semantics=("parallel", "arbitrary"),
            vmem_limit_bytes=LARGE_VMEM_LIMIT_BYTES),
        name="peer_dense",
    )(ht, e0, thr, e1, rank, u, vt, c1, c2, post, ln_g[None, :], ln_b[None, :])


def _fold_keys_kernel(keys_ref, wq_ref, o_ref):
    o_ref[...] = lax.dot_general(
        keys_ref[0], wq_ref[...], (((1,), (1,)), ((), ())),
        precision=lax.Precision.HIGHEST, preferred_element_type=jnp.float32)


def _fold_keys(keys, wq):
    n_blocks, n_keys, half_key = keys.shape
    d = wq.shape[0]
    return pl.pallas_call(
        _fold_keys_kernel,
        grid=(n_blocks,),
        in_specs=[pl.BlockSpec((1, n_keys, half_key), lambda b: (b, 0, 0)),
                  pl.BlockSpec((d, half_key), lambda b: (0, b))],
        out_specs=pl.BlockSpec((n_keys, d), lambda b: (b, 0)),
        out_shape=jax.ShapeDtypeStruct((n_blocks * n_keys, d), jnp.float32),
        compiler_params=pltpu.CompilerParams(dimension_semantics=("parallel",)),
        name="fold_keys",
    )(keys, wq)


def _batcher_pairs(n):
    pairs = []
    p = 1
    while p < n:
        k = p
        while k >= 1:
            for j in range(k % p, n - k, 2 * k):
                for i in range(min(k, n - j - k)):
                    if (i + j) // (2 * p) == (i + j + k) // (2 * p):
                        pairs.append((i + j, i + j + k))
            k //= 2
        p *= 2
    return pairs


_SORT_PAIRS = _batcher_pairs(TOPK)
_MERGE_PAIRS = [(i, i + d) for d in (8, 4, 2, 1) for i in range(TOPK) if not i & d]
_SUBLANE_SHIFTS = (4, 2, 1)
UNRANKED = 100.0


def _compare_exchange(v, pairs):
    v = list(v)
    for i, j in pairs:
        v[i], v[j] = jnp.maximum(v[i], v[j]), jnp.minimum(v[i], v[j])
    return v


def _top16(tiles):
    v = _compare_exchange(tiles, _SORT_PAIRS)
    for shift in _SUBLANE_SHIFTS:
        other = [pltpu.roll(x, shift, 0) for x in v]
        v = [jnp.maximum(v[r], other[TOPK - 1 - r]) for r in range(TOPK)]
        v = _compare_exchange(v, _MERGE_PAIRS)
    return v


def _stack_rows(tiles, sub):
    out = tiles[0]
    for r in range(1, SUBLANES):
        out = jnp.where(sub == r, tiles[r], out)
    return out


def _peer_select_kernel(sc_ref, e0_ref, thr_ref, e1_ref, rank_ref, *, n_heads, n_keys):
    tl = sc_ref.shape[1]
    reps = n_keys // SUBLANES
    half = TOPK // 2
    sub = lax.broadcasted_iota(jnp.int32, (SUBLANES, tl), 0)
    spread = lambda tile: pltpu.repeat(tile, reps, axis=0)

    def head_body(h, carry):
        r0 = pl.multiple_of(h * 2 * n_keys, 2 * n_keys)
        s0 = sc_ref[pl.ds(r0, n_keys), :]
        s1 = sc_ref[pl.ds(r0 + n_keys, n_keys), :]
        a = _top16([s0[SUBLANES * r:SUBLANES * (r + 1), :] for r in range(reps)])
        b = _top16([s1[SUBLANES * r:SUBLANES * (r + 1), :] for r in range(reps)])
        b_lo, b_hi = _stack_rows(b[:half], sub), _stack_rows(b[half:], sub)
        a_hi = _stack_rows(a[half:], sub)
        cand = ([a[0] + b_lo, a[0] + b_hi] + [a[p] + b_lo for p in range(1, half)]
                + [a_hi + b[0]])
        cand += [jnp.full((SUBLANES, tl), -jnp.inf, jnp.float32)] * (TOPK - len(cand))
        best = _top16(cand)
        tau = best[TOPK - 1]
        zsum = sum(jnp.exp(c - best[0]) for c in best)
        thr = jnp.zeros((n_keys, tl), jnp.float32)
        rank1 = jnp.full((n_keys, tl), UNRANKED, jnp.float32)
        for p in reversed(range(TOPK)):
            n_sel = (jnp.where(a[p] + b_lo >= tau, 1.0, 0.0)
                     + jnp.where(a[p] + b_hi >= tau, 1.0, 0.0))
            for shift in _SUBLANE_SHIFTS:
                n_sel = n_sel + pltpu.roll(n_sel, shift, 0)
            thr = jnp.where(s0 == spread(a[p]), spread(n_sel), thr)
            rank1 = jnp.where(s1 == spread(b[p]), float(p), rank1)
        e0_ref[h] = jnp.exp(s0 - spread(a[0])) / spread(zsum)
        thr_ref[h] = thr
        e1_ref[h] = jnp.exp(s1 - spread(b[0])).astype(e1_ref.dtype)
        rank_ref[h] = rank1.astype(rank_ref.dtype)
        return carry

    lax.fori_loop(0, n_heads, head_body, 0)


def _peer_select(scores_t, n_heads, n_keys, tl):
    rows, t = scores_t.shape
    assert rows == n_heads * 2 * n_keys and n_keys == SUBLANES * TOPK and TOPK == 2 * SUBLANES
    out_spec = pl.BlockSpec((n_heads, n_keys, tl), lambda i: (0, 0, i))
    f32 = jax.ShapeDtypeStruct((n_heads, n_keys, t), jnp.float32)
    b16 = jax.ShapeDtypeStruct((n_heads, n_keys, t), jnp.bfloat16)
    return pl.pallas_call(
        functools.partial(_peer_select_kernel, n_heads=n_heads, n_keys=n_keys),
        grid=(t // tl,),
        in_specs=[pl.BlockSpec((rows, tl), lambda i: (0, i))],
        out_specs=[out_spec, out_spec, out_spec, out_spec],
        out_shape=[f32, f32, b16, b16],
        compiler_params=pltpu.CompilerParams(
            dimension_semantics=("parallel",), vmem_limit_bytes=VMEM_LIMIT_BYTES),
        name="peer_select",
    )(scores_t)


def _log_sigmoid(x):
    return jnp.minimum(x, 0.0) - jnp.log1p(jnp.exp(-jnp.abs(x)))


def _mlstm_kernel(q_ref, k_ref, v_ref, o_ref, g_ref, gt_ref,
                  qm_ref, km_ref, vm_ref, om_ref, gm_ref, gmt_ref,
                  bias_ref, biast_ref, nw_ref, y_ref, c_ref, m_ref,
                  *, n_heads, head_qk, head_v):
    c = pl.program_id(0)
    n_batch, tc = q_ref.shape[0], q_ref.shape[1]
    f32, bf = jnp.float32, jnp.bfloat16
    first = c == 0

    @pl.when(first)
    def _():
        c_ref[...] = jnp.zeros_like(c_ref)
        m_ref[...] = jnp.zeros_like(m_ref)

    row = lax.broadcasted_iota(jnp.int32, (tc, tc), 0)
    col = lax.broadcasted_iota(jnp.int32, (tc, tc), 1)
    causal = col <= row
    tri = jnp.where(causal, 1.0, 0.0).astype(f32)
    ln_scale = -0.5 * math.log(head_qk)
    ones_blk = jnp.where(lax.broadcasted_iota(jnp.int32, (tc, LANES), 1) == 0, 1.0, 0.0).astype(bf)

    for b in range(n_batch):
        _mlstm_chunk(b, first, causal, tri, ln_scale, ones_blk,
                     q_ref, k_ref, v_ref, o_ref, g_ref, gt_ref,
                     qm_ref, km_ref, vm_ref, om_ref, gm_ref, gmt_ref,
                     bias_ref, biast_ref, nw_ref, y_ref, c_ref, m_ref,
                     n_heads=n_heads, head_qk=head_qk, head_v=head_v)


def _mlstm_chunk(b, first, causal, tri, ln_scale, ones_blk,
                 q_ref, k_ref, v_ref, o_ref, g_ref, gt_ref,
                 qm_ref, km_ref, vm_ref, om_ref, gm_ref, gmt_ref,
                 bias_ref, biast_ref, nw_ref, y_ref, c_ref, m_ref,
                 *, n_heads, head_qk, head_v):
    tc = q_ref.shape[1]
    f32, bf = jnp.float32, jnp.bfloat16
    q = jnp.where(first, qm_ref[...], q_ref[b])
    k = jnp.where(first, km_ref[...], k_ref[b])
    v = jnp.where(first, vm_ref[...], v_ref[b])
    o = jnp.where(first, om_ref[...], o_ref[b])
    g = jnp.where(first, gm_ref[...], g_ref[b]) + bias_ref[...]
    gt = jnp.where(first, gmt_ref[...], gt_ref[b]) + biast_ref[...]
    b_cols = jnp.dot(tri, _log_sigmoid(g), precision=lax.Precision.HIGHEST,
                     preferred_element_type=f32)
    b_rows = lax.dot_general(_log_sigmoid(gt), tri, (((1,), (1,)), ((), ())),
                             precision=lax.Precision.HIGHEST, preferred_element_type=f32)

    for h in range(n_heads):
        qh = q[:, h * head_qk:(h + 1) * head_qk]
        kh = k[:, h * head_qk:(h + 1) * head_qk]
        v_ext = jnp.concatenate([v[:, h * head_v:(h + 1) * head_v], ones_blk], axis=1)
        li_col = g[:, h:h + 1]
        b_col = b_cols[:, n_heads + h:n_heads + h + 1]
        li_row = gt[h:h + 1, :]
        b_row = b_rows[n_heads + h:n_heads + h + 1, :]
        m_prev = m_ref[b, h:h + 1, 0:1]
        d = jnp.where(causal, b_col + (li_row - b_row), -jnp.inf)
        inter = b_col + m_prev
        m_t = jnp.maximum(inter, jnp.max(d, axis=1, keepdims=True))
        w = jnp.exp(d - (m_t - ln_scale))
        s = lax.dot_general(qh, kh, (((1,), (1,)), ((), ())), preferred_element_type=f32) * w
        sc_inter = jnp.exp(inter - (m_t - ln_scale))
        c_ext = c_ref[b, h]
        num_ext = (jnp.dot(s.astype(bf), v_ext, preferred_element_type=f32)
                   + sc_inter * jnp.dot(qh, c_ext.astype(bf), preferred_element_type=f32))
        den = num_ext[:, head_v:head_v + 1]
        hm = num_ext[:, :head_v] / jnp.maximum(jnp.abs(den), jnp.exp(-m_t))
        hm = hm * lax.rsqrt(jnp.mean(hm * hm, axis=1, keepdims=True) + LN_EPS)
        gate_o = jax.nn.sigmoid(o[:, h * head_v:(h + 1) * head_v].astype(f32))
        y_ref[b, :, h * head_v:(h + 1) * head_v] = (
            gate_o * (hm * nw_ref[:, h * head_v:(h + 1) * head_v])).astype(y_ref.dtype)
        b_last = b_col[tc - 1:tc, :]
        dl = b_last - b_col + li_col
        m_new = jnp.maximum(b_last + m_prev, jnp.max(dl, axis=0, keepdims=True))
        wv = (jnp.exp(dl - m_new) * v_ext.astype(f32)).astype(bf)
        c_ref[b, h] = (jnp.exp(b_last + m_prev - m_new) * c_ext
                       + lax.dot_general(kh, wv, (((0,), (0,)), ((), ())),
                                         preferred_element_type=f32))
        m_ref[b, h:h + 1, :] = jnp.broadcast_to(m_new, (1, LANES))


def _mlstm(proj_x, proj_m, gif_x, gt_x, gif_m, gt_meta, b_if, norm_w, *, bsz, seq, n_heads,
           head_qk, head_v, off_q, off_v):
    tc = min(MLSTM_CHUNK, seq)
    assert seq % tc == 0
    n_meta = proj_m.shape[0]
    d_mqk, d_mv = n_heads * head_qk, n_heads * head_v
    assert off_q % d_mqk == 0 and off_v % d_mv == 0 and n_meta <= tc
    nc_x = seq // tc
    bf = jnp.bfloat16
    lead = tc - n_meta
    pad_rows = lambda a: jnp.pad(a, ((lead, 0), (0, 0)))
    qm = pad_rows(proj_m[:, off_q:off_q + d_mqk])
    km = pad_rows(proj_m[:, off_q + d_mqk:off_q + 2 * d_mqk])
    vm = pad_rows(proj_m[:, off_v:off_v + d_mv])
    om = pad_rows(proj_m[:, off_v + d_mv:off_v + 2 * d_mv])
    pad_gate = jnp.concatenate([jnp.full((lead, n_heads), I_GATE_PAD, jnp.float32),
                                jnp.full((lead, n_heads), -I_GATE_PAD, jnp.float32),
                                jnp.zeros((lead, LANES - 2 * n_heads), jnp.float32)], axis=1)
    gm = jnp.concatenate([pad_gate, gif_m], axis=0)
    bias = jnp.pad(b_if, (0, LANES - 2 * n_heads))[None, :]
    biast = jnp.broadcast_to(b_if[:, None], (2 * n_heads, tc))
    pad_gate_t = jnp.concatenate([jnp.full((n_heads, lead), I_GATE_PAD, jnp.float32),
                                  jnp.full((n_heads, lead), -I_GATE_PAD, jnp.float32)], axis=0)
    gt_m = jnp.concatenate([pad_gate_t, gt_meta[:2 * n_heads]], axis=1)

    proj3 = proj_x.reshape(bsz, seq, proj_x.shape[1])
    gif3 = gif_x.reshape(bsz, seq, LANES)
    gt3 = gt_x.reshape(LANES, bsz, seq).transpose(1, 0, 2)
    xchunk = lambda c: jnp.maximum(c - 1, 0)
    const = lambda c: (0, 0)
    y = pl.pallas_call(
        functools.partial(_mlstm_kernel, n_heads=n_heads, head_qk=head_qk, head_v=head_v),
        grid=(nc_x + 1,),
        in_specs=[pl.BlockSpec((bsz, tc, d_mqk), lambda c: (0, xchunk(c), off_q // d_mqk)),
                  pl.BlockSpec((bsz, tc, d_mqk), lambda c: (0, xchunk(c), off_q // d_mqk + 1)),
                  pl.BlockSpec((bsz, tc, d_mv), lambda c: (0, xchunk(c), off_v // d_mv)),
                  pl.BlockSpec((bsz, tc, d_mv), lambda c: (0, xchunk(c), off_v // d_mv + 1)),
                  pl.BlockSpec((bsz, tc, LANES), lambda c: (0, xchunk(c), 0)),
                  pl.BlockSpec((bsz, 2 * n_heads, tc), lambda c: (0, 0, xchunk(c))),
                  pl.BlockSpec((tc, d_mqk), const), pl.BlockSpec((tc, d_mqk), const),
                  pl.BlockSpec((tc, d_mv), const), pl.BlockSpec((tc, d_mv), const),
                  pl.BlockSpec((tc, LANES), const), pl.BlockSpec((2 * n_heads, tc), const),
                  pl.BlockSpec((1, LANES), const), pl.BlockSpec((2 * n_heads, tc), const),
                  pl.BlockSpec((1, d_mv), const)],
        out_specs=pl.BlockSpec((bsz, tc, d_mv), lambda c: (0, xchunk(c), 0)),
        out_shape=jax.ShapeDtypeStruct((bsz, seq, d_mv), bf),
        scratch_shapes=[pltpu.VMEM((bsz, n_heads, head_qk, head_v + LANES), jnp.float32),
                        pltpu.VMEM((bsz, n_heads, LANES), jnp.float32)],
        compiler_params=pltpu.CompilerParams(
            dimension_semantics=("arbitrary",), vmem_limit_bytes=VMEM_LIMIT_BYTES),
        name="mlstm",
    )(proj3, proj3, proj3, proj3, gif3, gt3, qm, km, vm, om, gm, gt_m,
      bias, biast, norm_w[None, :])
    return y.reshape(bsz * seq, d_mv)


def _mix_kernel(x_ref, cb_ref, cc_ref, cx_ref, cch_ref, cxh_ref, ccm_ref, cxm_ref,
                yb_ref, ga_ref, gb_ref, pa_ref, pb_ref, wo_ref, cw_ref,
                g0_ref, b0_ref, g1_ref, b1_ref, h1t_ref,
                *, tiles_per_batch, alpha):
    i = pl.program_id(0)
    f32, bf = jnp.float32, jnp.bfloat16
    tm = x_ref.shape[0]
    z = cc_ref[...].astype(f32) * cx_ref[...].astype(f32)
    batch_start = (i % tiles_per_batch) == 0
    halo = jnp.where(batch_start,
                     ccm_ref[...].astype(f32) * cxm_ref[...].astype(f32),
                     cch_ref[...].astype(f32) * cxh_ref[...].astype(f32))
    sub = lax.broadcasted_iota(jnp.int32, halo.shape, 0)
    n_halo = halo.shape[0]
    y = cw_ref[2:3, :] * z
    for shift in (1, 2):
        rolled = pltpu.roll(z, shift, 0)
        head = jnp.where(sub < shift, pltpu.roll(halo, shift, 0), rolled[:n_halo])
        y = y + cw_ref[2 - shift:3 - shift, :] * jnp.concatenate([head, rolled[n_halo:]], axis=0)
    y_a = (cb_ref[...].astype(f32) * y).astype(bf)
    za = jnp.dot(y_a, pa_ref[...], preferred_element_type=f32)
    zb = jnp.dot(yb_ref[...], pb_ref[...], preferred_element_type=f32)
    merged = (jax.nn.sigmoid(ga_ref[...].astype(f32)) * za
              + jax.nn.sigmoid(gb_ref[...].astype(f32)) * zb).astype(bf)
    mix = jnp.dot(merged, wo_ref[...], preferred_element_type=f32)
    h0 = _ln_rows(x_ref[...], g0_ref[...], b0_ref[...])
    h1t_ref[...] = _ln_rows(alpha * h0 + mix, g1_ref[...], b1_ref[...]).T


def _mix(x2, proj_x, proj_m, y_b, p_a, p_b, w_o, conv_w, ln0_g, ln0_b, ln1_g, ln1_b,
         *, seq, off_g, alpha, tm):
    t_x, d = x2.shape
    d_conv = conv_w.shape[-1]
    d_mv = y_b.shape[-1]
    n_meta = proj_m.shape[0]
    halo = SUBLANES
    assert seq % tm == 0 and tm % halo == 0 and n_meta % halo == 0 and off_g % d == 0
    assert conv_w.shape[0] == 3
    tile = lambda j: pl.BlockSpec((tm, d_conv), lambda i: (i, j))
    prev = lambda j: pl.BlockSpec((halo, d_conv),
                                  lambda i: (jnp.maximum(i * (tm // halo) - 1, 0), j))
    meta = lambda j: pl.BlockSpec((halo, d_conv), lambda i: (n_meta // halo - 1, j))
    const = lambda shape: pl.BlockSpec(shape, lambda i: (0, 0), pipeline_mode=pl.Buffered(1))
    vec = lambda a: a[None, :]
    return pl.pallas_call(
        functools.partial(_mix_kernel, tiles_per_batch=seq // tm, alpha=alpha),
        grid=(t_x // tm,),
        in_specs=[pl.BlockSpec((tm, d), lambda i: (i, 0)),
                  tile(0), tile(1), tile(2), prev(1), prev(2), meta(1), meta(2),
                  pl.BlockSpec((tm, d_mv), lambda i: (i, 0)),
                  pl.BlockSpec((tm, d), lambda i: (i, off_g // d)),
                  pl.BlockSpec((tm, d), lambda i: (i, off_g // d + 1)),
                  const(p_a.shape), const(p_b.shape), const(w_o.shape), const(conv_w.shape),
                  const((1, d)), const((1, d)), const((1, d)), const((1, d))],
        out_specs=pl.BlockSpec((d, tm), lambda i: (0, i)),
        out_shape=jax.ShapeDtypeStruct((d, t_x), jnp.float32),
        compiler_params=pltpu.CompilerParams(
            dimension_semantics=("parallel",), vmem_limit_bytes=VMEM_LIMIT_BYTES),
        name="mix",
    )(x2, proj_x, proj_x, proj_x, proj_x, proj_x, proj_m, proj_m, y_b, proj_x, proj_x,
      p_a, p_b, w_o, conv_w, vec(ln0_g), vec(ln0_b), vec(ln1_g), vec(ln1_b))


def kernel(x, meta_tokens, ln0_g, ln0_b, w_in, b_if, conv_w, mh_norm_w, p_a, p_b, w_o,
           ln1_g, ln1_b, peer_wq, peer_keys, peer_u, peer_v, ln2_g, ln2_b):
    depth = w_in.shape[0]
    assert depth == 1
    bsz, seq, d = x.shape
    n_meta = meta_tokens.shape[0]
    n_in = w_in.shape[-1]
    d_conv = conv_w.shape[-1]
    d_mv = mh_norm_w.shape[-1]
    n_mh = b_if.shape[-1] // 2
    d_mqk = (n_in - 3 * d_conv - 2 * d_mv - 2 * n_mh - 2 * d) // 2
    head_qk = d_mqk // n_mh
    head_v = d_mv // n_mh
    alpha = (2 * depth) ** 0.25
    bf = jnp.bfloat16
    t_x = bsz * seq

    x2 = x.reshape(t_x, d)
    gate_off = 3 * d_conv + 2 * d_mqk + 2 * d_mv
    w = w_in[0]
    n_main = n_in - 2 * n_mh
    w_main = lax.dynamic_update_slice(w[:, :n_main].astype(bf),
                                      w[:, gate_off + 2 * n_mh:].astype(bf), (0, gate_off))
    w_gif = jnp.pad(w[:, gate_off:gate_off + 2 * n_mh],
                    ((0, 0), (0, LANES - 2 * n_mh))).astype(bf)
    inproj_tn = min(INPROJ_TILE_N, w_main.shape[1])
    proj_x, gif_x, gt_x = _inproj(x2, ln0_g, ln0_b, w_main, w_gif, INPROJ_TILE_M, inproj_tn)
    proj_m, gif_m, gt_meta = _inproj(meta_tokens, ln0_g, ln0_b, w_main, w_gif, INPROJ_TILE_M,
                                     inproj_tn)

    off_q = 3 * d_conv
    off_v = off_q + 2 * d_mqk
    off_g = off_v + 2 * d_mv
    y_b = _mlstm(proj_x, proj_m, gif_x, gt_x, gif_m, gt_meta, b_if[0], mh_norm_w[0], bsz=bsz,
                 seq=seq,
                 n_heads=n_mh, head_qk=head_qk, head_v=head_v, off_q=off_q, off_v=off_v)
    h1_t = _mix(x2, proj_x, proj_m, y_b,
                p_a[0].astype(bf), p_b[0].astype(bf), w_o[0].astype(bf), conv_w[0],
                ln0_g, ln0_b, ln1_g[0], ln1_b[0], seq=seq, off_g=off_g, alpha=alpha,
                tm=MIX_TILE)

    n_ph, _, n_keys, half_key = peer_keys.shape[1:]
    w_fold = _fold_keys(peer_keys[0].reshape(n_ph * 2, n_keys, half_key), peer_wq[0])
    scores_t = _matmul(w_fold, h1_t, jnp.float32, 1024, 512)
    e0, thr, e1, rank = _peer_select(scores_t, n_ph, n_keys, LANES)
    out = _peer_dense(h1_t, e0, thr, e1, rank, peer_u[0], peer_v[0],
                      ln2_g[0], ln2_b[0], alpha=alpha, tm=PEER_TOKEN_TILE,
                      rows_per_step=PEER_ROWS_PER_STEP)
    return out.reshape(bsz, seq, d)
```

```python
import functools
import math

import jax
import jax.numpy as jnp
from jax import lax
from jax.experimental import pallas as pl
from jax.experimental.pallas import tpu as pltpu

LANES = 128
SUBLANES = 8
BF16_SUBLANES = 16
VMEM_LIMIT_BYTES = 56 * 1024 * 1024
LARGE_VMEM_LIMIT_BYTES = 58 * 1024 * 1024

TOPK = 16
LN_EPS = 1e-5
I_GATE_PAD = -1e30
MLSTM_CHUNK = 256
MIX_TILE = 256
INPROJ_TILE_M = 1024
INPROJ_TILE_N = 1024
PEER_TOKEN_TILE = 512
PEER_ROWS_PER_STEP = 8
QUANT_ROWS = 512
FP8 = jnp.float8_e4m3fn
FP8_CLIP = 448.0
FP8_TARGET = 256.0
PEER_P_SCALE = 64.0


def _ln_rows(x, g, b):
    mu = jnp.mean(x, axis=1, keepdims=True)
    xc = x - mu
    var = jnp.mean(xc * xc, axis=1, keepdims=True)
    return xc * lax.rsqrt(var + LN_EPS) * g + b


def _inproj_kernel(x_ref, g_ref, b_ref, w_ref, wg_ref, o_ref, og_ref, ogt_ref, h_ref):
    @pl.when(pl.program_id(1) == 0)
    def _():
        h = _ln_rows(x_ref[...], g_ref[...], b_ref[...]).astype(h_ref.dtype)
        h_ref[...] = h
        gates = jnp.dot(h, wg_ref[...], preferred_element_type=jnp.float32)
        og_ref[...] = gates
        ogt_ref[...] = gates.T

    o_ref[...] = jnp.dot(h_ref[...], w_ref[...],
                         preferred_element_type=jnp.float32).astype(o_ref.dtype)


def _inproj(x2, ln_g, ln_b, w_main, w_gate, tm, tn):
    m, d = x2.shape
    n = w_main.shape[1]
    tm = min(tm, m)
    assert m % tm == 0 and n % tn == 0 and w_gate.shape[1] == LANES
    const = lambda shape: pl.BlockSpec(shape, lambda i, j: (0, 0))
    return pl.pallas_call(
        _inproj_kernel,
        grid=(m // tm, n // tn),
        in_specs=[pl.BlockSpec((tm, d), lambda i, j: (i, 0)),
                  const((1, d)), const((1, d)),
                  pl.BlockSpec((d, tn), lambda i, j: (0, j)),
                  const((d, LANES))],
        out_specs=[pl.BlockSpec((tm, tn), lambda i, j: (i, j)),
                   pl.BlockSpec((tm, LANES), lambda i, j: (i, 0)),
                   pl.BlockSpec((LANES, tm), lambda i, j: (0, i))],
        out_shape=[jax.ShapeDtypeStruct((m, n), jnp.bfloat16),
                   jax.ShapeDtypeStruct((m, LANES), jnp.float32),
                   jax.ShapeDtypeStruct((LANES, m), jnp.float32)],
        scratch_shapes=[pltpu.VMEM((tm, d), jnp.bfloat16)],
        compiler_params=pltpu.CompilerParams(
            dimension_semantics=("parallel", "arbitrary"),
            vmem_limit_bytes=VMEM_LIMIT_BYTES),
        name="inproj",
    )(x2, ln_g[None, :], ln_b[None, :], w_main, w_gate)


def _mm_kernel(a_ref, b_ref, o_ref):
    bf = jnp.bfloat16
    o_ref[...] = jnp.dot(a_ref[...].astype(bf), b_ref[...].astype(bf),
                         preferred_element_type=jnp.float32).astype(o_ref.dtype)


def _matmul(a, b, out_dtype, tm, tn):
    m, k = a.shape
    _, n = b.shape
    tm = min(tm, m)
    tn = min(tn, n)
    assert m % tm == 0 and n % tn == 0
    return pl.pallas_call(
        _mm_kernel,
        grid=(m // tm, n // tn),
        in_specs=[pl.BlockSpec((tm, k), lambda i, j: (i, 0)),
                  pl.BlockSpec((k, tn), lambda i, j: (0, j))],
        out_specs=pl.BlockSpec((tm, tn), lambda i, j: (i, j)),
        out_shape=jax.ShapeDtypeStruct((m, n), out_dtype),
        compiler_params=pltpu.CompilerParams(
            dimension_semantics=("parallel", "parallel"),
            vmem_limit_bytes=VMEM_LIMIT_BYTES),
        name="matmul",
    )(a, b)


def _pow2_scale(amax):
    safe = jnp.where(amax > 0, amax, FP8_TARGET)
    return jnp.exp2(jnp.floor(jnp.log2(FP8_TARGET / safe)))


def _quantize_rows_kernel(x_ref, q_ref, inv_ref, *, transpose):
    x = x_ref[...]
    scale = _pow2_scale(jnp.max(jnp.abs(x), axis=1, keepdims=True))
    xs = x * scale
    q_ref[...] = (xs.T if transpose else xs).astype(q_ref.dtype)
    inv_ref[...] = jnp.broadcast_to(1.0 / scale, inv_ref.shape)


def _quantize_rows(x, rows, transpose):
    n, d = x.shape
    rows = min(rows, n)
    assert n % rows == 0
    q_shape, q_spec = (((d, n), pl.BlockSpec((d, rows), lambda i: (0, i))) if transpose
                       else ((n, d), pl.BlockSpec((rows, d), lambda i: (i, 0))))
    return pl.pallas_call(
        functools.partial(_quantize_rows_kernel, transpose=transpose),
        grid=(n // rows,),
        in_specs=[pl.BlockSpec((rows, d), lambda i: (i, 0))],
        out_specs=[q_spec, pl.BlockSpec((rows, LANES), lambda i: (i, 0))],
        out_shape=[jax.ShapeDtypeStruct(q_shape, FP8),
                   jax.ShapeDtypeStruct((n, LANES), jnp.float32)],
        compiler_params=pltpu.CompilerParams(
            dimension_semantics=("parallel",), vmem_limit_bytes=VMEM_LIMIT_BYTES),
        name="quantize_rows",
    )(x)


def _peer_dense_kernel(ht_ref, e0_ref, thr_ref, e1_ref, rank_ref, u_ref, vt_ref, c1_ref, c2_ref,
                       post_ref, g_ref, b_ref, o_ref, hb_ref, sh_ref, acc_ref, w_ref, p_ref,
                       *, n_heads, rows_per_step, alpha):
    k = pl.program_id(1)
    bf, f8 = jnp.bfloat16, FP8
    tm = ht_ref.shape[1]
    zero = jnp.zeros((), bf)

    def row_tile(ref, h, ii):
        row = jnp.broadcast_to(ref[h, ii:ii + 1, :], (BF16_SUBLANES, tm)).astype(bf)
        return pltpu.repeat(row, LANES // BF16_SUBLANES, axis=0)

    @pl.when(k == 0)
    def _():
        acc_ref[...] = jnp.zeros_like(acc_ref)
        h = ht_ref[...]
        amax = jnp.max(jnp.max(jnp.abs(h), axis=0, keepdims=True), axis=1, keepdims=True)
        s_h = _pow2_scale(amax)
        hb_ref[...] = (h * s_h).astype(f8)
        sh_ref[...] = jnp.broadcast_to(1.0 / s_h, sh_ref.shape)

    for ii in range(rows_per_step):
        w = jnp.zeros((LANES, tm), bf)
        for h in range(n_heads):
            sel = rank_ref[h] < row_tile(thr_ref, h, ii)
            w = w + jnp.where(sel, e1_ref[h], zero) * row_tile(e0_ref, h, ii)
        w_ref[ii * LANES:(ii + 1) * LANES, :] = w
    a = jnp.dot(u_ref[...], hb_ref[...], preferred_element_type=jnp.float32)
    inv_sh = sh_ref[0:1, 0:1]
    c1 = pltpu.repeat((c1_ref[...] * inv_sh).astype(bf), tm // LANES, axis=1)
    c2 = pltpu.repeat((c2_ref[...] * inv_sh).astype(bf), tm // LANES, axis=1)
    ab = a.astype(bf)
    act = (ab * c1) * (1.0 + lax.erf(ab * c2))
    p = jnp.clip(w_ref[...] * act, -FP8_CLIP, FP8_CLIP)
    p_ref[...] = p.astype(f8)
    acc_ref[...] += jnp.dot(vt_ref[...], p_ref[...], preferred_element_type=jnp.float32)

    @pl.when(k == pl.num_programs(1) - 1)
    def _():
        y = alpha * ht_ref[...] + acc_ref[...] * post_ref[0:1, 0:1]
        mu = jnp.mean(y, axis=0, keepdims=True)
        yc = y - mu
        var = jnp.mean(yc * yc, axis=0, keepdims=True)
        o_ref[...] = (yc * lax.rsqrt(var + LN_EPS)).T * g_ref[...] + b_ref[...]


def _peer_dense(ht, e0, thr, e1, rank, peer_u, peer_v, ln_g, ln_b, *, alpha, tm, rows_per_step):
    u, inv_u = _quantize_rows(peer_u, QUANT_ROWS, transpose=False)
    vt, inv_v = _quantize_rows(peer_v, QUANT_ROWS, transpose=True)
    inv_v_max = jnp.max(inv_v)
    c1 = inv_u * (inv_v / inv_v_max) * (0.5 * PEER_P_SCALE)
    c2 = inv_u * (2.0 ** -0.5)
    post = jnp.broadcast_to(inv_v_max / PEER_P_SCALE, (1, LANES)).astype(jnp.float32)
    d, t = ht.shape
    n_heads, n_keys, _ = e0.shape
    assert n_keys == LANES
    n_experts = u.shape[0]
    eb = rows_per_step * LANES
    sel_spec = pl.BlockSpec((n_heads, n_keys, tm), lambda i, k: (0, 0, i))
    row_spec = pl.BlockSpec((n_heads, rows_per_step, tm), lambda i, k: (0, k, i))
    vec_spec = pl.BlockSpec((1, d), lambda i, k: (0, 0))
    return pl.pallas_call(
        functools.partial(_peer_dense_kernel, n_heads=n_heads, rows_per_step=rows_per_step,
                          alpha=alpha),
        grid=(t // tm, n_experts // eb),
        in_specs=[pl.BlockSpec((d, tm), lambda i, k: (0, i)),
                  row_spec, row_spec, sel_spec, sel_spec,
                  pl.BlockSpec((eb, d), lambda i, k: (k, 0)),
                  pl.BlockSpec((d, eb), lambda i, k: (0, k)),
                  pl.BlockSpec((eb, LANES), lambda i, k: (k, 0)),
                  pl.BlockSpec((eb, LANES), lambda i, k: (k, 0)),
                  pl.BlockSpec((1, LANES), lambda i, k: (0, 0)),
                  vec_spec, vec_spec],
        out_specs=pl.BlockSpec((tm, d), lambda i, k: (i, 0)),
        out_shape=jax.ShapeDtypeStruct((t, d), jnp.float32),
        scratch_shapes=[pltpu.VMEM((d, tm), FP8),
                        pltpu.VMEM((SUBLANES, LANES), jnp.float32),
                        pltpu.VMEM((d, tm), jnp.float32),
                        pltpu.VMEM((eb, tm), jnp.bfloat16),
                        pltpu.VMEM((eb, tm), FP8)],
        compiler_params=pltpu.CompilerParams(
            dimension_semantics=("parallel", "arbitrary"),
            vmem_limit_bytes=LARGE_VMEM_LIMIT_BYTES),
        name="peer_dense",
    )(ht, e0, thr, e1, rank, u, vt, c1, c2, post, ln_g[None, :], ln_b[None, :])


def _fold_keys_kernel(keys_ref, wq_ref, o_ref):
    o_ref[...] = lax.dot_general(
        keys_ref[0], wq_ref[...], (((1,), (1,)), ((), ())),
        precision=lax.Precision.HIGHEST, preferred_element_type=jnp.float32)


def _fold_keys(keys, wq):
    n_blocks, n_keys, half_key = keys.shape
    d = wq.shape[0]
    return pl.pallas_call(
        _fold_keys_kernel,
        grid=(n_blocks,),
        in_specs=[pl.BlockSpec((1, n_keys, half_key), lambda b: (b, 0, 0)),
                  pl.BlockSpec((d, half_key), lambda b: (0, b))],
        out_specs=pl.BlockSpec((n_keys, d), lambda b: (b, 0)),
        out_shape=jax.ShapeDtypeStruct((n_blocks * n_keys, d), jnp.float32),
        compiler_params=pltpu.CompilerParams(dimension_semantics=("parallel",)),
        name="fold_keys",
    )(keys, wq)


def _batcher_pairs(n):
    pairs = []
    p = 1
    while p < n:
        k = p
        while k >= 1:
            for j in range(k % p, n - k, 2 * k):
                for i in range(min(k, n - j - k)):
                    if (i + j) // (2 * p) == (i + j + k) // (2 * p):
                        pairs.append((i + j, i + j + k))
            k //= 2
        p *= 2
    return pairs


_SORT_PAIRS = _batcher_pairs(TOPK)
_MERGE_PAIRS = [(i, i + d) for d in (8, 4, 2, 1) for i in range(TOPK) if not i & d]
_SUBLANE_SHIFTS = (4, 2, 1)
UNRANKED = 100.0


def _compare_exchange(v, pairs):
    v = list(v)
    for i, j in pairs:
        v[i], v[j] = jnp.maximum(v[i], v[j]), jnp.minimum(v[i], v[j])
    return v


def _top16(tiles):
    v = _compare_exchange(tiles, _SORT_PAIRS)
    for shift in _SUBLANE_SHIFTS:
        other = [pltpu.roll(x, shift, 0) for x in v]
        v = [jnp.maximum(v[r], other[TOPK - 1 - r]) for r in range(TOPK)]
        v = _compare_exchange(v, _MERGE_PAIRS)
    return v


def _stack_rows(tiles, sub):
    out = tiles[0]
    for r in range(1, SUBLANES):
        out = jnp.where(sub == r, tiles[r], out)
    return out


def _peer_select_kernel(sc_ref, e0_ref, thr_ref, e1_ref, rank_ref, *, n_heads, n_keys):
    tl = sc_ref.shape[1]
    reps = n_keys // SUBLANES
    half = TOPK // 2
    sub = lax.broadcasted_iota(jnp.int32, (SUBLANES, tl), 0)
    spread = lambda tile: pltpu.repeat(tile, reps, axis=0)

    def head_body(h, carry):
        r0 = pl.multiple_of(h * 2 * n_keys, 2 * n_keys)
        s0 = sc_ref[pl.ds(r0, n_keys), :]
        s1 = sc_ref[pl.ds(r0 + n_keys, n_keys), :]
        a = _top16([s0[SUBLANES * r:SUBLANES * (r + 1), :] for r in range(reps)])
        b = _top16([s1[SUBLANES * r:SUBLANES * (r + 1), :] for r in range(reps)])
        b_lo, b_hi = _stack_rows(b[:half], sub), _stack_rows(b[half:], sub)
        a_hi = _stack_rows(a[half:], sub)
        cand = ([a[0] + b_lo, a[0] + b_hi] + [a[p] + b_lo for p in range(1, half)]
                + [a_hi + b[0]])
        cand += [jnp.full((SUBLANES, tl), -jnp.inf, jnp.float32)] * (TOPK - len(cand))
        best = _top16(cand)
        tau = best[TOPK - 1]
        zsum = sum(jnp.exp(c - best[0]) for c in best)
        thr = jnp.zeros((n_keys, tl), jnp.float32)
        rank1 = jnp.full((n_keys, tl), UNRANKED, jnp.float32)
        for p in reversed(range(TOPK)):
            n_sel = (jnp.where(a[p] + b_lo >= tau, 1.0, 0.0)
                     + jnp.where(a[p] + b_hi >= tau, 1.0, 0.0))
            for shift in _SUBLANE_SHIFTS:
                n_sel = n_sel + pltpu.roll(n_sel, shift, 0)
            thr = jnp.where(s0 == spread(a[p]), spread(n_sel), thr)
            rank1 = jnp.where(s1 == spread(b[p]), float(p), rank1)
        e0_ref[h] = jnp.exp(s0 - spread(a[0])) / spread(zsum)
        thr_ref[h] = thr
        e1_ref[h] = jnp.exp(s1 - spread(b[0])).astype(e1_ref.dtype)
        rank_ref[h] = rank1.astype(rank_ref.dtype)
        return carry

    lax.fori_loop(0, n_heads, head_body, 0)


def _peer_select(scores_t, n_heads, n_keys, tl):
    rows, t = scores_t.shape
    assert rows == n_heads * 2 * n_keys and n_keys == SUBLANES * TOPK and TOPK == 2 * SUBLANES
    out_spec = pl.BlockSpec((n_heads, n_keys, tl), lambda i: (0, 0, i))
    f32 = jax.ShapeDtypeStruct((n_heads, n_keys, t), jnp.float32)
    b16 = jax.ShapeDtypeStruct((n_heads, n_keys, t), jnp.bfloat16)
    return pl.pallas_call(
        functools.partial(_peer_select_kernel, n_heads=n_heads, n_keys=n_keys),
        grid=(t // tl,),
        in_specs=[pl.BlockSpec((rows, tl), lambda i: (0, i))],
        out_specs=[out_spec, out_spec, out_spec, out_spec],
        out_shape=[f32, f32, b16, b16],
        compiler_params=pltpu.CompilerParams(
            dimension_semantics=("parallel",), vmem_limit_bytes=VMEM_LIMIT_BYTES),
        name="peer_select",
    )(scores_t)


def _log_sigmoid(x):
    return jnp.minimum(x, 0.0) - jnp.log1p(jnp.exp(-jnp.abs(x)))


def _mlstm_kernel(q_ref, k_ref, v_ref, o_ref, g_ref, gt_ref,
                  qm_ref, km_ref, vm_ref, om_ref, gm_ref, gmt_ref,
                  bias_ref, biast_ref, nw_ref, y_ref, c_ref, m_ref,
                  *, n_heads, head_qk, head_v):
    c = pl.program_id(0)
    n_batch, tc = q_ref.shape[0], q_ref.shape[1]
    f32, bf = jnp.float32, jnp.bfloat16
    first = c == 0

    @pl.when(first)
    def _():
        c_ref[...] = jnp.zeros_like(c_ref)
        m_ref[...] = jnp.zeros_like(m_ref)

    row = lax.broadcasted_iota(jnp.int32, (tc, tc), 0)
    col = lax.broadcasted_iota(jnp.int32, (tc, tc), 1)
    causal = col <= row
    tri = jnp.where(causal, 1.0, 0.0).astype(f32)
    ln_scale = -0.5 * math.log(head_qk)
    ones_blk = jnp.where(lax.broadcasted_iota(jnp.int32, (tc, LANES), 1) == 0, 1.0, 0.0).astype(bf)

    for b in range(n_batch):
        _mlstm_chunk(b, first, causal, tri, ln_scale, ones_blk,
                     q_ref, k_ref, v_ref, o_ref, g_ref, gt_ref,
                     qm_ref, km_ref, vm_ref, om_ref, gm_ref, gmt_ref,
                     bias_ref, biast_ref, nw_ref, y_ref, c_ref, m_ref,
                     n_heads=n_heads, head_qk=head_qk, head_v=head_v)


def _mlstm_chunk(b, first, causal, tri, ln_scale, ones_blk,
                 q_ref, k_ref, v_ref, o_ref, g_ref, gt_ref,
                 qm_ref, km_ref, vm_ref, om_ref, gm_ref, gmt_ref,
                 bias_ref, biast_ref, nw_ref, y_ref, c_ref, m_ref,
                 *, n_heads, head_qk, head_v):
    tc = q_ref.shape[1]
    f32, bf = jnp.float32, jnp.bfloat16
    q = jnp.where(first, qm_ref[...], q_ref[b])
    k = jnp.where(first, km_ref[...], k_ref[b])
    v = jnp.where(first, vm_ref[...], v_ref[b])
    o = jnp.where(first, om_ref[...], o_ref[b])
    g = jnp.where(first, gm_ref[...], g_ref[b]) + bias_ref[...]
    gt = jnp.where(first, gmt_ref[...], gt_ref[b]) + biast_ref[...]
    b_cols = jnp.dot(tri, _log_sigmoid(g), precision=lax.Precision.HIGHEST,
                     preferred_element_type=f32)
    b_rows = lax.dot_general(_log_sigmoid(gt), tri, (((1,), (1,)), ((), ())),
                             precision=lax.Precision.HIGHEST, preferred_element_type=f32)

    for h in range(n_heads):
        qh = q[:, h * head_qk:(h + 1) * head_qk]
        kh = k[:, h * head_qk:(h + 1) * head_qk]
        v_ext = jnp.concatenate([v[:, h * head_v:(h + 1) * head_v], ones_blk], axis=1)
        li_col = g[:, h:h + 1]
        b_col = b_cols[:, n_heads + h:n_heads + h + 1]
        li_row = gt[h:h + 1, :]
        b_row = b_rows[n_heads + h:n_heads + h + 1, :]
        m_prev = m_ref[b, h:h + 1, 0:1]
        d = jnp.where(causal, b_col + (li_row - b_row), -jnp.inf)
        inter = b_col + m_prev
        m_t = jnp.maximum(inter, jnp.max(d, axis=1, keepdims=True))
        w = jnp.exp(d - (m_t - ln_scale))
        s = lax.dot_general(qh, kh, (((1,), (1,)), ((), ())), preferred_element_type=f32) * w
        sc_inter = jnp.exp(inter - (m_t - ln_scale))
        c_ext = c_ref[b, h]
        num_ext = (jnp.dot(s.astype(bf), v_ext, preferred_element_type=f32)
                   + sc_inter * jnp.dot(qh, c_ext.astype(bf), preferred_element_type=f32))
        den = num_ext[:, head_v:head_v + 1]
        hm = num_ext[:, :head_v] / jnp.maximum(jnp.abs(den), jnp.exp(-m_t))
        hm = hm * lax.rsqrt(jnp.mean(hm * hm, axis=1, keepdims=True) + LN_EPS)
        gate_o = jax.nn.sigmoid(o[:, h * head_v:(h + 1) * head_v].astype(f32))
        y_ref[b, :, h * head_v:(h + 1) * head_v] = (
            gate_o * (hm * nw_ref[:, h * head_v:(h + 1) * head_v])).astype(y_ref.dtype)
        b_last = b_col[tc - 1:tc, :]
        dl = b_last - b_col + li_col
        m_new = jnp.maximum(b_last + m_prev, jnp.max(dl, axis=0, keepdims=True))
        wv = (jnp.exp(dl - m_new) * v_ext.astype(f32)).astype(bf)
        c_ref[b, h] = (jnp.exp(b_last + m_prev - m_new) * c_ext
                       + lax.dot_general(kh, wv, (((0,), (0,)), ((), ())),
                                         preferred_element_type=f32))
        m_ref[b, h:h + 1, :] = jnp.broadcast_to(m_new, (1, LANES))


def _mlstm(proj_x, proj_m, gif_x, gt_x, gif_m, gt_meta, b_if, norm_w, *, bsz, seq, n_heads,
           head_qk, head_v, off_q, off_v):
    tc = min(MLSTM_CHUNK, seq)
    assert seq % tc == 0
    n_meta = proj_m.shape[0]
    d_mqk, d_mv = n_heads * head_qk, n_heads * head_v
    assert off_q % d_mqk == 0 and off_v % d_mv == 0 and n_meta <= tc
    nc_x = seq // tc
    bf = jnp.bfloat16
    lead = tc - n_meta
    pad_rows = lambda a: jnp.pad(a, ((lead, 0), (0, 0)))
    qm = pad_rows(proj_m[:, off_q:off_q + d_mqk])
    km = pad_rows(proj_m[:, off_q + d_mqk:off_q + 2 * d_mqk])
    vm = pad_rows(proj_m[:, off_v:off_v + d_mv])
    om = pad_rows(proj_m[:, off_v + d_mv:off_v + 2 * d_mv])
    pad_gate = jnp.concatenate([jnp.full((lead, n_heads), I_GATE_PAD, jnp.float32),
                                jnp.full((lead, n_heads), -I_GATE_PAD, jnp.float32),
                                jnp.zeros((lead, LANES - 2 * n_heads), jnp.float32)], axis=1)
    gm = jnp.concatenate([pad_gate, gif_m], axis=0)
    bias = jnp.pad(b_if, (0, LANES - 2 * n_heads))[None, :]
    biast = jnp.broadcast_to(b_if[:, None], (2 * n_heads, tc))
    pad_gate_t = jnp.concatenate([jnp.full((n_heads, lead), I_GATE_PAD, jnp.float32),
                                  jnp.full((n_heads, lead), -I_GATE_PAD, jnp.float32)], axis=0)
    gt_m = jnp.concatenate([pad_gate_t, gt_meta[:2 * n_heads]], axis=1)

    proj3 = proj_x.reshape(bsz, seq, proj_x.shape[1])
    gif3 = gif_x.reshape(bsz, seq, LANES)
    gt3 = gt_x.reshape(LANES, bsz, seq).transpose(1, 0, 2)
    xchunk = lambda c: jnp.maximum(c - 1, 0)
    const = lambda c: (0, 0)
    y = pl.pallas_call(
        functools.partial(_mlstm_kernel, n_heads=n_heads, head_qk=head_qk, head_v=head_v),
        grid=(nc_x + 1,),
        in_specs=[pl.BlockSpec((bsz, tc, d_mqk), lambda c: (0, xchunk(c), off_q // d_mqk)),
                  pl.BlockSpec((bsz, tc, d_mqk), lambda c: (0, xchunk(c), off_q // d_mqk + 1)),
                  pl.BlockSpec((bsz, tc, d_mv), lambda c: (0, xchunk(c), off_v // d_mv)),
                  pl.BlockSpec((bsz, tc, d_mv), lambda c: (0, xchunk(c), off_v // d_mv + 1)),
                  pl.BlockSpec((bsz, tc, LANES), lambda c: (0, xchunk(c), 0)),
                  pl.BlockSpec((bsz, 2 * n_heads, tc), lambda c: (0, 0, xchunk(c))),
                  pl.BlockSpec((tc, d_mqk), const), pl.BlockSpec((tc, d_mqk), const),
                  pl.BlockSpec((tc, d_mv), const), pl.BlockSpec((tc, d_mv), const),
                  pl.BlockSpec((tc, LANES), const), pl.BlockSpec((2 * n_heads, tc), const),
                  pl.BlockSpec((1, LANES), const), pl.BlockSpec((2 * n_heads, tc), const),
                  pl.BlockSpec((1, d_mv), const)],
        out_specs=pl.BlockSpec((bsz, tc, d_mv), lambda c: (0, xchunk(c), 0)),
        out_shape=jax.ShapeDtypeStruct((bsz, seq, d_mv), bf),
        scratch_shapes=[pltpu.VMEM((bsz, n_heads, head_qk, head_v + LANES), jnp.float32),
                        pltpu.VMEM((bsz, n_heads, LANES), jnp.float32)],
        compiler_params=pltpu.CompilerParams(
            dimension_semantics=("arbitrary",), vmem_limit_bytes=VMEM_LIMIT_BYTES),
        name="mlstm",
    )(proj3, proj3, proj3, proj3, gif3, gt3, qm, km, vm, om, gm, gt_m,
      bias, biast, norm_w[None, :])
    return y.reshape(bsz * seq, d_mv)


def _mix_kernel(x_ref, cb_ref, cc_ref, cx_ref, cch_ref, cxh_ref, ccm_ref, cxm_ref,
                yb_ref, ga_ref, gb_ref, pa_ref, pb_ref, wo_ref, cw_ref,
                g0_ref, b0_ref, g1_ref, b1_ref, h1t_ref,
                *, tiles_per_batch, alpha):
    i = pl.program_id(0)
    f32, bf = jnp.float32, jnp.bfloat16
    tm = x_ref.shape[0]
    z = cc_ref[...].astype(f32) * cx_ref[...].astype(f32)
    batch_start = (i % tiles_per_batch) == 0
    halo = jnp.where(batch_start,
                     ccm_ref[...].astype(f32) * cxm_ref[...].astype(f32),
                     cch_ref[...].astype(f32) * cxh_ref[...].astype(f32))
    sub = lax.broadcasted_iota(jnp.int32, halo.shape, 0)
    n_halo = halo.shape[0]
    y = cw_ref[2:3, :] * z
    for shift in (1, 2):
        rolled = pltpu.roll(z, shift, 0)
        head = jnp.where(sub < shift, pltpu.roll(halo, shift, 0), rolled[:n_halo])
        y = y + cw_ref[2 - shift:3 - shift, :] * jnp.concatenate([head, rolled[n_halo:]], axis=0)
    y_a = (cb_ref[...].astype(f32) * y).astype(bf)
    za = jnp.dot(y_a, pa_ref[...], preferred_element_type=f32)
    zb = jnp.dot(yb_ref[...], pb_ref[...], preferred_element_type=f32)
    merged = (jax.nn.sigmoid(ga_ref[...].astype(f32)) * za
              + jax.nn.sigmoid(gb_ref[...].astype(f32)) * zb).astype(bf)
    mix = jnp.dot(merged, wo_ref[...], preferred_element_type=f32)
    h0 = _ln_rows(x_ref[...], g0_ref[...], b0_ref[...])
    h1t_ref[...] = _ln_rows(alpha * h0 + mix, g1_ref[...], b1_ref[...]).T


def _mix(x2, proj_x, proj_m, y_b, p_a, p_b, w_o, conv_w, ln0_g, ln0_b, ln1_g, ln1_b,
         *, seq, off_g, alpha, tm):
    t_x, d = x2.shape
    d_conv = conv_w.shape[-1]
    d_mv = y_b.shape[-1]
    n_meta = proj_m.shape[0]
    halo = SUBLANES
    assert seq % tm == 0 and tm % halo == 0 and n_meta % halo == 0 and off_g % d == 0
    assert conv_w.shape[0] == 3
    tile = lambda j: pl.BlockSpec((tm, d_conv), lambda i: (i, j))
    prev = lambda j: pl.BlockSpec((halo, d_conv),
                                  lambda i: (jnp.maximum(i * (tm // halo) - 1, 0), j))
    meta = lambda j: pl.BlockSpec((halo, d_conv), lambda i: (n_meta // halo - 1, j))
    const = lambda shape: pl.BlockSpec(shape, lambda i: (0, 0), pipeline_mode=pl.Buffered(1))
    vec = lambda a: a[None, :]
    return pl.pallas_call(
        functools.partial(_mix_kernel, tiles_per_batch=seq // tm, alpha=alpha),
        grid=(t_x // tm,),
        in_specs=[pl.BlockSpec((tm, d), lambda i: (i, 0)),
                  tile(0), tile(1), tile(2), prev(1), prev(2), meta(1), meta(2),
                  pl.BlockSpec((tm, d_mv), lambda i: (i, 0)),
                  pl.BlockSpec((tm, d), lambda i: (i, off_g // d)),
                  pl.BlockSpec((tm, d), lambda i: (i, off_g // d + 1)),
                  const(p_a.shape), const(p_b.shape), const(w_o.shape), const(conv_w.shape),
                  const((1, d)), const((1, d)), const((1, d)), const((1, d))],
        out_specs=pl.BlockSpec((d, tm), lambda i: (0, i)),
        out_shape=jax.ShapeDtypeStruct((d, t_x), jnp.float32),
        compiler_params=pltpu.CompilerParams(
            dimension_semantics=("parallel",), vmem_limit_bytes=VMEM_LIMIT_BYTES),
        name="mix",
    )(x2, proj_x, proj_x, proj_x, proj_x, proj_x, proj_m, proj_m, y_b, proj_x, proj_x,
      p_a, p_b, w_o, conv_w, vec(ln0_g), vec(ln0_b), vec(ln1_g), vec(ln1_b))


def kernel(x, meta_tokens, ln0_g, ln0_b, w_in, b_if, conv_w, mh_norm_w, p_a, p_b, w_o,
           ln1_g, ln1_b, peer_wq, peer_keys, peer_u, peer_v, ln2_g, ln2_b):
    depth = w_in.shape[0]
    assert depth == 1
    bsz, seq, d = x.shape
    n_meta = meta_tokens.shape[0]
    n_in = w_in.shape[-1]
    d_conv = conv_w.shape[-1]
    d_mv = mh_norm_w.shape[-1]
    n_mh = b_if.shape[-1] // 2
    d_mqk = (n_in - 3 * d_conv - 2 * d_mv - 2 * n_mh - 2 * d) // 2
    head_qk = d_mqk // n_mh
    head_v = d_mv // n_mh
    alpha = (2 * depth) ** 0.25
    bf = jnp.bfloat16
    t_x = bsz * seq

    x2 = x.reshape(t_x, d)
    gate_off = 3 * d_conv + 2 * d_mqk + 2 * d_mv
    w = w_in[0]
    n_main = n_in - 2 * n_mh
    w_main = lax.dynamic_update_slice(w[:, :n_main].astype(bf),
                                      w[:, gate_off + 2 * n_mh:].astype(bf), (0, gate_off))
    w_gif = jnp.pad(w[:, gate_off:gate_off + 2 * n_mh],
                    ((0, 0), (0, LANES - 2 * n_mh))).astype(bf)
    inproj_tn = min(INPROJ_TILE_N, w_main.shape[1])
    proj_x, gif_x, gt_x = _inproj(x2, ln0_g, ln0_b, w_main, w_gif, INPROJ_TILE_M, inproj_tn)
    proj_m, gif_m, gt_meta = _inproj(meta_tokens, ln0_g, ln0_b, w_main, w_gif, INPROJ_TILE_M,
                                     inproj_tn)

    off_q = 3 * d_conv
    off_v = off_q + 2 * d_mqk
    off_g = off_v + 2 * d_mv
    y_b = _mlstm(proj_x, proj_m, gif_x, gt_x, gif_m, gt_meta, b_if[0], mh_norm_w[0], bsz=bsz,
                 seq=seq,
                 n_heads=n_mh, head_qk=head_qk, head_v=head_v, off_q=off_q, off_v=off_v)
    h1_t = _mix(x2, proj_x, proj_m, y_b,
                p_a[0].astype(bf), p_b[0].astype(bf), w_o[0].astype(bf), conv_w[0],
                ln0_g, ln0_b, ln1_g[0], ln1_b[0], seq=seq, off_g=off_g, alpha=alpha,
                tm=MIX_TILE)

    n_ph, _, n_keys, half_key = peer_keys.shape[1:]
    w_fold = _fold_keys(peer_keys[0].reshape(n_ph * 2, n_keys, half_key), peer_wq[0])
    scores_t = _matmul(w_fold, h1_t, jnp.float32, 1024, 512)
    e0, thr, e1, rank = _peer_select(scores_t, n_ph, n_keys, 2 * LANES)
    out = _peer_dense(h1_t, e0, thr, e1, rank, peer_u[0], peer_v[0],
                      ln2_g[0], ln2_b[0], alpha=alpha, tm=PEER_TOKEN_TILE,
                      rows_per_step=PEER_ROWS_PER_STEP)
    return out.reshape(bsz, seq, d)
```
